```python
import math
import jax, jax.numpy as jnp
from jax import lax
import numpy as np

D_MODEL = 1024
BATCH = 8
SEQ = 2048
DEPTH = 1

HEAD_DIM = 64
HEADS_PER_GROUP = 8
ATTN_GROUPS = ((128, 1), (512, 4), (2048, 16))
N_ATTN_GROUPS = len(ATTN_GROUPS)
ATTN_WIDTH = HEADS_PER_GROUP * HEAD_DIM
Q_BLOCK = 128
ROPE_THETA = 500000.0
ROPE_DIMS = HEAD_DIM // 4
GMLP_WIDTH = 512
GMLP_GROUPS = 8
GMLP_GROUP_DIM = GMLP_WIDTH // GMLP_GROUPS
CHUNK = 128
EPS = 1e-6

QKV_COLS = N_ATTN_GROUPS * 3 * ATTN_WIDTH
OFF_GATE_A = QKV_COLS
OFF_Z_B = OFF_GATE_A + ATTN_WIDTH
OFF_GATE_B = OFF_Z_B + 2 * GMLP_WIDTH
OFF_MERGE_A = OFF_GATE_B + GMLP_WIDTH
OFF_MERGE_B = OFF_MERGE_A + D_MODEL
IN_COLS = OFF_MERGE_B + D_MODEL

kernel_name = "hybrid_dilated_attn_gmlp_block"


def rms_norm(x, g):
    xf = x.astype(jnp.float32)
    y = xf * lax.rsqrt(jnp.mean(xf * xf, axis=-1, keepdims=True) + EPS)
    return (y * g.astype(jnp.float32)).astype(x.dtype)


def layer_norm(x, g, b):
    xf = x.astype(jnp.float32)
    mu = jnp.mean(xf, axis=-1, keepdims=True)
    var = jnp.mean(jnp.square(xf - mu), axis=-1, keepdims=True)
    y = (xf - mu) * lax.rsqrt(var + EPS) * g.astype(jnp.float32) + b.astype(jnp.float32)
    return y.astype(x.dtype)


def partial_rope(x, positions):
    half = ROPE_DIMS // 2
    freqs = ROPE_THETA ** (-jnp.arange(0, ROPE_DIMS, 2, dtype=jnp.float32) / ROPE_DIMS)
    ang = positions.astype(jnp.float32)[..., None] * freqs
    cos = jnp.cos(ang)[:, :, None, :]
    sin = jnp.sin(ang)[:, :, None, :]
    xf = x.astype(jnp.float32)
    x1 = xf[..., :half]
    x2 = xf[..., half:ROPE_DIMS]
    rot = jnp.concatenate([x1 * cos - x2 * sin, x2 * cos + x1 * sin, xf[..., ROPE_DIMS:]], axis=-1)
    return rot.astype(x.dtype)


def dilated_window_attention(q, k, v, window, dilation):
    B, S, H, E = q.shape
    n_back = window // dilation
    L = S // dilation
    n_blk = -(-L // Q_BLOCK)
    Lp = n_blk * Q_BLOCK
    kb_len = Q_BLOCK + n_back
    qr = q.astype(jnp.float32).reshape(B, L, dilation, H, E)
    kr = k.astype(jnp.float32).reshape(B, L, dilation, H, E)
    vr = v.astype(jnp.float32).reshape(B, L, dilation, H, E)
    qp = jnp.pad(qr, ((0, 0), (0, Lp - L), (0, 0), (0, 0), (0, 0)))
    qb = qp.reshape(B, n_blk, Q_BLOCK, dilation, H, E)
    pad_k = ((0, 0), (n_back, Lp - L), (0, 0), (0, 0), (0, 0))
    kp = jnp.pad(kr, pad_k)
    vp = jnp.pad(vr, pad_k)
    idx = (jnp.arange(n_blk) * Q_BLOCK)[:, None] + jnp.arange(kb_len)[None, :]
    kb = kp[:, idx]
    vb = vp[:, idx]
    s = jnp.einsum('bnqrhe,bnkrhe->brhnqk', qb, kb) * (1.0 / math.sqrt(E))
    qq = jnp.arange(Q_BLOCK)[:, None]
    kk = jnp.arange(kb_len)[None, :]
    blk = jnp.arange(n_blk)[:, None, None]
    diff = qq + n_back - kk
    key_m = blk * Q_BLOCK - n_back + kk
    valid = (diff >= 0) & (diff <= n_back) & (key_m >= 0)
    s = jnp.where(valid, s, -jnp.inf)
    m = jnp.max(s, axis=-1, keepdims=True)
    p = jnp.exp(s - m)
    den = jnp.sum(p, axis=-1)
    lse = m[..., 0] + jnp.log(den)
    o = jnp.einsum('brhnqk,bnkrhe->bnqrhe', p, vb)
    o = o / jnp.transpose(den, (0, 3, 4, 1, 2))[..., None]
    o = o.reshape(B, Lp, dilation, H, E)[:, :L].reshape(B, S, H, E)
    lse = jnp.transpose(lse, (0, 3, 4, 1, 2)).reshape(B, Lp, dilation, H)[:, :L].reshape(B, S, H)
    return o, lse


def setup_inputs(seed: int = 0) -> dict:
    key = jax.random.key(seed)
    ks = jax.random.split(key, 16)
    D = D_MODEL
    f32 = jnp.float32
    x = jax.random.normal(ks[0], (BATCH, SEQ, D), f32)
    c = jax.random.normal(ks[1], (BATCH, D), f32)
    offs = jax.random.randint(ks[2], (BATCH, 1), 0, 1024, dtype=jnp.int32)
    positions = offs + jnp.arange(SEQ, dtype=jnp.int32)[None, :]
    norm_g = 1.0 + 0.02 * jax.random.normal(ks[3], (D,), f32)
    w_ada = 0.2 * jax.random.normal(ks[4], (D, 3 * D), f32) * D ** -0.5
    b_ada = 0.02 * jax.random.normal(ks[5], (3 * D,), f32)
    w_in = jax.random.normal(ks[6], (D, IN_COLS), f32) * D ** -0.5
    q_norm_g = 1.0 + 0.02 * jax.random.normal(ks[7], (N_ATTN_GROUPS, HEAD_DIM), f32)
    k_norm_g = 1.0 + 0.02 * jax.random.normal(ks[8], (N_ATTN_GROUPS, HEAD_DIM), f32)
    sgu_ln_g = 1.0 + 0.02 * jax.random.normal(ks[9], (GMLP_WIDTH,), f32)
    sgu_ln_b = 0.02 * jax.random.normal(ks[10], (GMLP_WIDTH,), f32)
    w_spatial = 0.5 * jax.random.normal(ks[11], (GMLP_GROUPS, CHUNK, CHUNK), f32) * CHUNK ** -0.5
    b_spatial = 1.0 + 0.02 * jax.random.normal(ks[12], (GMLP_GROUPS, CHUNK), f32)
    w_branch_a = jax.random.normal(ks[13], (ATTN_WIDTH, D), f32) * ATTN_WIDTH ** -0.5
    w_branch_b = jax.random.normal(ks[14], (GMLP_WIDTH, D), f32) * GMLP_WIDTH ** -0.5
    w_out = jax.random.normal(ks[15], (D, D), f32) * D ** -0.5
    return {"x": x, "c": c, "positions": positions, "norm_g": norm_g,
            "w_ada": w_ada, "b_ada": b_ada, "w_in": w_in,
            "q_norm_g": q_norm_g, "k_norm_g": k_norm_g,
            "sgu_ln_g": sgu_ln_g, "sgu_ln_b": sgu_ln_b,
            "w_spatial": w_spatial, "b_spatial": b_spatial,
            "w_branch_a": w_branch_a, "w_branch_b": w_branch_b, "w_out": w_out}


def reference(x, c, positions, norm_g, w_ada, b_ada, w_in, q_norm_g, k_norm_g,
              sgu_ln_g, sgu_ln_b, w_spatial, b_spatial, w_branch_a, w_branch_b, w_out):
    B, S, D = x.shape
    causal_chunk = jnp.tril(jnp.ones((CHUNK, CHUNK), dtype=bool))
    for layer in range(DEPTH):
        ada = jax.nn.silu(c) @ w_ada + b_ada
        shift, scale, gate = jnp.split(ada, 3, axis=-1)
        h = rms_norm(x, norm_g) * (1.0 + scale[:, None, :]) + shift[:, None, :]

        z = h @ w_in
        qkv = z[..., :QKV_COLS].reshape(B, S, N_ATTN_GROUPS, 3, HEADS_PER_GROUP, HEAD_DIM)

        outs, lses = [], []
        for g, (window, dilation) in enumerate(ATTN_GROUPS):
            q = partial_rope(rms_norm(qkv[:, :, g, 0], q_norm_g[g]), positions)
            k = partial_rope(rms_norm(qkv[:, :, g, 1], k_norm_g[g]), positions)
            v = qkv[:, :, g, 2]
            o, lse = dilated_window_attention(q, k, v, window, dilation)
            outs.append(o)
            lses.append(lse)
        o_all = jnp.stack(outs, axis=0)
        w_grp = jax.nn.softmax(jnp.stack(lses, axis=0), axis=0)
        attn = jnp.sum(w_grp[..., None] * o_all, axis=0).reshape(B, S, ATTN_WIDTH).astype(x.dtype)
        y_a = attn * jax.nn.silu(z[..., OFF_GATE_A:OFF_Z_B])

        uv = jax.nn.gelu(z[..., OFF_Z_B:OFF_GATE_B], approximate=False)
        u, v = jnp.split(uv, 2, axis=-1)
        v = layer_norm(v, sgu_ln_g, sgu_ln_b)
        n_chunks = S // CHUNK
        vc = v.reshape(B, n_chunks, CHUNK, GMLP_GROUPS, GMLP_GROUP_DIM)
        w_s = jnp.where(causal_chunk[None], w_spatial, 0.0)
        sv = jnp.einsum('gts,bnsgc->bntgc', w_s, vc) + b_spatial.T[None, None, :, :, None]
        y_b = u * sv.reshape(B, S, GMLP_WIDTH)
        y_b = y_b * jax.nn.silu(z[..., OFF_GATE_B:OFF_MERGE_A])

        merged = (jax.nn.sigmoid(z[..., OFF_MERGE_A:OFF_MERGE_B]) * (y_a @ w_branch_a)
                  + jax.nn.sigmoid(z[..., OFF_MERGE_B:IN_COLS]) * (y_b @ w_branch_b))
        out = merged @ w_out
        x = x + gate[:, None, :] * out
    return x
```

```python
import functools
import math

import jax
import jax.numpy as jnp
import numpy as np
from jax import lax
from jax.experimental import pallas as pl
from jax.experimental.pallas import tpu as pltpu

D_MODEL = 1024
HEAD_DIM = 64
HEADS = 8
ATTN_GROUPS = ((128, 1), (512, 4), (2048, 16))
ATTN_WIDTH = HEADS * HEAD_DIM
Q_BLOCK = 128
ROPE_THETA = 500000.0
ROPE_DIMS = HEAD_DIM // 4
ROPE_HALF = ROPE_DIMS // 2
GMLP_WIDTH = 512
GMLP_GROUPS = 8
GMLP_GROUP_DIM = GMLP_WIDTH // GMLP_GROUPS
CHUNK = 128
EPS = 1e-6
GROUP_COLS = 3 * ATTN_WIDTH
QKV_COLS = len(ATTN_GROUPS) * GROUP_COLS
REST_COLS = ATTN_WIDTH + 3 * GMLP_WIDTH + 2 * D_MODEL

LANES = 128
QKV_TILE = 512
MERGE_TILE = 256
VMEM_LIMIT_BYTES = 56 * 1024 * 1024

F32 = jnp.float32
BF16 = jnp.bfloat16


def _dot(a, b):
    return jnp.dot(a, b, preferred_element_type=F32)


def _silu(v):
    return v * (1.0 / (1.0 + jnp.exp(-v)))


def _sigmoid(v):
    return 1.0 / (1.0 + jnp.exp(-v))


def _gelu_exact(v):
    return 0.5 * v * (1.0 + lax.erf(v * (1.0 / math.sqrt(2.0))))


def _ada_ln(xf, ada_ref, norm_g_ref):
    ms = jnp.mean(xf * xf, axis=-1, keepdims=True)
    shift = ada_ref[0, :, 0:D_MODEL]
    scale = ada_ref[0, :, D_MODEL:2 * D_MODEL]
    return xf * lax.rsqrt(ms + EPS) * norm_g_ref[...] * (1.0 + scale) + shift


def _ada_kernel(c_ref, w_ref, b_ref, o_ref):
    o_ref[...] = _dot(_silu(c_ref[...]).astype(BF16), w_ref[...]) + b_ref[...]


def _ada_call(c, w_ada_bf16, b_ada):
    batch = c.shape[0]
    return pl.pallas_call(
        _ada_kernel,
        out_shape=jax.ShapeDtypeStruct((batch, 3 * D_MODEL), F32),
        compiler_params=pltpu.CompilerParams(vmem_limit_bytes=VMEM_LIMIT_BYTES),
        name="ada",
    )(c, w_ada_bf16, b_ada.reshape(1, 3 * D_MODEL))


def _rope_expand_matrix():
    e = np.zeros((64, 3 * LANES), np.float32)
    for lane in range(LANES):
        dim = lane % HEAD_DIM
        if dim < ROPE_DIMS:
            j = dim % ROPE_HALF
            for part in range(3):
                e[part * 8 + j, lane] = 1.0
                if dim >= ROPE_HALF:
                    e[24 + part * 8 + j, LANES + lane] = 1.0
                else:
                    e[24 + part * 8 + j, 2 * LANES + lane] = -1.0
        else:
            e[48, lane] = 1.0
    return e


def _head_mean_matrix():
    m = np.zeros((LANES, LANES), np.float32)
    for i in range(LANES):
        for j in range(LANES):
            if i // HEAD_DIM == j // HEAD_DIM:
                m[i, j] = 1.0 / HEAD_DIM
    return m


def _split3(a):
    hi = a.astype(BF16).astype(F32)
    r = a - hi
    mid = r.astype(BF16).astype(F32)
    return hi, mid, r - mid


def _qkv_kernel(x_ref, pos_ref, ada_ref, norm_g_ref, w_ref, gq_ref, gk_ref, freq_ref, e_ref, bd_ref,
                q_out, k_out, v_out, *, n_res):
    xf = jnp.concatenate([x_ref[0, :, r * D_MODEL:(r + 1) * D_MODEL] for r in range(n_res)], axis=0)
    tm = xf.shape[0]
    h = _ada_ln(xf, ada_ref, norm_g_ref)
    z = _dot(h.astype(BF16), w_ref[...])

    ang = freq_ref[...] * pos_ref[0].astype(F32)
    parts = _split3(jnp.cos(ang)) + _split3(jnp.sin(ang))
    lhs_t = jnp.concatenate(list(parts) + [jnp.ones((8, tm), F32), jnp.zeros((8, tm), F32)], axis=0)
    tab = lax.dot_general(lhs_t.astype(BF16), e_ref[...], (((0,), (0,)), ((), ())),
                          preferred_element_type=F32)
    cos_t, s1_t, s2_t = tab[:, 0:LANES], tab[:, LANES:2 * LANES], tab[:, 2 * LANES:3 * LANES]

    for which, g_ref, out in ((0, gq_ref, q_out), (1, gk_ref, k_out)):
        for cb in range(ATTN_WIDTH // LANES):
            a = z[:, which * ATTN_WIDTH + cb * LANES: which * ATTN_WIDTH + (cb + 1) * LANES]
            ms = _dot((a * a).astype(BF16), bd_ref[...])
            an = a * lax.rsqrt(ms + EPS) * g_ref[...]
            rot = (an * cos_t + pltpu.roll(an, ROPE_HALF, 1) * s1_t
                   + pltpu.roll(an, LANES - ROPE_HALF, 1) * s2_t)
            if which == 0:
                rot = rot * (1.0 / math.sqrt(HEAD_DIM))
            out[0, :, cb * LANES:(cb + 1) * LANES] = rot.astype(BF16)
    v_out[0] = z[:, 2 * ATTN_WIDTH:3 * ATTN_WIDTH].astype(BF16)


def _qkv_call(g, dilation, x, pos_perm, ada3, norm_g2, w_in_bf16, gq, gk, freq, e_mat, bd_mat):
    batch, seq, _ = x.shape
    res_len = seq // dilation
    rows = min(QKV_TILE, res_len)
    n_res = QKV_TILE // rows
    n_mi = res_len // rows
    x_view = x.reshape(batch, res_len, dilation * D_MODEL)
    const = lambda b, j: (0, 0)
    out_sds = jax.ShapeDtypeStruct((batch, seq, ATTN_WIDTH), BF16)
    out_spec = pl.BlockSpec((1, QKV_TILE, ATTN_WIDTH), lambda b, j: (b, j, 0))
    return pl.pallas_call(
        functools.partial(_qkv_kernel, n_res=n_res),
        grid=(batch, seq // QKV_TILE),
        in_specs=[
            pl.BlockSpec((1, rows, n_res * D_MODEL), lambda b, j: (b, j % n_mi, j // n_mi)),
            pl.BlockSpec((1, 1, QKV_TILE), lambda b, j: (b, 0, j)),
            pl.BlockSpec((1, 1, 3 * D_MODEL), lambda b, j: (b, 0, 0)),
            pl.BlockSpec((1, D_MODEL), const),
            pl.BlockSpec((D_MODEL, GROUP_COLS), lambda b, j: (0, g)),
            pl.BlockSpec((1, LANES), const),
            pl.BlockSpec((1, LANES), const),
            pl.BlockSpec((8, 1), const),
            pl.BlockSpec((64, 3 * LANES), const),
            pl.BlockSpec((LANES, LANES), const),
        ],
        out_specs=[out_spec, out_spec, out_spec],
        out_shape=[out_sds, out_sds, out_sds],
        compiler_params=pltpu.CompilerParams(
            dimension_semantics=("arbitrary", "arbitrary"), vmem_limit_bytes=VMEM_LIMIT_BYTES),
        name=f"qkv_g{g}",
    )(x_view, pos_perm, ada3, norm_g2, w_in_bf16, gq, gk, freq, e_mat, bd_mat)


def _attn_kernel(q_ref, k_ref, v_ref, o_ref, lse_ref, *, n_blk):
    row = lax.broadcasted_iota(jnp.int32, (Q_BLOCK, LANES), 0)
    col = lax.broadcasted_iota(jnp.int32, (Q_BLOCK, LANES), 1)
    cur_ok = col <= row
    prev_ok = col >= row
    band_ok = jnp.concatenate([prev_ok, cur_ok], axis=1)
    lane_lo = col < HEAD_DIM

    def block(r0, first):
        for hp in range(ATTN_WIDTH // LANES):
            lanes = slice(hp * LANES, (hp + 1) * LANES)
            q_pair = q_ref[0, 0, pl.ds(r0, Q_BLOCK), lanes]
            if first:
                k_win = k_ref[0, 0, pl.ds(r0, Q_BLOCK), lanes]
                v_win = v_ref[0, 0, pl.ds(r0, Q_BLOCK), lanes]
                ok = cur_ok
            else:
                k_win = k_ref[0, 0, pl.ds(r0 - Q_BLOCK, 2 * Q_BLOCK), lanes]
                v_win = v_ref[0, 0, pl.ds(r0 - Q_BLOCK, 2 * Q_BLOCK), lanes]
                ok = band_ok
            outs, lses = [], []
            for head in range(2):
                keep = lane_lo if head == 0 else jnp.logical_not(lane_lo)
                q_h = jnp.where(keep, q_pair, jnp.zeros_like(q_pair))
                s = lax.dot_general(q_h, k_win, (((1,), (1,)), ((), ())), preferred_element_type=F32)
                s = jnp.where(ok, s, -jnp.inf)
                m = jnp.max(s, axis=-1, keepdims=True)
                p = jnp.exp(s - m)
                den = jnp.sum(p, axis=-1, keepdims=True)
                o = _dot(p.astype(BF16), v_win)
                outs.append(o * (1.0 / den))
                lses.append(jnp.broadcast_to(m + jnp.log(den), (Q_BLOCK, LANES)))
            o_ref[0, pl.ds(r0, Q_BLOCK), lanes] = jnp.where(lane_lo, outs[0], outs[1]).astype(o_ref.dtype)
            lse_ref[0, pl.ds(r0, Q_BLOCK), lanes] = jnp.where(lane_lo, lses[0], lses[1])

    block(0, True)
    if n_blk > 1:
        def body(blk, carry):
            block(pl.multiple_of(blk * Q_BLOCK, Q_BLOCK), False)
            return carry
        lax.fori_loop(1, n_blk, body, 0)


def _attn_call(g, dilation, q, k, v):
    batch, seq, _ = q.shape
    res_len = seq // dilation
    view = lambda a: a.reshape(batch, dilation, res_len, ATTN_WIDTH)
    in_spec = pl.BlockSpec((1, 1, res_len, ATTN_WIDTH), lambda b, r: (b, r, 0, 0))
    out_spec = pl.BlockSpec((1, res_len, ATTN_WIDTH), lambda b, r: (b, 0, r))
    o, lse = pl.pallas_call(
        functools.partial(_attn_kernel, n_blk=res_len // Q_BLOCK),
        grid=(batch, dilation),
        in_specs=[in_spec, in_spec, in_spec],
        out_specs=[out_spec, out_spec],
        out_shape=[jax.ShapeDtypeStruct((batch, res_len, dilation * ATTN_WIDTH), BF16),
                   jax.ShapeDtypeStruct((batch, res_len, dilation * ATTN_WIDTH), F32)],
        compiler_params=pltpu.CompilerParams(
            dimension_semantics=("arbitrary", "arbitrary"), vmem_limit_bytes=VMEM_LIMIT_BYTES),
        name=f"attn_g{g}",
    )(view(q), view(k), view(v))
    return o.reshape(batch, seq, ATTN_WIDTH), lse.reshape(batch, seq, ATTN_WIDTH)


def _merge_kernel(x_ref, ada_ref, norm_g_ref, w_ref,
                  o0_ref, o1_ref, o2_ref, l0_ref, l1_ref, l2_ref,
                  ln_g_ref, ln_b_ref, ws_ref, bias_ref, wa_ref, wb_ref, wo_ref, out_ref):
    xf = x_ref[0]
    tm = xf.shape[0]
    h = _ada_ln(xf, ada_ref, norm_g_ref)
    z = _dot(h.astype(BF16), w_ref[...])
    off = 0
    gate_a = z[:, off:off + ATTN_WIDTH]; off += ATTN_WIDTH
    u = _gelu_exact(z[:, off:off + GMLP_WIDTH]); off += GMLP_WIDTH
    v = _gelu_exact(z[:, off:off + GMLP_WIDTH]); off += GMLP_WIDTH
    gate_b = z[:, off:off + GMLP_WIDTH]; off += GMLP_WIDTH
    merge_a = z[:, off:off + D_MODEL]; off += D_MODEL
    merge_b = z[:, off:off + D_MODEL]

    l0, l1, l2 = l0_ref[0], l1_ref[0], l2_ref[0]
    lmax = jnp.maximum(jnp.maximum(l0, l1), l2)
    e0, e1, e2 = jnp.exp(l0 - lmax), jnp.exp(l1 - lmax), jnp.exp(l2 - lmax)
    attn = (e0 * o0_ref[0].astype(F32) + e1 * o1_ref[0].astype(F32) + e2 * o2_ref[0].astype(F32)) \
        / (e0 + e1 + e2)
    y_a = attn * _silu(gate_a)

    mu = jnp.mean(v, axis=-1, keepdims=True)
    vc = v - mu
    var = jnp.mean(vc * vc, axis=-1, keepdims=True)
    v_ln = (vc * lax.rsqrt(var + EPS) * ln_g_ref[...] + ln_b_ref[...]).astype(BF16)
    row = lax.broadcasted_iota(jnp.int32, (CHUNK, CHUNK), 0)
    col = lax.broadcasted_iota(jnp.int32, (CHUNK, CHUNK), 1)
    causal = row >= col
    lane_lo = col < GMLP_GROUP_DIM
    n_chunks = tm // CHUNK
    sv_cols = []
    for gp in range(GMLP_WIDTH // LANES):
        lanes = slice(gp * LANES, (gp + 1) * LANES)
        rhs = jnp.concatenate([v_ln[c * CHUNK:(c + 1) * CHUNK, lanes] for c in range(n_chunks)], axis=1)
        res = []
        for half in range(2):
            w_s = jnp.where(causal, ws_ref[2 * gp + half], 0.0).astype(BF16)
            res.append(_dot(w_s, rhs))
        sel = jnp.concatenate(
            [jnp.where(lane_lo, res[0][:, c * LANES:(c + 1) * LANES], res[1][:, c * LANES:(c + 1) * LANES])
             + bias_ref[:, lanes] for c in range(n_chunks)], axis=0)
        sv_cols.append(sel)
    sv = jnp.concatenate(sv_cols, axis=1)
    y_b = u * sv * _silu(gate_b)

    merged = (_sigmoid(merge_a) * _dot(y_a.astype(BF16), wa_ref[...])
              + _sigmoid(merge_b) * _dot(y_b.astype(BF16), wb_ref[...]))
    out = _dot(merged.astype(BF16), wo_ref[...])
    gate = ada_ref[0, :, 2 * D_MODEL:3 * D_MODEL]
    out_ref[0] = xf + gate * out


def _merge_call(x, ada3, norm_g2, w_in_bf16, os_, lses, ln_g, ln_b, w_spatial, bias_tab, wa, wb, wo):
    batch, seq, _ = x.shape
    const2 = lambda b, i: (0, 0)
    tile = lambda width: pl.BlockSpec((1, MERGE_TILE, width), lambda b, i: (b, i, 0))
    return pl.pallas_call(
        _merge_kernel,
        grid=(batch, seq // MERGE_TILE),
        in_specs=[
            tile(D_MODEL),
            pl.BlockSpec((1, 1, 3 * D_MODEL), lambda b, i: (b, 0, 0)),
            pl.BlockSpec((1, D_MODEL), const2),
            pl.BlockSpec((D_MODEL, REST_COLS), const2),
            tile(ATTN_WIDTH), tile(ATTN_WIDTH), tile(ATTN_WIDTH),
            tile(ATTN_WIDTH), tile(ATTN_WIDTH), tile(ATTN_WIDTH),
            pl.BlockSpec((1, GMLP_WIDTH), const2),
            pl.BlockSpec((1, GMLP_WIDTH), const2),
            pl.BlockSpec((GMLP_GROUPS, CHUNK, CHUNK), lambda b, i: (0, 0, 0)),
            pl.BlockSpec((CHUNK, GMLP_WIDTH), const2),
            pl.BlockSpec((ATTN_WIDTH, D_MODEL), const2),
            pl.BlockSpec((GMLP_WIDTH, D_MODEL), const2),
            pl.BlockSpec((D_MODEL, D_MODEL), const2),
        ],
        out_specs=tile(D_MODEL),
        out_shape=jax.ShapeDtypeStruct(x.shape, x.dtype),
        compiler_params=pltpu.CompilerParams(
            dimension_semantics=("arbitrary", "arbitrary"), vmem_limit_bytes=VMEM_LIMIT_BYTES),
        name="merge",
    )(x, ada3, norm_g2, w_in_bf16, *os_, *lses, ln_g, ln_b, w_spatial, bias_tab, wa, wb, wo)


def kernel(x, c, positions, norm_g, w_ada, b_ada, w_in, q_norm_g, k_norm_g, sgu_ln_g, sgu_ln_b,
           w_spatial, b_spatial, w_branch_a, w_branch_b, w_out):
    batch, seq, d_model = x.shape
    assert d_model == D_MODEL and seq % QKV_TILE == 0 and seq % MERGE_TILE == 0
    assert w_in.shape == (D_MODEL, QKV_COLS + REST_COLS)

    w_qkv_bf16 = w_in[:, :QKV_COLS].astype(BF16)
    w_rest_bf16 = w_in[:, QKV_COLS:].astype(BF16)
    ada = _ada_call(c, w_ada.astype(BF16), b_ada)
    ada3 = ada.reshape(batch, 1, 3 * D_MODEL)
    norm_g2 = norm_g.reshape(1, D_MODEL)

    freq = (ROPE_THETA ** (-np.arange(0, ROPE_DIMS, 2, dtype=np.float32) / ROPE_DIMS)).astype(np.float32)
    freq = jnp.asarray(freq.reshape(ROPE_HALF, 1))
    e_mat = jnp.asarray(_rope_expand_matrix(), dtype=BF16)
    bd_mat = jnp.asarray(_head_mean_matrix(), dtype=BF16)

    outs, lses = [], []
    for g, (window, dilation) in enumerate(ATTN_GROUPS):
        assert window // dilation == Q_BLOCK
        res_len = seq // dilation
        pos_perm = positions.reshape(batch, res_len, dilation).transpose(0, 2, 1).reshape(batch, 1, seq)
        gq = jnp.tile(q_norm_g[g], LANES // HEAD_DIM).reshape(1, LANES)
        gk = jnp.tile(k_norm_g[g], LANES // HEAD_DIM).reshape(1, LANES)
        q, k, v = _qkv_call(g, dilation, x, pos_perm, ada3, norm_g2, w_qkv_bf16, gq, gk, freq, e_mat, bd_mat)
        o, lse = _attn_call(g, dilation, q, k, v)
        outs.append(o)
        lses.append(lse)

    bias_tab = jnp.repeat(b_spatial.T, GMLP_GROUP_DIM, axis=1)
    return _merge_call(x, ada3, norm_g2, w_rest_bf16, outs, lses,
                       sgu_ln_g.reshape(1, GMLP_WIDTH), sgu_ln_b.reshape(1, GMLP_WIDTH),
                       w_spatial, bias_tab,
                       w_branch_a.astype(BF16), w_branch_b.astype(BF16), w_out.astype(BF16))
```

```python
import functools
import math

import jax
import jax.numpy as jnp
import numpy as np
from jax import lax
from jax.experimental import pallas as pl
from jax.experimental.pallas import tpu as pltpu

D_MODEL = 1024
HEAD_DIM = 64
HEADS = 8
ATTN_GROUPS = ((128, 1), (512, 4), (2048, 16))
ATTN_WIDTH = HEADS * HEAD_DIM
Q_BLOCK = 128
ROPE_THETA = 500000.0
ROPE_DIMS = HEAD_DIM // 4
ROPE_HALF = ROPE_DIMS // 2
GMLP_WIDTH = 512
GMLP_GROUPS = 8
GMLP_GROUP_DIM = GMLP_WIDTH // GMLP_GROUPS
CHUNK = 128
EPS = 1e-6
GROUP_COLS = 3 * ATTN_WIDTH
QKV_COLS = len(ATTN_GROUPS) * GROUP_COLS
REST_COLS = ATTN_WIDTH + 3 * GMLP_WIDTH + 2 * D_MODEL

Q_SCALE = math.log2(math.e) / math.sqrt(HEAD_DIM)

LANES = 128
QKV_TILE = 512
MERGE_TILE = 256
VMEM_LIMIT_BYTES = 56 * 1024 * 1024

F32 = jnp.float32
BF16 = jnp.bfloat16


def _dot(a, b):
    return jnp.dot(a, b, preferred_element_type=F32)


def _silu(v):
    return v * (1.0 / (1.0 + jnp.exp(-v)))


def _sigmoid(v):
    return 1.0 / (1.0 + jnp.exp(-v))


def _gelu_exact(v):
    return 0.5 * v * (1.0 + lax.erf(v * (1.0 / math.sqrt(2.0))))


def _ada_ln(xf, ada_ref, norm_g_ref):
    ms = jnp.mean(xf * xf, axis=-1, keepdims=True)
    shift = ada_ref[0, :, 0:D_MODEL]
    scale = ada_ref[0, :, D_MODEL:2 * D_MODEL]
    return xf * lax.rsqrt(ms + EPS) * norm_g_ref[...] * (1.0 + scale) + shift


def _ada_kernel(c_ref, w_ref, b_ref, o_ref):
    o_ref[...] = _dot(_silu(c_ref[...]).astype(BF16), w_ref[...]) + b_ref[...]


def _ada_call(c, w_ada_bf16, b_ada):
    batch = c.shape[0]
    return pl.pallas_call(
        _ada_kernel,
        out_shape=jax.ShapeDtypeStruct((batch, 3 * D_MODEL), F32),
        compiler_params=pltpu.CompilerParams(vmem_limit_bytes=VMEM_LIMIT_BYTES),
        name="ada",
    )(c, w_ada_bf16, b_ada.reshape(1, 3 * D_MODEL))


def _rope_expand_matrix():
    e = np.zeros((64, 3 * LANES), np.float32)
    for lane in range(LANES):
        dim = lane % HEAD_DIM
        if dim < ROPE_DIMS:
            j = dim % ROPE_HALF
            for part in range(3):
                e[part * 8 + j, lane] = 1.0
                if dim >= ROPE_HALF:
                    e[24 + part * 8 + j, LANES + lane] = 1.0
                else:
                    e[24 + part * 8 + j, 2 * LANES + lane] = -1.0
        else:
            e[48, lane] = 1.0
    return e


def _head_mean_matrix():
    m = np.zeros((LANES, LANES), np.float32)
    for i in range(LANES):
        for j in range(LANES):
            if i // HEAD_DIM == j // HEAD_DIM:
                m[i, j] = 1.0 / HEAD_DIM
    return m


def _split3(a):
    hi = a.astype(BF16).astype(F32)
    r = a - hi
    mid = r.astype(BF16).astype(F32)
    return hi, mid, r - mid


def _store_residue_major(out, val, dilation, perm_scr):
    tm, width = val.shape
    if dilation == 1:
        out[0, 0] = val.astype(out.dtype)
        return
    for cb in range(width // LANES):
        perm_scr[cb] = val[:, cb * LANES:(cb + 1) * LANES]
    for r in range(dilation):
        for cb in range(width // LANES):
            out[0, r, :, cb * LANES:(cb + 1) * LANES] = (
                perm_scr[cb, pl.ds(r, tm // dilation, stride=dilation), :].astype(out.dtype))


def _qkv_kernel(x_ref, pos_ref, ada_ref, norm_g_ref, w_ref, gq_ref, gk_ref, freq_ref, e_ref, bd_ref,
                *refs):
    out_refs, perm_scr = refs[:-1], refs[-1]
    xf = x_ref[0]
    tm = xf.shape[0]
    hb = _ada_ln(xf, ada_ref, norm_g_ref).astype(BF16)

    ang = freq_ref[...] * pos_ref[0].astype(F32)
    parts = _split3(jnp.cos(ang)) + _split3(jnp.sin(ang))
    lhs_t = jnp.concatenate(list(parts) + [jnp.ones((8, tm), F32), jnp.zeros((8, tm), F32)], axis=0)
    tab = lax.dot_general(lhs_t.astype(BF16), e_ref[...], (((0,), (0,)), ((), ())),
                          preferred_element_type=F32)
    cos_t, s1_t, s2_t = tab[:, 0:LANES], tab[:, LANES:2 * LANES], tab[:, 2 * LANES:3 * LANES]

    for g, (_, dilation) in enumerate(ATTN_GROUPS):
        z = _dot(hb, w_ref[:, g * GROUP_COLS:(g + 1) * GROUP_COLS])
        q_out, k_out, v_out = out_refs[3 * g:3 * g + 3]
        for which, g_ref, out in ((0, gq_ref, q_out), (1, gk_ref, k_out)):
            cols = []
            for cb in range(ATTN_WIDTH // LANES):
                a = z[:, which * ATTN_WIDTH + cb * LANES: which * ATTN_WIDTH + (cb + 1) * LANES]
                ms = _dot((a * a).astype(BF16), bd_ref[...])
                an = a * lax.rsqrt(ms + EPS) * g_ref[g:g + 1, :]
                rot = (an * cos_t + pltpu.roll(an, ROPE_HALF, 1) * s1_t
                       + pltpu.roll(an, LANES - ROPE_HALF, 1) * s2_t)
                if which == 0:
                    rot = rot * Q_SCALE
                cols.append(rot)
            _store_residue_major(out, jnp.concatenate(cols, axis=1), dilation, perm_scr)
        _store_residue_major(v_out, z[:, 2 * ATTN_WIDTH:3 * ATTN_WIDTH], dilation, perm_scr)


def _qkv_call(x, pos3, ada3, norm_g2, w_qkv_bf16, gq, gk, freq, e_mat, bd_mat):
    batch, seq, _ = x.shape
    const = lambda b, j: (0, 0)
    out_specs, out_shapes = [], []
    for _, dilation in ATTN_GROUPS:
        spec = pl.BlockSpec((1, dilation, QKV_TILE // dilation, ATTN_WIDTH), lambda b, j: (b, 0, j, 0))
        sds = jax.ShapeDtypeStruct((batch, dilation, seq // dilation, ATTN_WIDTH), BF16)
        out_specs += [spec] * 3
        out_shapes += [sds] * 3
    n_groups = len(ATTN_GROUPS)
    return pl.pallas_call(
        _qkv_kernel,
        grid=(batch, seq // QKV_TILE),
        in_specs=[
            pl.BlockSpec((1, QKV_TILE, D_MODEL), lambda b, j: (b, j, 0)),
            pl.BlockSpec((1, 1, QKV_TILE), lambda b, j: (b, 0, j)),
            pl.BlockSpec((1, 1, 3 * D_MODEL), lambda b, j: (b, 0, 0)),
            pl.BlockSpec((1, D_MODEL), const),
            pl.BlockSpec((D_MODEL, QKV_COLS), const),
            pl.BlockSpec((n_groups, LANES), const),
            pl.BlockSpec((n_groups, LANES), const),
            pl.BlockSpec((8, 1), const),
            pl.BlockSpec((64, 3 * LANES), const),
            pl.BlockSpec((LANES, LANES), const),
        ],
        out_specs=out_specs,
        out_shape=out_shapes,
        scratch_shapes=[pltpu.VMEM((ATTN_WIDTH // LANES, QKV_TILE, LANES), F32)],
        compiler_params=pltpu.CompilerParams(
            dimension_semantics=("arbitrary", "arbitrary"), vmem_limit_bytes=VMEM_LIMIT_BYTES),
        name="qkv",
    )(x, pos3, ada3, norm_g2, w_qkv_bf16, gq, gk, freq, e_mat, bd_mat)


def _attn_kernel(q_ref, k_ref, v_ref, o_ref, lse_ref, *scratch, dilation, n_blk):
    row = lax.broadcasted_iota(jnp.int32, (Q_BLOCK, LANES), 0)
    col = lax.broadcasted_iota(jnp.int32, (Q_BLOCK, LANES), 1)
    cur_ok = col <= row
    prev_ok = col >= row
    first_ok = jnp.concatenate([cur_ok, cur_ok], axis=0)
    band_one = jnp.concatenate([prev_ok, cur_ok], axis=1)
    band_ok = jnp.concatenate([band_one, band_one], axis=0)
    lane_lo = col < HEAD_DIM

    def block(r, r0, first):
        for hp in range(ATTN_WIDTH // LANES):
            lanes = slice(hp * LANES, (hp + 1) * LANES)
            q_pair = q_ref[0, r, pl.ds(r0, Q_BLOCK), lanes]
            zero = jnp.zeros_like(q_pair)
            q2 = jnp.concatenate([jnp.where(lane_lo, q_pair, zero), jnp.where(lane_lo, zero, q_pair)], axis=0)
            if first:
                k_win = k_ref[0, r, pl.ds(r0, Q_BLOCK), lanes]
                v_win = v_ref[0, r, pl.ds(r0, Q_BLOCK), lanes]
                ok = first_ok
            else:
                k_win = k_ref[0, r, pl.ds(r0 - Q_BLOCK, 2 * Q_BLOCK), lanes]
                v_win = v_ref[0, r, pl.ds(r0 - Q_BLOCK, 2 * Q_BLOCK), lanes]
                ok = band_ok
            s = lax.dot_general(q2, k_win, (((1,), (1,)), ((), ())), preferred_element_type=F32)
            s = jnp.where(ok, s, -jnp.inf)
            m = jnp.max(s, axis=-1, keepdims=True)
            p = jnp.exp2(s - m).astype(BF16)
            v_ext = jnp.concatenate([v_win, jnp.ones_like(v_win)], axis=1)
            o2 = _dot(p, v_ext)
            den = o2[:, LANES:]
            on = o2[:, :LANES] * (1.0 / den)
            lse2 = m * math.log(2.0) + jnp.log(den)
            o_pair = jnp.where(lane_lo, on[:Q_BLOCK], on[Q_BLOCK:])
            lse_pair = jnp.where(lane_lo, lse2[:Q_BLOCK], lse2[Q_BLOCK:])
            if dilation == 1:
                o_ref[0, pl.ds(r0, Q_BLOCK), lanes] = o_pair.astype(o_ref.dtype)
                lse_ref[0, pl.ds(r0, Q_BLOCK), lanes] = lse_pair
            else:
                rows = pl.ds(r + r0 * dilation, Q_BLOCK, stride=dilation)
                scratch[0][hp, rows, :] = o_pair
                scratch[1][hp, rows, :] = lse_pair

    def residue(r):
        block(r, 0, True)
        if n_blk > 1:
            def body(blk, carry):
                block(r, pl.multiple_of(blk * Q_BLOCK, Q_BLOCK), False)
                return carry
            lax.fori_loop(1, n_blk, body, 0)

    if dilation == 1:
        residue(0)
    else:
        def res_body(r, carry):
            residue(r)
            return carry
        lax.fori_loop(0, dilation, res_body, 0)
        for hp in range(ATTN_WIDTH // LANES):
            lanes = slice(hp * LANES, (hp + 1) * LANES)
            o_ref[0, :, lanes] = scratch[0][hp].astype(o_ref.dtype)
            lse_ref[0, :, lanes] = scratch[1][hp]


def _attn_call(g, dilation, q, k, v):
    batch, _, res_len, _ = q.shape
    seq = dilation * res_len
    in_spec = pl.BlockSpec((1, dilation, res_len, ATTN_WIDTH), lambda b: (b, 0, 0, 0))
    out_spec = pl.BlockSpec((1, seq, ATTN_WIDTH), lambda b: (b, 0, 0))
    plane = pltpu.VMEM((ATTN_WIDTH // LANES, seq, LANES), F32)
    scratch = [] if dilation == 1 else [plane, plane]
    return pl.pallas_call(
        functools.partial(_attn_kernel, dilation=dilation, n_blk=res_len // Q_BLOCK),
        grid=(batch,),
        in_specs=[in_spec, in_spec, in_spec],
        out_specs=[out_spec, out_spec],
        out_shape=[jax.ShapeDtypeStruct((batch, seq, ATTN_WIDTH), BF16),
                   jax.ShapeDtypeStruct((batch, seq, ATTN_WIDTH), F32)],
        scratch_shapes=scratch,
        compiler_params=pltpu.CompilerParams(
            dimension_semantics=("arbitrary",), vmem_limit_bytes=VMEM_LIMIT_BYTES),
        name=f"attn_g{g}",
    )(q, k, v)


def _merge_kernel(x_ref, ada_ref, norm_g_ref, w_ref,
                  o0_ref, o1_ref, o2_ref, l0_ref, l1_ref, l2_ref,
                  ln_g_ref, ln_b_ref, ws_ref, bias_ref, wa_ref, wb_ref, wo_ref, out_ref):
    xf = x_ref[0]
    tm = xf.shape[0]
    h = _ada_ln(xf, ada_ref, norm_g_ref)
    z = _dot(h.astype(BF16), w_ref[...])
    off = 0
    gate_a = z[:, off:off + ATTN_WIDTH]; off += ATTN_WIDTH
    u = _gelu_exact(z[:, off:off + GMLP_WIDTH]); off += GMLP_WIDTH
    v = _gelu_exact(z[:, off:off + GMLP_WIDTH]); off += GMLP_WIDTH
    gate_b = z[:, off:off + GMLP_WIDTH]; off += GMLP_WIDTH
    merge_a = z[:, off:off + D_MODEL]; off += D_MODEL
    merge_b = z[:, off:off + D_MODEL]

    l0, l1, l2 = l0_ref[0], l1_ref[0], l2_ref[0]
    lmax = jnp.maximum(jnp.maximum(l0, l1), l2)
    e0, e1, e2 = jnp.exp(l0 - lmax), jnp.exp(l1 - lmax), jnp.exp(l2 - lmax)
    attn = (e0 * o0_ref[0].astype(F32) + e1 * o1_ref[0].astype(F32) + e2 * o2_ref[0].astype(F32)) \
        / (e0 + e1 + e2)
    y_a = attn * _silu(gate_a)

    mu = jnp.mean(v, axis=-1, keepdims=True)
    vc = v - mu
    var = jnp.mean(vc * vc, axis=-1, keepdims=True)
    v_ln = (vc * lax.rsqrt(var + EPS) * ln_g_ref[...] + ln_b_ref[...]).astype(BF16)
    row = lax.broadcasted_iota(jnp.int32, (CHUNK, CHUNK), 0)
    col = lax.broadcasted_iota(jnp.int32, (CHUNK, CHUNK), 1)
    causal = row >= col
    lane_lo = col < GMLP_GROUP_DIM
    n_chunks = tm // CHUNK
    sv_cols = []
    for gp in range(GMLP_WIDTH // LANES):
        lanes = slice(gp * LANES, (gp + 1) * LANES)
        rhs = jnp.concatenate([v_ln[c * CHUNK:(c + 1) * CHUNK, lanes] for c in range(n_chunks)], axis=1)
        res = []
        for half in range(2):
            w_s = jnp.where(causal, ws_ref[2 * gp + half], 0.0).astype(BF16)
            res.append(_dot(w_s, rhs))
        sel = jnp.concatenate(
            [jnp.where(lane_lo, res[0][:, c * LANES:(c + 1) * LANES], res[1][:, c * LANES:(c + 1) * LANES])
             + bias_ref[:, lanes] for c in range(n_chunks)], axis=0)
        sv_cols.append(sel)
    sv = jnp.concatenate(sv_cols, axis=1)
    y_b = u * sv * _silu(gate_b)

    merged = (_sigmoid(merge_a) * _dot(y_a.astype(BF16), wa_ref[...])
              + _sigmoid(merge_b) * _dot(y_b.astype(BF16), wb_ref[...]))
    out = _dot(merged.astype(BF16), wo_ref[...])
    gate = ada_ref[0, :, 2 * D_MODEL:3 * D_MODEL]
    out_ref[0] = xf + gate * out


def _merge_call(x, ada3, norm_g2, w_in_bf16, os_, lses, ln_g, ln_b, w_spatial, bias_tab, wa, wb, wo):
    batch, seq, _ = x.shape
    const2 = lambda b, i: (0, 0)
    tile = lambda width: pl.BlockSpec((1, MERGE_TILE, width), lambda b, i: (b, i, 0))
    return pl.pallas_call(
        _merge_kernel,
        grid=(batch, seq // MERGE_TILE),
        in_specs=[
            tile(D_MODEL),
            pl.BlockSpec((1, 1, 3 * D_MODEL), lambda b, i: (b, 0, 0)),
            pl.BlockSpec((1, D_MODEL), const2),
            pl.BlockSpec((D_MODEL, REST_COLS), const2),
            tile(ATTN_WIDTH), tile(ATTN_WIDTH), tile(ATTN_WIDTH),
            tile(ATTN_WIDTH), tile(ATTN_WIDTH), tile(ATTN_WIDTH),
            pl.BlockSpec((1, GMLP_WIDTH), const2),
            pl.BlockSpec((1, GMLP_WIDTH), const2),
            pl.BlockSpec((GMLP_GROUPS, CHUNK, CHUNK), lambda b, i: (0, 0, 0)),
            pl.BlockSpec((CHUNK, GMLP_WIDTH), const2),
            pl.BlockSpec((ATTN_WIDTH, D_MODEL), const2),
            pl.BlockSpec((GMLP_WIDTH, D_MODEL), const2),
            pl.BlockSpec((D_MODEL, D_MODEL), const2),
        ],
        out_specs=tile(D_MODEL),
        out_shape=jax.ShapeDtypeStruct(x.shape, x.dtype),
        compiler_params=pltpu.CompilerParams(
            dimension_semantics=("arbitrary", "arbitrary"), vmem_limit_bytes=VMEM_LIMIT_BYTES),
        name="merge",
    )(x, ada3, norm_g2, w_in_bf16, *os_, *lses, ln_g, ln_b, w_spatial, bias_tab, wa, wb, wo)


def kernel(x, c, positions, norm_g, w_ada, b_ada, w_in, q_norm_g, k_norm_g, sgu_ln_g, sgu_ln_b,
           w_spatial, b_spatial, w_branch_a, w_branch_b, w_out):
    batch, seq, d_model = x.shape
    assert d_model == D_MODEL and seq % QKV_TILE == 0 and seq % MERGE_TILE == 0
    assert w_in.shape == (D_MODEL, QKV_COLS + REST_COLS)

    w_qkv_bf16 = w_in[:, :QKV_COLS].astype(BF16)
    w_rest_bf16 = w_in[:, QKV_COLS:].astype(BF16)
    ada = _ada_call(c, w_ada.astype(BF16), b_ada)
    ada3 = ada.reshape(batch, 1, 3 * D_MODEL)
    norm_g2 = norm_g.reshape(1, D_MODEL)

    freq = (ROPE_THETA ** (-np.arange(0, ROPE_DIMS, 2, dtype=np.float32) / ROPE_DIMS)).astype(np.float32)
    freq = jnp.asarray(freq.reshape(ROPE_HALF, 1))
    e_mat = jnp.asarray(_rope_expand_matrix(), dtype=BF16)
    bd_mat = jnp.asarray(_head_mean_matrix(), dtype=BF16)

    gq = jnp.tile(q_norm_g, (1, LANES // HEAD_DIM))
    gk = jnp.tile(k_norm_g, (1, LANES // HEAD_DIM))
    qkv = _qkv_call(x, positions.reshape(batch, 1, seq), ada3, norm_g2, w_qkv_bf16, gq, gk,
                    freq, e_mat, bd_mat)
    outs, lses = [], []
    for g, (window, dilation) in enumerate(ATTN_GROUPS):
        assert window // dilation == Q_BLOCK
        o, lse = _attn_call(g, dilation, *qkv[3 * g:3 * g + 3])
        outs.append(o)
        lses.append(lse)

    bias_tab = jnp.repeat(b_spatial.T, GMLP_GROUP_DIM, axis=1)
    return _merge_call(x, ada3, norm_g2, w_rest_bf16, outs, lses,
                       sgu_ln_g.reshape(1, GMLP_WIDTH), sgu_ln_b.reshape(1, GMLP_WIDTH),
                       w_spatial, bias_tab,
                       w_branch_a.astype(BF16), w_branch_b.astype(BF16), w_out.astype(BF16))
```

```python
import functools
import math

import jax
import jax.numpy as jnp
import numpy as np
from jax import lax
from jax.experimental import pallas as pl
from jax.experimental.pallas import tpu as pltpu

D_MODEL = 1024
HEAD_DIM = 64
HEADS = 8
ATTN_GROUPS = ((128, 1), (512, 4), (2048, 16))
ATTN_WIDTH = HEADS * HEAD_DIM
Q_BLOCK = 128
ROPE_THETA = 500000.0
ROPE_DIMS = HEAD_DIM // 4
ROPE_HALF = ROPE_DIMS // 2
GMLP_WIDTH = 512
GMLP_GROUPS = 8
GMLP_GROUP_DIM = GMLP_WIDTH // GMLP_GROUPS
CHUNK = 128
EPS = 1e-6
GROUP_COLS = 3 * ATTN_WIDTH
QKV_COLS = len(ATTN_GROUPS) * GROUP_COLS
REST_COLS = ATTN_WIDTH + 3 * GMLP_WIDTH + 2 * D_MODEL

Q_SCALE = math.log2(math.e) / math.sqrt(HEAD_DIM)

LANES = 128
QKV_TILE = 512
MERGE_TILE = 256
VMEM_LIMIT_BYTES = 56 * 1024 * 1024

F32 = jnp.float32
BF16 = jnp.bfloat16


def _dot(a, b):
    return jnp.dot(a, b, preferred_element_type=F32)


def _silu(v):
    return v * (1.0 / (1.0 + jnp.exp(-v)))


def _sigmoid(v):
    return 1.0 / (1.0 + jnp.exp(-v))


def _gelu_exact(v):
    return 0.5 * v * (1.0 + lax.erf(v * (1.0 / math.sqrt(2.0))))


def _ada_ln(xf, ada_ref, norm_g_ref):
    ms = jnp.mean(xf * xf, axis=-1, keepdims=True)
    shift = ada_ref[0, :, 0:D_MODEL]
    scale = ada_ref[0, :, D_MODEL:2 * D_MODEL]
    return xf * lax.rsqrt(ms + EPS) * norm_g_ref[...] * (1.0 + scale) + shift


def _ada_kernel(c_ref, w_ref, b_ref, o_ref):
    o_ref[...] = _dot(_silu(c_ref[...]).astype(BF16), w_ref[...]) + b_ref[...]


def _ada_call(c, w_ada_bf16, b_ada):
    batch = c.shape[0]
    return pl.pallas_call(
        _ada_kernel,
        out_shape=jax.ShapeDtypeStruct((batch, 3 * D_MODEL), F32),
        compiler_params=pltpu.CompilerParams(vmem_limit_bytes=VMEM_LIMIT_BYTES),
        name="ada",
    )(c, w_ada_bf16, b_ada.reshape(1, 3 * D_MODEL))


def _rope_expand_matrix():
    e = np.zeros((64, 3 * LANES), np.float32)
    for lane in range(LANES):
        dim = lane % HEAD_DIM
        if dim < ROPE_DIMS:
            j = dim % ROPE_HALF
            for part in range(3):
                e[part * 8 + j, lane] = 1.0
                if dim >= ROPE_HALF:
                    e[24 + part * 8 + j, LANES + lane] = 1.0
                else:
                    e[24 + part * 8 + j, 2 * LANES + lane] = -1.0
        else:
            e[48, lane] = 1.0
    return e


def _head_mean_matrix():
    m = np.zeros((LANES, LANES), np.float32)
    for i in range(LANES):
        for j in range(LANES):
            if i // HEAD_DIM == j // HEAD_DIM:
                m[i, j] = 1.0 / HEAD_DIM
    return m


def _split3(a):
    hi = a.astype(BF16).astype(F32)
    r = a - hi
    mid = r.astype(BF16).astype(F32)
    return hi, mid, r - mid


def _store_residue_major(out, val, dilation, perm_scr):
    tm, width = val.shape
    if dilation == 1:
        out[0, 0] = val.astype(out.dtype)
        return
    for cb in range(width // LANES):
        perm_scr[cb] = val[:, cb * LANES:(cb + 1) * LANES]
    for r in range(dilation):
        for cb in range(width // LANES):
            out[0, r, :, cb * LANES:(cb + 1) * LANES] = (
                perm_scr[cb, pl.ds(r, tm // dilation, stride=dilation), :].astype(out.dtype))


def _qkv_kernel(x_ref, pos_ref, ada_ref, norm_g_ref, w_ref, gq_ref, gk_ref, freq_ref, e_ref, bd_ref,
                *refs):
    out_refs, perm_scr = refs[:-1], refs[-1]
    xf = x_ref[0]
    tm = xf.shape[0]
    hb = _ada_ln(xf, ada_ref, norm_g_ref).astype(BF16)

    ang = freq_ref[...] * pos_ref[0].astype(F32)
    parts = _split3(jnp.cos(ang)) + _split3(jnp.sin(ang))
    lhs_t = jnp.concatenate(list(parts) + [jnp.ones((8, tm), F32), jnp.zeros((8, tm), F32)], axis=0)
    tab = lax.dot_general(lhs_t.astype(BF16), e_ref[...], (((0,), (0,)), ((), ())),
                          preferred_element_type=F32)
    cos_t, s1_t, s2_t = tab[:, 0:LANES], tab[:, LANES:2 * LANES], tab[:, 2 * LANES:3 * LANES]

    for g, (_, dilation) in enumerate(ATTN_GROUPS):
        z = _dot(hb, w_ref[:, g * GROUP_COLS:(g + 1) * GROUP_COLS])
        q_out, k_out, v_out = out_refs[3 * g:3 * g + 3]
        for which, g_ref, out in ((0, gq_ref, q_out), (1, gk_ref, k_out)):
            cols = []
            for cb in range(ATTN_WIDTH // LANES):
                a = z[:, which * ATTN_WIDTH + cb * LANES: which * ATTN_WIDTH + (cb + 1) * LANES]
                ms = _dot((a * a).astype(BF16), bd_ref[...])
                an = a * lax.rsqrt(ms + EPS) * g_ref[g:g + 1, :]
                rot = (an * cos_t + pltpu.roll(an, ROPE_HALF, 1) * s1_t
                       + pltpu.roll(an, LANES - ROPE_HALF, 1) * s2_t)
                if which == 0:
                    rot = rot * Q_SCALE
                cols.append(rot)
            _store_residue_major(out, jnp.concatenate(cols, axis=1), dilation, perm_scr)
        _store_residue_major(v_out, z[:, 2 * ATTN_WIDTH:3 * ATTN_WIDTH], dilation, perm_scr)


def _qkv_call(x, pos3, ada3, norm_g2, w_qkv_bf16, gq, gk, freq, e_mat, bd_mat):
    batch, seq, _ = x.shape
    const = lambda b, j: (0, 0)
    out_specs, out_shapes = [], []
    for _, dilation in ATTN_GROUPS:
        spec = pl.BlockSpec((1, dilation, QKV_TILE // dilation, ATTN_WIDTH), lambda b, j: (b, 0, j, 0))
        sds = jax.ShapeDtypeStruct((batch, dilation, seq // dilation, ATTN_WIDTH), BF16)
        out_specs += [spec] * 3
        out_shapes += [sds] * 3
    n_groups = len(ATTN_GROUPS)
    return pl.pallas_call(
        _qkv_kernel,
        grid=(batch, seq // QKV_TILE),
        in_specs=[
            pl.BlockSpec((1, QKV_TILE, D_MODEL), lambda b, j: (b, j, 0)),
            pl.BlockSpec((1, 1, QKV_TILE), lambda b, j: (b, 0, j)),
            pl.BlockSpec((1, 1, 3 * D_MODEL), lambda b, j: (b, 0, 0)),
            pl.BlockSpec((1, D_MODEL), const),
            pl.BlockSpec((D_MODEL, QKV_COLS), const),
            pl.BlockSpec((n_groups, LANES), const),
            pl.BlockSpec((n_groups, LANES), const),
            pl.BlockSpec((8, 1), const),
            pl.BlockSpec((64, 3 * LANES), const),
            pl.BlockSpec((LANES, LANES), const),
        ],
        out_specs=out_specs,
        out_shape=out_shapes,
        scratch_shapes=[pltpu.VMEM((ATTN_WIDTH // LANES, QKV_TILE, LANES), F32)],
        compiler_params=pltpu.CompilerParams(
            dimension_semantics=("arbitrary", "arbitrary"), vmem_limit_bytes=VMEM_LIMIT_BYTES),
        name="qkv",
    )(x, pos3, ada3, norm_g2, w_qkv_bf16, gq, gk, freq, e_mat, bd_mat)


def _attn_kernel(q_ref, k_ref, v_ref, o_ref, lse_ref, *scratch, dilation, n_blk):
    row = lax.broadcasted_iota(jnp.int32, (Q_BLOCK, LANES), 0)
    col = lax.broadcasted_iota(jnp.int32, (Q_BLOCK, LANES), 1)
    cur_ok = col <= row
    prev_ok = col >= row
    first_ok = jnp.concatenate([cur_ok, cur_ok], axis=0)
    band_one = jnp.concatenate([prev_ok, cur_ok], axis=1)
    band_ok = jnp.concatenate([band_one, band_one], axis=0)
    lane_lo = col < HEAD_DIM
    p_scr, m_scr = scratch[0], scratch[1]
    n_pairs = ATTN_WIDTH // LANES

    def scores(r, r0, first):
        for hp in range(n_pairs):
            lanes = slice(hp * LANES, (hp + 1) * LANES)
            q_pair = q_ref[0, r, pl.ds(r0, Q_BLOCK), lanes]
            zero = jnp.zeros_like(q_pair)
            q2 = jnp.concatenate([jnp.where(lane_lo, q_pair, zero), jnp.where(lane_lo, zero, q_pair)], axis=0)
            if first:
                k_win, ok = k_ref[0, r, pl.ds(r0, Q_BLOCK), lanes], first_ok
            else:
                k_win, ok = k_ref[0, r, pl.ds(r0 - Q_BLOCK, 2 * Q_BLOCK), lanes], band_ok
            s = lax.dot_general(q2, k_win, (((1,), (1,)), ((), ())), preferred_element_type=F32)
            s = jnp.where(ok, s, -jnp.inf)
            m = jnp.max(s, axis=-1, keepdims=True)
            p_scr[hp, :, 0:s.shape[1]] = jnp.exp2(s - m).astype(BF16)
            m_scr[hp] = jnp.where(lane_lo, m[:Q_BLOCK], m[Q_BLOCK:]) * math.log(2.0)

    def values(r, r0, first):
        for hp in range(n_pairs):
            lanes = slice(hp * LANES, (hp + 1) * LANES)
            if first:
                p, v_win = p_scr[hp, :, 0:Q_BLOCK], v_ref[0, r, pl.ds(r0, Q_BLOCK), lanes]
            else:
                p, v_win = p_scr[hp], v_ref[0, r, pl.ds(r0 - Q_BLOCK, 2 * Q_BLOCK), lanes]
            v_ext = jnp.concatenate([v_win, jnp.ones_like(v_win)], axis=1)
            o2 = _dot(p, v_ext)
            num = jnp.where(lane_lo, o2[:Q_BLOCK, :LANES], o2[Q_BLOCK:, :LANES])
            den = jnp.where(lane_lo, o2[:Q_BLOCK, LANES:], o2[Q_BLOCK:, LANES:])
            o_pair = num * (1.0 / den)
            lse_pair = m_scr[hp] + jnp.log(den)
            if dilation == 1:
                o_ref[0, pl.ds(r0, Q_BLOCK), lanes] = o_pair.astype(o_ref.dtype)
                lse_ref[0, pl.ds(r0, Q_BLOCK), lanes] = lse_pair
            else:
                rows = pl.ds(r + r0 * dilation, Q_BLOCK, stride=dilation)
                scratch[2][hp, rows, :] = o_pair
                scratch[3][hp, rows, :] = lse_pair

    def at(blk):
        return pl.multiple_of(blk * Q_BLOCK, Q_BLOCK)

    scores(0, 0, True)
    if n_blk == 1:
        def res_body(r, carry):
            values(r - 1, 0, True)
            scores(r, 0, True)
            return carry
        lax.fori_loop(1, dilation, res_body, 0)
        values(dilation - 1, 0, True)
    else:
        for r in range(dilation):
            values(r, 0, True)
            scores(r, Q_BLOCK, False)

            def blk_body(blk, carry, r=r):
                values(r, at(blk - 1), False)
                scores(r, at(blk), False)
                return carry
            lax.fori_loop(2, n_blk, blk_body, 0)
            values(r, (n_blk - 1) * Q_BLOCK, False)
            if r + 1 < dilation:
                scores(r + 1, 0, True)

    if dilation > 1:
        for hp in range(n_pairs):
            lanes = slice(hp * LANES, (hp + 1) * LANES)
            o_ref[0, :, lanes] = scratch[2][hp].astype(o_ref.dtype)
            lse_ref[0, :, lanes] = scratch[3][hp]


def _attn_call(g, dilation, q, k, v):
    batch, _, res_len, _ = q.shape
    seq = dilation * res_len
    in_spec = pl.BlockSpec((1, dilation, res_len, ATTN_WIDTH), lambda b: (b, 0, 0, 0))
    out_spec = pl.BlockSpec((1, seq, ATTN_WIDTH), lambda b: (b, 0, 0))
    plane = pltpu.VMEM((ATTN_WIDTH // LANES, seq, LANES), F32)
    n_pairs = ATTN_WIDTH // LANES
    scratch = [pltpu.VMEM((n_pairs, 2 * Q_BLOCK, 2 * Q_BLOCK), BF16),
               pltpu.VMEM((n_pairs, Q_BLOCK, LANES), F32)]
    if dilation > 1:
        scratch += [plane, plane]
    return pl.pallas_call(
        functools.partial(_attn_kernel, dilation=dilation, n_blk=res_len // Q_BLOCK),
        grid=(batch,),
        in_specs=[in_spec, in_spec, in_spec],
        out_specs=[out_spec, out_spec],
        out_shape=[jax.ShapeDtypeStruct((batch, seq, ATTN_WIDTH), BF16),
                   jax.ShapeDtypeStruct((batch, seq, ATTN_WIDTH), F32)],
        scratch_shapes=scratch,
        compiler_params=pltpu.CompilerParams(
            dimension_semantics=("arbitrary",), vmem_limit_bytes=VMEM_LIMIT_BYTES),
        name=f"attn_g{g}",
    )(q, k, v)


def _merge_kernel(x_ref, ada_ref, norm_g_ref, w_ref,
                  o0_ref, o1_ref, o2_ref, l0_ref, l1_ref, l2_ref,
                  ln_g_ref, ln_b_ref, ws_ref, bias_ref, wa_ref, wb_ref, wo_ref, out_ref):
    xf = x_ref[0]
    tm = xf.shape[0]
    h = _ada_ln(xf, ada_ref, norm_g_ref)
    z = _dot(h.astype(BF16), w_ref[...])
    off = 0
    gate_a = z[:, off:off + ATTN_WIDTH]; off += ATTN_WIDTH
    u = _gelu_exact(z[:, off:off + GMLP_WIDTH]); off += GMLP_WIDTH
    v = _gelu_exact(z[:, off:off + GMLP_WIDTH]); off += GMLP_WIDTH
    gate_b = z[:, off:off + GMLP_WIDTH]; off += GMLP_WIDTH
    merge_a = z[:, off:off + D_MODEL]; off += D_MODEL
    merge_b = z[:, off:off + D_MODEL]

    l0, l1, l2 = l0_ref[0], l1_ref[0], l2_ref[0]
    lmax = jnp.maximum(jnp.maximum(l0, l1), l2)
    e0, e1, e2 = jnp.exp(l0 - lmax), jnp.exp(l1 - lmax), jnp.exp(l2 - lmax)
    attn = (e0 * o0_ref[0].astype(F32) + e1 * o1_ref[0].astype(F32) + e2 * o2_ref[0].astype(F32)) \
        / (e0 + e1 + e2)
    y_a = attn * _silu(gate_a)

    mu = jnp.mean(v, axis=-1, keepdims=True)
    vc = v - mu
    var = jnp.mean(vc * vc, axis=-1, keepdims=True)
    v_ln = (vc * lax.rsqrt(var + EPS) * ln_g_ref[...] + ln_b_ref[...]).astype(BF16)
    row = lax.broadcasted_iota(jnp.int32, (CHUNK, CHUNK), 0)
    col = lax.broadcasted_iota(jnp.int32, (CHUNK, CHUNK), 1)
    causal = row >= col
    lane_lo = col < GMLP_GROUP_DIM
    n_chunks = tm // CHUNK
    sv_cols = []
    for gp in range(GMLP_WIDTH // LANES):
        lanes = slice(gp * LANES, (gp + 1) * LANES)
        rhs = jnp.concatenate([v_ln[c * CHUNK:(c + 1) * CHUNK, lanes] for c in range(n_chunks)], axis=1)
        res = []
        for half in range(2):
            w_s = jnp.where(causal, ws_ref[2 * gp + half], 0.0).astype(BF16)
            res.append(_dot(w_s, rhs))
        sel = jnp.concatenate(
            [jnp.where(lane_lo, res[0][:, c * LANES:(c + 1) * LANES], res[1][:, c * LANES:(c + 1) * LANES])
             + bias_ref[:, lanes] for c in range(n_chunks)], axis=0)
        sv_cols.append(sel)
    sv = jnp.concatenate(sv_cols, axis=1)
    y_b = u * sv * _silu(gate_b)

    merged = (_sigmoid(merge_a) * _dot(y_a.astype(BF16), wa_ref[...])
              + _sigmoid(merge_b) * _dot(y_b.astype(BF16), wb_ref[...]))
    out = _dot(merged.astype(BF16), wo_ref[...])
    gate = ada_ref[0, :, 2 * D_MODEL:3 * D_MODEL]
    out_ref[0] = xf + gate * out


def _merge_call(x, ada3, norm_g2, w_in_bf16, os_, lses, ln_g, ln_b, w_spatial, bias_tab, wa, wb, wo):
    batch, seq, _ = x.shape
    const2 = lambda b, i: (0, 0)
    tile = lambda width: pl.BlockSpec((1, MERGE_TILE, width), lambda b, i: (b, i, 0))
    return pl.pallas_call(
        _merge_kernel,
        grid=(batch, seq // MERGE_TILE),
        in_specs=[
            tile(D_MODEL),
            pl.BlockSpec((1, 1, 3 * D_MODEL), lambda b, i: (b, 0, 0)),
            pl.BlockSpec((1, D_MODEL), const2),
            pl.BlockSpec((D_MODEL, REST_COLS), const2),
            tile(ATTN_WIDTH), tile(ATTN_WIDTH), tile(ATTN_WIDTH),
            tile(ATTN_WIDTH), tile(ATTN_WIDTH), tile(ATTN_WIDTH),
            pl.BlockSpec((1, GMLP_WIDTH), const2),
            pl.BlockSpec((1, GMLP_WIDTH), const2),
            pl.BlockSpec((GMLP_GROUPS, CHUNK, CHUNK), lambda b, i: (0, 0, 0)),
            pl.BlockSpec((CHUNK, GMLP_WIDTH), const2),
            pl.BlockSpec((ATTN_WIDTH, D_MODEL), const2),
            pl.BlockSpec((GMLP_WIDTH, D_MODEL), const2),
            pl.BlockSpec((D_MODEL, D_MODEL), const2),
        ],
        out_specs=tile(D_MODEL),
        out_shape=jax.ShapeDtypeStruct(x.shape, x.dtype),
        compiler_params=pltpu.CompilerParams(
            dimension_semantics=("arbitrary", "arbitrary"), vmem_limit_bytes=VMEM_LIMIT_BYTES),
        name="merge",
    )(x, ada3, norm_g2, w_in_bf16, *os_, *lses, ln_g, ln_b, w_spatial, bias_tab, wa, wb, wo)


def kernel(x, c, positions, norm_g, w_ada, b_ada, w_in, q_norm_g, k_norm_g, sgu_ln_g, sgu_ln_b,
           w_spatial, b_spatial, w_branch_a, w_branch_b, w_out):
    batch, seq, d_model = x.shape
    assert d_model == D_MODEL and seq % QKV_TILE == 0 and seq % MERGE_TILE == 0
    assert w_in.shape == (D_MODEL, QKV_COLS + REST_COLS)

    w_qkv_bf16 = w_in[:, :QKV_COLS].astype(BF16)
    w_rest_bf16 = w_in[:, QKV_COLS:].astype(BF16)
    ada = _ada_call(c, w_ada.astype(BF16), b_ada)
    ada3 = ada.reshape(batch, 1, 3 * D_MODEL)
    norm_g2 = norm_g.reshape(1, D_MODEL)

    freq = (ROPE_THETA ** (-np.arange(0, ROPE_DIMS, 2, dtype=np.float32) / ROPE_DIMS)).astype(np.float32)
    freq = jnp.asarray(freq.reshape(ROPE_HALF, 1))
    e_mat = jnp.asarray(_rope_expand_matrix(), dtype=BF16)
    bd_mat = jnp.asarray(_head_mean_matrix(), dtype=BF16)

    gq = jnp.tile(q_norm_g, (1, LANES // HEAD_DIM))
    gk = jnp.tile(k_norm_g, (1, LANES // HEAD_DIM))
    qkv = _qkv_call(x, positions.reshape(batch, 1, seq), ada3, norm_g2, w_qkv_bf16, gq, gk,
                    freq, e_mat, bd_mat)
    outs, lses = [], []
    for g, (window, dilation) in enumerate(ATTN_GROUPS):
        assert window // dilation == Q_BLOCK
        o, lse = _attn_call(g, dilation, *qkv[3 * g:3 * g + 3])
        outs.append(o)
        lses.append(lse)

    bias_tab = jnp.repeat(b_spatial.T, GMLP_GROUP_DIM, axis=1)
    return _merge_call(x, ada3, norm_g2, w_rest_bf16, outs, lses,
                       sgu_ln_g.reshape(1, GMLP_WIDTH), sgu_ln_b.reshape(1, GMLP_WIDTH),
                       w_spatial, bias_tab,
                       w_branch_a.astype(BF16), w_branch_b.astype(BF16), w_out.astype(BF16))
```

```python
import functools
import math

import jax
import jax.numpy as jnp
import numpy as np
from jax import lax
from jax.experimental import pallas as pl
from jax.experimental.pallas import tpu as pltpu

D_MODEL = 1024
HEAD_DIM = 64
HEADS = 8
ATTN_GROUPS = ((128, 1), (512, 4), (2048, 16))
ATTN_WIDTH = HEADS * HEAD_DIM
Q_BLOCK = 128
ROPE_THETA = 500000.0
ROPE_DIMS = HEAD_DIM // 4
ROPE_HALF = ROPE_DIMS // 2
GMLP_WIDTH = 512
GMLP_GROUPS = 8
GMLP_GROUP_DIM = GMLP_WIDTH // GMLP_GROUPS
CHUNK = 128
EPS = 1e-6
GROUP_COLS = 3 * ATTN_WIDTH
QKV_COLS = len(ATTN_GROUPS) * GROUP_COLS
REST_COLS = ATTN_WIDTH + 3 * GMLP_WIDTH + 2 * D_MODEL

Q_SCALE = math.log2(math.e) / math.sqrt(HEAD_DIM)

LANES = 128
QKV_TILE = 512
MERGE_TILE = 256
VMEM_LIMIT_BYTES = 56 * 1024 * 1024

F32 = jnp.float32
BF16 = jnp.bfloat16


def _dot(a, b):
    return jnp.dot(a, b, preferred_element_type=F32)


def _silu(v):
    return v * (1.0 / (1.0 + jnp.exp(-v)))


def _sigmoid(v):
    return 1.0 / (1.0 + jnp.exp(-v))


def _gelu_exact(v):
    return 0.5 * v * (1.0 + lax.erf(v * (1.0 / math.sqrt(2.0))))


def _ada_ln(xf, ada_ref, norm_g_ref):
    ms = jnp.mean(xf * xf, axis=-1, keepdims=True)
    shift = ada_ref[0, :, 0:D_MODEL]
    scale = ada_ref[0, :, D_MODEL:2 * D_MODEL]
    return xf * lax.rsqrt(ms + EPS) * norm_g_ref[...] * (1.0 + scale) + shift


def _ada_kernel(c_ref, w_ref, b_ref, o_ref):
    o_ref[...] = _dot(_silu(c_ref[...]).astype(BF16), w_ref[...]) + b_ref[...]


def _ada_call(c, w_ada_bf16, b_ada):
    batch = c.shape[0]
    return pl.pallas_call(
        _ada_kernel,
        out_shape=jax.ShapeDtypeStruct((batch, 3 * D_MODEL), F32),
        compiler_params=pltpu.CompilerParams(vmem_limit_bytes=VMEM_LIMIT_BYTES),
        name="ada",
    )(c, w_ada_bf16, b_ada.reshape(1, 3 * D_MODEL))


def _rope_expand_matrix():
    e = np.zeros((64, 3 * LANES), np.float32)
    for lane in range(LANES):
        dim = lane % HEAD_DIM
        if dim < ROPE_DIMS:
            j = dim % ROPE_HALF
            for part in range(3):
                e[part * 8 + j, lane] = 1.0
                if dim >= ROPE_HALF:
                    e[24 + part * 8 + j, LANES + lane] = 1.0
                else:
                    e[24 + part * 8 + j, 2 * LANES + lane] = -1.0
        else:
            e[48, lane] = 1.0
    return e


def _head_mean_matrix():
    head = np.arange(2 * LANES) // HEAD_DIM
    return (head[:, None] == head[None, :]).astype(np.float32) / HEAD_DIM


def _split3(a):
    hi = a.astype(BF16).astype(F32)
    r = a - hi
    mid = r.astype(BF16).astype(F32)
    return hi, mid, r - mid


def _store_residue_major(out, val, dilation, perm_scr):
    tm, width = val.shape
    if dilation == 1:
        out[0, 0] = val.astype(out.dtype)
        return
    for cb in range(width // LANES):
        perm_scr[cb] = val[:, cb * LANES:(cb + 1) * LANES]
    for r in range(dilation):
        for cb in range(width // LANES):
            out[0, r, :, cb * LANES:(cb + 1) * LANES] = (
                perm_scr[cb, pl.ds(r, tm // dilation, stride=dilation), :].astype(out.dtype))


def _qkv_kernel(x_ref, pos_ref, ada_ref, norm_g_ref, w_ref, gq_ref, gk_ref, freq_ref, e_ref, bd_ref,
                *refs):
    out_refs, perm_scr = refs[:-1], refs[-1]
    xf = x_ref[0]
    tm = xf.shape[0]
    hb = _ada_ln(xf, ada_ref, norm_g_ref).astype(BF16)

    ang = freq_ref[...] * pos_ref[0].astype(F32)
    parts = _split3(jnp.cos(ang)) + _split3(jnp.sin(ang))
    lhs_t = jnp.concatenate(list(parts) + [jnp.ones((8, tm), F32), jnp.zeros((8, tm), F32)], axis=0)
    tab = lax.dot_general(lhs_t.astype(BF16), e_ref[...], (((0,), (0,)), ((), ())),
                          preferred_element_type=F32)
    cos_t, s1_t, s2_t = tab[:, 0:LANES], tab[:, LANES:2 * LANES], tab[:, 2 * LANES:3 * LANES]

    n_groups = len(ATTN_GROUPS)
    chunks = ([(g, which) for which in (0, 1) for g in range(n_groups)]
              + [(g, 2) for g in reversed(range(n_groups))])
    for g, which in chunks:
        dilation = ATTN_GROUPS[g][1]
        col0 = g * GROUP_COLS + which * ATTN_WIDTH
        zc = _dot(hb, w_ref[:, col0:col0 + ATTN_WIDTH])
        out = out_refs[3 * g + which]
        if which == 2:
            _store_residue_major(out, zc, dilation, perm_scr)
            continue
        gain = gq_ref[g:g + 1, :] * Q_SCALE if which == 0 else gk_ref[g:g + 1, :]
        cols = []
        for half in range(ATTN_WIDTH // (2 * LANES)):
            a2 = zc[:, half * 2 * LANES:(half + 1) * 2 * LANES]
            ms = _dot((a2 * a2).astype(BF16), bd_ref[...])
            a2n = a2 * lax.rsqrt(ms + EPS)
            for sub in range(2):
                an = a2n[:, sub * LANES:(sub + 1) * LANES] * gain
                cols.append(an * cos_t + pltpu.roll(an, ROPE_HALF, 1) * s1_t
                            + pltpu.roll(an, LANES - ROPE_HALF, 1) * s2_t)
        _store_residue_major(out, jnp.concatenate(cols, axis=1), dilation, perm_scr)


def _qkv_call(x, pos3, ada3, norm_g2, w_qkv_bf16, gq, gk, freq, e_mat, bd_mat):
    batch, seq, _ = x.shape
    const = lambda b, j: (0, 0)
    out_specs, out_shapes = [], []
    for _, dilation in ATTN_GROUPS:
        spec = pl.BlockSpec((1, dilation, QKV_TILE // dilation, ATTN_WIDTH), lambda b, j: (b, 0, j, 0))
        sds = jax.ShapeDtypeStruct((batch, dilation, seq // dilation, ATTN_WIDTH), BF16)
        out_specs += [spec] * 3
        out_shapes += [sds] * 3
    n_groups = len(ATTN_GROUPS)
    return pl.pallas_call(
        _qkv_kernel,
        grid=(batch, seq // QKV_TILE),
        in_specs=[
            pl.BlockSpec((1, QKV_TILE, D_MODEL), lambda b, j: (b, j, 0)),
            pl.BlockSpec((1, 1, QKV_TILE), lambda b, j: (b, 0, j)),
            pl.BlockSpec((1, 1, 3 * D_MODEL), lambda b, j: (b, 0, 0)),
            pl.BlockSpec((1, D_MODEL), const),
            pl.BlockSpec((D_MODEL, QKV_COLS), const),
            pl.BlockSpec((n_groups, LANES), const),
            pl.BlockSpec((n_groups, LANES), const),
            pl.BlockSpec((8, 1), const),
            pl.BlockSpec((64, 3 * LANES), const),
            pl.BlockSpec((2 * LANES, 2 * LANES), const),
        ],
        out_specs=out_specs,
        out_shape=out_shapes,
        scratch_shapes=[pltpu.VMEM((ATTN_WIDTH // LANES, QKV_TILE, LANES), F32)],
        compiler_params=pltpu.CompilerParams(
            dimension_semantics=("arbitrary", "arbitrary"), vmem_limit_bytes=VMEM_LIMIT_BYTES),
        name="qkv",
    )(x, pos3, ada3, norm_g2, w_qkv_bf16, gq, gk, freq, e_mat, bd_mat)


def _attn_kernel(q_ref, k_ref, v_ref, o_ref, lse_ref, *scratch, dilation, n_blk):
    row = lax.broadcasted_iota(jnp.int32, (Q_BLOCK, LANES), 0)
    col = lax.broadcasted_iota(jnp.int32, (Q_BLOCK, LANES), 1)
    cur_ok = col <= row
    prev_ok = col >= row
    first_ok = jnp.concatenate([cur_ok, cur_ok], axis=0)
    band_one = jnp.concatenate([prev_ok, cur_ok], axis=1)
    band_ok = jnp.concatenate([band_one, band_one], axis=0)
    lane_lo = col < HEAD_DIM
    p_scr, m_scr = scratch[0], scratch[1]
    n_pairs = ATTN_WIDTH // LANES

    def scores(r, r0, first):
        for hp in range(n_pairs):
            lanes = slice(hp * LANES, (hp + 1) * LANES)
            q_pair = q_ref[0, r, pl.ds(r0, Q_BLOCK), lanes]
            zero = jnp.zeros_like(q_pair)
            q2 = jnp.concatenate([jnp.where(lane_lo, q_pair, zero), jnp.where(lane_lo, zero, q_pair)], axis=0)
            if first:
                k_win, ok = k_ref[0, r, pl.ds(r0, Q_BLOCK), lanes], first_ok
            else:
                k_win, ok = k_ref[0, r, pl.ds(r0 - Q_BLOCK, 2 * Q_BLOCK), lanes], band_ok
            s = lax.dot_general(q2, k_win, (((1,), (1,)), ((), ())), preferred_element_type=F32)
            s = jnp.where(ok, s, -jnp.inf)
            m = jnp.max(s, axis=-1, keepdims=True)
            p_scr[hp, :, 0:s.shape[1]] = jnp.exp2(s - m).astype(BF16)
            m_scr[hp] = jnp.where(lane_lo, m[:Q_BLOCK], m[Q_BLOCK:]) * math.log(2.0)

    def values(r, r0, first):
        for hp in range(n_pairs):
            lanes = slice(hp * LANES, (hp + 1) * LANES)
            if first:
                p, v_win = p_scr[hp, :, 0:Q_BLOCK], v_ref[0, r, pl.ds(r0, Q_BLOCK), lanes]
            else:
                p, v_win = p_scr[hp], v_ref[0, r, pl.ds(r0 - Q_BLOCK, 2 * Q_BLOCK), lanes]
            v_ext = jnp.concatenate([v_win, jnp.ones_like(v_win)], axis=1)
            o2 = _dot(p, v_ext)
            num = jnp.where(lane_lo, o2[:Q_BLOCK, :LANES], o2[Q_BLOCK:, :LANES])
            den = jnp.where(lane_lo, o2[:Q_BLOCK, LANES:], o2[Q_BLOCK:, LANES:])
            o_pair = num * (1.0 / den)
            lse_pair = m_scr[hp] + jnp.log(den)
            if dilation == 1:
                o_ref[0, pl.ds(r0, Q_BLOCK), lanes] = o_pair.astype(o_ref.dtype)
                lse_ref[0, pl.ds(r0, Q_BLOCK), lanes] = lse_pair
            else:
                rows = pl.ds(r + r0 * dilation, Q_BLOCK, stride=dilation)
                scratch[2][hp, rows, :] = o_pair
                scratch[3][hp, rows, :] = lse_pair

    def at(blk):
        return pl.multiple_of(blk * Q_BLOCK, Q_BLOCK)

    scores(0, 0, True)
    if n_blk == 1:
        def res_body(r, carry):
            values(r - 1, 0, True)
            scores(r, 0, True)
            return carry
        lax.fori_loop(1, dilation, res_body, 0)
        values(dilation - 1, 0, True)
    else:
        for r in range(dilation):
            values(r, 0, True)
            scores(r, Q_BLOCK, False)

            def blk_body(blk, carry, r=r):
                values(r, at(blk - 1), False)
                scores(r, at(blk), False)
                return carry
            lax.fori_loop(2, n_blk, blk_body, 0)
            values(r, (n_blk - 1) * Q_BLOCK, False)
            if r + 1 < dilation:
                scores(r + 1, 0, True)

    if dilation > 1:
        for hp in range(n_pairs):
            lanes = slice(hp * LANES, (hp + 1) * LANES)
            o_ref[0, :, lanes] = scratch[2][hp].astype(o_ref.dtype)
            lse_ref[0, :, lanes] = scratch[3][hp]


def _attn_call(g, dilation, q, k, v):
    batch, _, res_len, _ = q.shape
    seq = dilation * res_len
    in_spec = pl.BlockSpec((1, dilation, res_len, ATTN_WIDTH), lambda b: (b, 0, 0, 0))
    out_spec = pl.BlockSpec((1, seq, ATTN_WIDTH), lambda b: (b, 0, 0))
    plane = pltpu.VMEM((ATTN_WIDTH // LANES, seq, LANES), F32)
    n_pairs = ATTN_WIDTH // LANES
    scratch = [pltpu.VMEM((n_pairs, 2 * Q_BLOCK, 2 * Q_BLOCK), BF16),
               pltpu.VMEM((n_pairs, Q_BLOCK, LANES), F32)]
    if dilation > 1:
        scratch += [plane, plane]
    return pl.pallas_call(
        functools.partial(_attn_kernel, dilation=dilation, n_blk=res_len // Q_BLOCK),
        grid=(batch,),
        in_specs=[in_spec, in_spec, in_spec],
        out_specs=[out_spec, out_spec],
        out_shape=[jax.ShapeDtypeStruct((batch, seq, ATTN_WIDTH), BF16),
                   jax.ShapeDtypeStruct((batch, seq, ATTN_WIDTH), F32)],
        scratch_shapes=scratch,
        compiler_params=pltpu.CompilerParams(
            dimension_semantics=("arbitrary",), vmem_limit_bytes=VMEM_LIMIT_BYTES),
        name=f"attn_g{g}",
    )(q, k, v)


def _merge_kernel(x_ref, ada_ref, norm_g_ref, w_ref,
                  o0_ref, o1_ref, o2_ref, l0_ref, l1_ref, l2_ref,
                  ln_g_ref, ln_b_ref, ws_ref, bias_ref, wa_ref, wb_ref, wo_ref, out_ref):
    xf = x_ref[0]
    tm = xf.shape[0]
    h = _ada_ln(xf, ada_ref, norm_g_ref)
    z = _dot(h.astype(BF16), w_ref[...])
    off = 0
    gate_a = z[:, off:off + ATTN_WIDTH]; off += ATTN_WIDTH
    u = _gelu_exact(z[:, off:off + GMLP_WIDTH]); off += GMLP_WIDTH
    v = _gelu_exact(z[:, off:off + GMLP_WIDTH]); off += GMLP_WIDTH
    gate_b = z[:, off:off + GMLP_WIDTH]; off += GMLP_WIDTH
    merge_a = z[:, off:off + D_MODEL]; off += D_MODEL
    merge_b = z[:, off:off + D_MODEL]

    l0, l1, l2 = l0_ref[0], l1_ref[0], l2_ref[0]
    lmax = jnp.maximum(jnp.maximum(l0, l1), l2)
    e0, e1, e2 = jnp.exp(l0 - lmax), jnp.exp(l1 - lmax), jnp.exp(l2 - lmax)
    attn = (e0 * o0_ref[0].astype(F32) + e1 * o1_ref[0].astype(F32) + e2 * o2_ref[0].astype(F32)) \
        / (e0 + e1 + e2)
    y_a = attn * _silu(gate_a)

    mu = jnp.mean(v, axis=-1, keepdims=True)
    vc = v - mu
    var = jnp.mean(vc * vc, axis=-1, keepdims=True)
    v_ln = (vc * lax.rsqrt(var + EPS) * ln_g_ref[...] + ln_b_ref[...]).astype(BF16)
    row = lax.broadcasted_iota(jnp.int32, (CHUNK, CHUNK), 0)
    col = lax.broadcasted_iota(jnp.int32, (CHUNK, CHUNK), 1)
    causal = row >= col
    lane_lo = col < GMLP_GROUP_DIM
    n_chunks = tm // CHUNK
    sv_cols = []
    for gp in range(GMLP_WIDTH // LANES):
        lanes = slice(gp * LANES, (gp + 1) * LANES)
        rhs = jnp.concatenate([v_ln[c * CHUNK:(c + 1) * CHUNK, lanes] for c in range(n_chunks)], axis=1)
        res = []
        for half in range(2):
            w_s = jnp.where(causal, ws_ref[2 * gp + half], 0.0).astype(BF16)
            res.append(_dot(w_s, rhs))
        sel = jnp.concatenate(
            [jnp.where(lane_lo, res[0][:, c * LANES:(c + 1) * LANES], res[1][:, c * LANES:(c + 1) * LANES])
             + bias_ref[:, lanes] for c in range(n_chunks)], axis=0)
        sv_cols.append(sel)
    sv = jnp.concatenate(sv_cols, axis=1)
    y_b = u * sv * _silu(gate_b)

    merged = (_sigmoid(merge_a) * _dot(y_a.astype(BF16), wa_ref[...])
              + _sigmoid(merge_b) * _dot(y_b.astype(BF16), wb_ref[...]))
    out = _dot(merged.astype(BF16), wo_ref[...])
    gate = ada_ref[0, :, 2 * D_MODEL:3 * D_MODEL]
    out_ref[0] = xf + gate * out


def _merge_call(x, ada3, norm_g2, w_in_bf16, os_, lses, ln_g, ln_b, w_spatial, bias_tab, wa, wb, wo):
    batch, seq, _ = x.shape
    const2 = lambda b, i: (0, 0)
    tile = lambda width: pl.BlockSpec((1, MERGE_TILE, width), lambda b, i: (b, i, 0))
    return pl.pallas_call(
        _merge_kernel,
        grid=(batch, seq // MERGE_TILE),
        in_specs=[
            tile(D_MODEL),
            pl.BlockSpec((1, 1, 3 * D_MODEL), lambda b, i: (b, 0, 0)),
            pl.BlockSpec((1, D_MODEL), const2),
            pl.BlockSpec((D_MODEL, REST_COLS), const2),
            tile(ATTN_WIDTH), tile(ATTN_WIDTH), tile(ATTN_WIDTH),
            tile(ATTN_WIDTH), tile(ATTN_WIDTH), tile(ATTN_WIDTH),
            pl.BlockSpec((1, GMLP_WIDTH), const2),
            pl.BlockSpec((1, GMLP_WIDTH), const2),
            pl.BlockSpec((GMLP_GROUPS, CHUNK, CHUNK), lambda b, i: (0, 0, 0)),
            pl.BlockSpec((CHUNK, GMLP_WIDTH), const2),
            pl.BlockSpec((ATTN_WIDTH, D_MODEL), const2),
            pl.BlockSpec((GMLP_WIDTH, D_MODEL), const2),
            pl.BlockSpec((D_MODEL, D_MODEL), const2),
        ],
        out_specs=tile(D_MODEL),
        out_shape=jax.ShapeDtypeStruct(x.shape, x.dtype),
        compiler_params=pltpu.CompilerParams(
            dimension_semantics=("arbitrary", "arbitrary"), vmem_limit_bytes=VMEM_LIMIT_BYTES),
        name="merge",
    )(x, ada3, norm_g2, w_in_bf16, *os_, *lses, ln_g, ln_b, w_spatial, bias_tab, wa, wb, wo)


def kernel(x, c, positions, norm_g, w_ada, b_ada, w_in, q_norm_g, k_norm_g, sgu_ln_g, sgu_ln_b,
           w_spatial, b_spatial, w_branch_a, w_branch_b, w_out):
    batch, seq, d_model = x.shape
    assert d_model == D_MODEL and seq % QKV_TILE == 0 and seq % MERGE_TILE == 0
    assert w_in.shape == (D_MODEL, QKV_COLS + REST_COLS)

    w_qkv_bf16 = w_in[:, :QKV_COLS].astype(BF16)
    w_rest_bf16 = w_in[:, QKV_COLS:].astype(BF16)
    ada = _ada_call(c, w_ada.astype(BF16), b_ada)
    ada3 = ada.reshape(batch, 1, 3 * D_MODEL)
    norm_g2 = norm_g.reshape(1, D_MODEL)

    freq = (ROPE_THETA ** (-np.arange(0, ROPE_DIMS, 2, dtype=np.float32) / ROPE_DIMS)).astype(np.float32)
    freq = jnp.asarray(freq.reshape(ROPE_HALF, 1))
    e_mat = jnp.asarray(_rope_expand_matrix(), dtype=BF16)
    bd_mat = jnp.asarray(_head_mean_matrix(), dtype=BF16)

    gq = jnp.tile(q_norm_g, (1, LANES // HEAD_DIM))
    gk = jnp.tile(k_norm_g, (1, LANES // HEAD_DIM))
    qkv = _qkv_call(x, positions.reshape(batch, 1, seq), ada3, norm_g2, w_qkv_bf16, gq, gk,
                    freq, e_mat, bd_mat)
    outs, lses = [], []
    for g, (window, dilation) in enumerate(ATTN_GROUPS):
        assert window // dilation == Q_BLOCK
        o, lse = _attn_call(g, dilation, *qkv[3 * g:3 * g + 3])
        outs.append(o)
        lses.append(lse)

    bias_tab = jnp.repeat(b_spatial.T, GMLP_GROUP_DIM, axis=1)
    return _merge_call(x, ada3, norm_g2, w_rest_bf16, outs, lses,
                       sgu_ln_g.reshape(1, GMLP_WIDTH), sgu_ln_b.reshape(1, GMLP_WIDTH),
                       w_spatial, bias_tab,
                       w_branch_a.astype(BF16), w_branch_b.astype(BF16), w_out.astype(BF16))
```

```python
import functools
import math

import jax
import jax.numpy as jnp
import numpy as np
from jax import lax
from jax.experimental import pallas as pl
from jax.experimental.pallas import tpu as pltpu

D_MODEL = 1024
HEAD_DIM = 64
HEADS = 8
ATTN_GROUPS = ((128, 1), (512, 4), (2048, 16))
ATTN_WIDTH = HEADS * HEAD_DIM
Q_BLOCK = 128
ROPE_THETA = 500000.0
ROPE_DIMS = HEAD_DIM // 4
ROPE_HALF = ROPE_DIMS // 2
GMLP_WIDTH = 512
GMLP_GROUPS = 8
GMLP_GROUP_DIM = GMLP_WIDTH // GMLP_GROUPS
CHUNK = 128
EPS = 1e-6
GROUP_COLS = 3 * ATTN_WIDTH
QKV_COLS = len(ATTN_GROUPS) * GROUP_COLS
REST_COLS = ATTN_WIDTH + 3 * GMLP_WIDTH + 2 * D_MODEL

Q_SCALE = math.log2(math.e) / math.sqrt(HEAD_DIM)

LANES = 128
QKV_TILE = 512
MERGE_TILE = 256
BLOCK_UNROLL = 4
VMEM_LIMIT_BYTES = 56 * 1024 * 1024

F32 = jnp.float32
BF16 = jnp.bfloat16


def _dot(a, b):
    return jnp.dot(a, b, preferred_element_type=F32)


def _silu(v):
    return v * (1.0 / (1.0 + jnp.exp(-v)))


def _sigmoid(v):
    return 1.0 / (1.0 + jnp.exp(-v))


def _gelu_exact(v):
    return 0.5 * v * (1.0 + lax.erf(v * (1.0 / math.sqrt(2.0))))


def _ada_ln(xf, ada_ref, norm_g_ref):
    ms = jnp.mean(xf * xf, axis=-1, keepdims=True)
    shift = ada_ref[0, :, 0:D_MODEL]
    scale = ada_ref[0, :, D_MODEL:2 * D_MODEL]
    return xf * lax.rsqrt(ms + EPS) * norm_g_ref[...] * (1.0 + scale) + shift


def _ada_kernel(c_ref, w_ref, b_ref, o_ref):
    o_ref[...] = _dot(_silu(c_ref[...]).astype(BF16), w_ref[...]) + b_ref[...]


def _ada_call(c, w_ada_bf16, b_ada):
    batch = c.shape[0]
    return pl.pallas_call(
        _ada_kernel,
        out_shape=jax.ShapeDtypeStruct((batch, 3 * D_MODEL), F32),
        compiler_params=pltpu.CompilerParams(vmem_limit_bytes=VMEM_LIMIT_BYTES),
        name="ada",
    )(c, w_ada_bf16, b_ada.reshape(1, 3 * D_MODEL))


def _rope_expand_matrix():
    e = np.zeros((64, 3 * LANES), np.float32)
    for lane in range(LANES):
        dim = lane % HEAD_DIM
        if dim < ROPE_DIMS:
            j = dim % ROPE_HALF
            for part in range(3):
                e[part * 8 + j, lane] = 1.0
                if dim >= ROPE_HALF:
                    e[24 + part * 8 + j, LANES + lane] = 1.0
                else:
                    e[24 + part * 8 + j, 2 * LANES + lane] = -1.0
        else:
            e[48, lane] = 1.0
    return e


def _head_mean_matrix():
    head = np.arange(2 * LANES) // HEAD_DIM
    return (head[:, None] == head[None, :]).astype(np.float32) / HEAD_DIM


def _split3(a):
    hi = a.astype(BF16).astype(F32)
    r = a - hi
    mid = r.astype(BF16).astype(F32)
    return hi, mid, r - mid


def _store_residue_major(out, val, dilation, perm_scr):
    tm, width = val.shape
    if dilation == 1:
        out[0, 0] = val.astype(out.dtype)
        return
    for cb in range(width // LANES):
        perm_scr[cb] = val[:, cb * LANES:(cb + 1) * LANES]
    for r in range(dilation):
        for cb in range(width // LANES):
            out[0, r, :, cb * LANES:(cb + 1) * LANES] = (
                perm_scr[cb, pl.ds(r, tm // dilation, stride=dilation), :].astype(out.dtype))


def _qkv_kernel(x_ref, pos_ref, ada_ref, norm_g_ref, w_ref, gq_ref, gk_ref, freq_ref, e_ref, bd_ref,
                *refs):
    out_refs, perm_scr = refs[:-1], refs[-1]
    xf = x_ref[0]
    tm = xf.shape[0]
    hb = _ada_ln(xf, ada_ref, norm_g_ref).astype(BF16)

    ang = freq_ref[...] * pos_ref[0].astype(F32)
    parts = _split3(jnp.cos(ang)) + _split3(jnp.sin(ang))
    lhs_t = jnp.concatenate(list(parts) + [jnp.ones((8, tm), F32), jnp.zeros((8, tm), F32)], axis=0)
    tab = lax.dot_general(lhs_t.astype(BF16), e_ref[...], (((0,), (0,)), ((), ())),
                          preferred_element_type=F32)
    cos_t, s1_t, s2_t = tab[:, 0:LANES], tab[:, LANES:2 * LANES], tab[:, 2 * LANES:3 * LANES]

    n_groups = len(ATTN_GROUPS)
    chunks = ([(g, which) for which in (0, 1) for g in range(n_groups)]
              + [(g, 2) for g in reversed(range(n_groups))])
    for g, which in chunks:
        dilation = ATTN_GROUPS[g][1]
        col0 = g * GROUP_COLS + which * ATTN_WIDTH
        zc = _dot(hb, w_ref[:, col0:col0 + ATTN_WIDTH])
        out = out_refs[3 * g + which]
        if which == 2:
            _store_residue_major(out, zc, dilation, perm_scr)
            continue
        gain = gq_ref[g:g + 1, :] * Q_SCALE if which == 0 else gk_ref[g:g + 1, :]
        cols = []
        for half in range(ATTN_WIDTH // (2 * LANES)):
            a2 = zc[:, half * 2 * LANES:(half + 1) * 2 * LANES]
            ms = _dot((a2 * a2).astype(BF16), bd_ref[...])
            a2n = a2 * lax.rsqrt(ms + EPS)
            for sub in range(2):
                an = a2n[:, sub * LANES:(sub + 1) * LANES] * gain
                cols.append(an * cos_t + pltpu.roll(an, ROPE_HALF, 1) * s1_t
                            + pltpu.roll(an, LANES - ROPE_HALF, 1) * s2_t)
        _store_residue_major(out, jnp.concatenate(cols, axis=1), dilation, perm_scr)


def _qkv_call(x, pos3, ada3, norm_g2, w_qkv_bf16, gq, gk, freq, e_mat, bd_mat):
    batch, seq, _ = x.shape
    const = lambda b, j: (0, 0)
    out_specs, out_shapes = [], []
    for _, dilation in ATTN_GROUPS:
        spec = pl.BlockSpec((1, dilation, QKV_TILE // dilation, ATTN_WIDTH), lambda b, j: (b, 0, j, 0))
        sds = jax.ShapeDtypeStruct((batch, dilation, seq // dilation, ATTN_WIDTH), BF16)
        out_specs += [spec] * 3
        out_shapes += [sds] * 3
    n_groups = len(ATTN_GROUPS)
    return pl.pallas_call(
        _qkv_kernel,
        grid=(batch, seq // QKV_TILE),
        in_specs=[
            pl.BlockSpec((1, QKV_TILE, D_MODEL), lambda b, j: (b, j, 0)),
            pl.BlockSpec((1, 1, QKV_TILE), lambda b, j: (b, 0, j)),
            pl.BlockSpec((1, 1, 3 * D_MODEL), lambda b, j: (b, 0, 0)),
            pl.BlockSpec((1, D_MODEL), const),
            pl.BlockSpec((D_MODEL, QKV_COLS), const),
            pl.BlockSpec((n_groups, LANES), const),
            pl.BlockSpec((n_groups, LANES), const),
            pl.BlockSpec((8, 1), const),
            pl.BlockSpec((64, 3 * LANES), const),
            pl.BlockSpec((2 * LANES, 2 * LANES), const),
        ],
        out_specs=out_specs,
        out_shape=out_shapes,
        scratch_shapes=[pltpu.VMEM((ATTN_WIDTH // LANES, QKV_TILE, LANES), F32)],
        compiler_params=pltpu.CompilerParams(
            dimension_semantics=("arbitrary", "arbitrary"), vmem_limit_bytes=VMEM_LIMIT_BYTES),
        name="qkv",
    )(x, pos3, ada3, norm_g2, w_qkv_bf16, gq, gk, freq, e_mat, bd_mat)


def _attn_kernel(q_ref, k_ref, v_ref, o_ref, lse_ref, *scratch, dilation, n_blk):
    row = lax.broadcasted_iota(jnp.int32, (Q_BLOCK, LANES), 0)
    col = lax.broadcasted_iota(jnp.int32, (Q_BLOCK, LANES), 1)
    cur_ok = col <= row
    prev_ok = col >= row
    first_ok = jnp.concatenate([cur_ok, cur_ok], axis=0)
    band_one = jnp.concatenate([prev_ok, cur_ok], axis=1)
    band_ok = jnp.concatenate([band_one, band_one], axis=0)
    lane_lo = col < HEAD_DIM
    p_scr, m_scr = scratch[0], scratch[1]
    n_pairs = ATTN_WIDTH // LANES

    def scores(r, r0, first):
        for hp in range(n_pairs):
            lanes = slice(hp * LANES, (hp + 1) * LANES)
            q_pair = q_ref[0, r, pl.ds(r0, Q_BLOCK), lanes]
            zero = jnp.zeros_like(q_pair)
            q2 = jnp.concatenate([jnp.where(lane_lo, q_pair, zero), jnp.where(lane_lo, zero, q_pair)], axis=0)
            if first:
                k_win, ok = k_ref[0, r, pl.ds(r0, Q_BLOCK), lanes], first_ok
            else:
                k_win, ok = k_ref[0, r, pl.ds(r0 - Q_BLOCK, 2 * Q_BLOCK), lanes], band_ok
            s = lax.dot_general(q2, k_win, (((1,), (1,)), ((), ())), preferred_element_type=F32)
            s = jnp.where(ok, s, -jnp.inf)
            m = jnp.max(s, axis=-1, keepdims=True)
            p_scr[hp, :, 0:s.shape[1]] = jnp.exp2(s - m).astype(BF16)
            m_scr[hp] = jnp.where(lane_lo, m[:Q_BLOCK], m[Q_BLOCK:]) * math.log(2.0)

    def values(r, r0, first):
        for hp in range(n_pairs):
            lanes = slice(hp * LANES, (hp + 1) * LANES)
            if first:
                p, v_win = p_scr[hp, :, 0:Q_BLOCK], v_ref[0, r, pl.ds(r0, Q_BLOCK), lanes]
            else:
                p, v_win = p_scr[hp], v_ref[0, r, pl.ds(r0 - Q_BLOCK, 2 * Q_BLOCK), lanes]
            v_ext = jnp.concatenate([v_win, jnp.ones_like(v_win)], axis=1)
            o2 = _dot(p, v_ext)
            num = jnp.where(lane_lo, o2[:Q_BLOCK, :LANES], o2[Q_BLOCK:, :LANES])
            den = jnp.where(lane_lo, o2[:Q_BLOCK, LANES:], o2[Q_BLOCK:, LANES:])
            o_pair = num * (1.0 / den)
            lse_pair = m_scr[hp] + jnp.log(den)
            if dilation == 1:
                o_ref[0, pl.ds(r0, Q_BLOCK), lanes] = o_pair.astype(o_ref.dtype)
                lse_ref[0, pl.ds(r0, Q_BLOCK), lanes] = lse_pair
            else:
                rows = pl.ds(r + r0 * dilation, Q_BLOCK, stride=dilation)
                scratch[2][hp, rows, :] = o_pair
                scratch[3][hp, rows, :] = lse_pair

    def at(blk):
        return pl.multiple_of(blk * Q_BLOCK, Q_BLOCK)

    scores(0, 0, True)
    if n_blk == 1:
        def res_body(r, carry):
            values(r - 1, 0, True)
            scores(r, 0, True)
            return carry
        lax.fori_loop(1, dilation, res_body, 0, unroll=BLOCK_UNROLL)
        values(dilation - 1, 0, True)
    else:
        for r in range(dilation):
            values(r, 0, True)
            scores(r, Q_BLOCK, False)

            def blk_body(blk, carry, r=r):
                values(r, at(blk - 1), False)
                scores(r, at(blk), False)
                return carry
            lax.fori_loop(2, n_blk, blk_body, 0, unroll=BLOCK_UNROLL)
            values(r, (n_blk - 1) * Q_BLOCK, False)
            if r + 1 < dilation:
                scores(r + 1, 0, True)

    if dilation > 1:
        for hp in range(n_pairs):
            lanes = slice(hp * LANES, (hp + 1) * LANES)
            o_ref[0, :, lanes] = scratch[2][hp].astype(o_ref.dtype)
            lse_ref[0, :, lanes] = scratch[3][hp]


def _attn_call(g, dilation, q, k, v):
    batch, _, res_len, _ = q.shape
    seq = dilation * res_len
    in_spec = pl.BlockSpec((1, dilation, res_len, ATTN_WIDTH), lambda b: (b, 0, 0, 0))
    out_spec = pl.BlockSpec((1, seq, ATTN_WIDTH), lambda b: (b, 0, 0))
    plane = pltpu.VMEM((ATTN_WIDTH // LANES, seq, LANES), F32)
    n_pairs = ATTN_WIDTH // LANES
    scratch = [pltpu.VMEM((n_pairs, 2 * Q_BLOCK, 2 * Q_BLOCK), BF16),
               pltpu.VMEM((n_pairs, Q_BLOCK, LANES), F32)]
    if dilation > 1:
        scratch += [plane, plane]
    return pl.pallas_call(
        functools.partial(_attn_kernel, dilation=dilation, n_blk=res_len // Q_BLOCK),
        grid=(batch,),
        in_specs=[in_spec, in_spec, in_spec],
        out_specs=[out_spec, out_spec],
        out_shape=[jax.ShapeDtypeStruct((batch, seq, ATTN_WIDTH), BF16),
                   jax.ShapeDtypeStruct((batch, seq, ATTN_WIDTH), F32)],
        scratch_shapes=scratch,
        compiler_params=pltpu.CompilerParams(
            dimension_semantics=("arbitrary",), vmem_limit_bytes=VMEM_LIMIT_BYTES),
        name=f"attn_g{g}",
    )(q, k, v)


def _merge_kernel(x_ref, ada_ref, norm_g_ref, w_ref,
                  o0_ref, o1_ref, o2_ref, l0_ref, l1_ref, l2_ref,
                  ln_g_ref, ln_b_ref, ws_ref, bias_ref, wa_ref, wb_ref, wo_ref, out_ref):
    xf = x_ref[0]
    tm = xf.shape[0]
    h = _ada_ln(xf, ada_ref, norm_g_ref)
    z = _dot(h.astype(BF16), w_ref[...])
    off = 0
    gate_a = z[:, off:off + ATTN_WIDTH]; off += ATTN_WIDTH
    u = _gelu_exact(z[:, off:off + GMLP_WIDTH]); off += GMLP_WIDTH
    v = _gelu_exact(z[:, off:off + GMLP_WIDTH]); off += GMLP_WIDTH
    gate_b = z[:, off:off + GMLP_WIDTH]; off += GMLP_WIDTH
    merge_a = z[:, off:off + D_MODEL]; off += D_MODEL
    merge_b = z[:, off:off + D_MODEL]

    l0, l1, l2 = l0_ref[0], l1_ref[0], l2_ref[0]
    lmax = jnp.maximum(jnp.maximum(l0, l1), l2)
    e0, e1, e2 = jnp.exp(l0 - lmax), jnp.exp(l1 - lmax), jnp.exp(l2 - lmax)
    attn = (e0 * o0_ref[0].astype(F32) + e1 * o1_ref[0].astype(F32) + e2 * o2_ref[0].astype(F32)) \
        / (e0 + e1 + e2)
    y_a = attn * _silu(gate_a)

    mu = jnp.mean(v, axis=-1, keepdims=True)
    vc = v - mu
    var = jnp.mean(vc * vc, axis=-1, keepdims=True)
    v_ln = (vc * lax.rsqrt(var + EPS) * ln_g_ref[...] + ln_b_ref[...]).astype(BF16)
    row = lax.broadcasted_iota(jnp.int32, (CHUNK, CHUNK), 0)
    col = lax.broadcasted_iota(jnp.int32, (CHUNK, CHUNK), 1)
    causal = row >= col
    lane_lo = col < GMLP_GROUP_DIM
    n_chunks = tm // CHUNK
    sv_cols = []
    for gp in range(GMLP_WIDTH // LANES):
        lanes = slice(gp * LANES, (gp + 1) * LANES)
        rhs = jnp.concatenate([v_ln[c * CHUNK:(c + 1) * CHUNK, lanes] for c in range(n_chunks)], axis=1)
        res = []
        for half in range(2):
            w_s = jnp.where(causal, ws_ref[2 * gp + half], 0.0).astype(BF16)
            res.append(_dot(w_s, rhs))
        sel = jnp.concatenate(
            [jnp.where(lane_lo, res[0][:, c * LANES:(c + 1) * LANES], res[1][:, c * LANES:(c + 1) * LANES])
             + bias_ref[:, lanes] for c in range(n_chunks)], axis=0)
        sv_cols.append(sel)
    sv = jnp.concatenate(sv_cols, axis=1)
    y_b = u * sv * _silu(gate_b)

    merged = (_sigmoid(merge_a) * _dot(y_a.astype(BF16), wa_ref[...])
              + _sigmoid(merge_b) * _dot(y_b.astype(BF16), wb_ref[...]))
    out = _dot(merged.astype(BF16), wo_ref[...])
    gate = ada_ref[0, :, 2 * D_MODEL:3 * D_MODEL]
    out_ref[0] = xf + gate * out


def _merge_call(x, ada3, norm_g2, w_in_bf16, os_, lses, ln_g, ln_b, w_spatial, bias_tab, wa, wb, wo):
    batch, seq, _ = x.shape
    const2 = lambda b, i: (0, 0)
    tile = lambda width: pl.BlockSpec((1, MERGE_TILE, width), lambda b, i: (b, i, 0))
    return pl.pallas_call(
        _merge_kernel,
        grid=(batch, seq // MERGE_TILE),
        in_specs=[
            tile(D_MODEL),
            pl.BlockSpec((1, 1, 3 * D_MODEL), lambda b, i: (b, 0, 0)),
            pl.BlockSpec((1, D_MODEL), const2),
            pl.BlockSpec((D_MODEL, REST_COLS), const2),
            tile(ATTN_WIDTH), tile(ATTN_WIDTH), tile(ATTN_WIDTH),
            tile(ATTN_WIDTH), tile(ATTN_WIDTH), tile(ATTN_WIDTH),
            pl.BlockSpec((1, GMLP_WIDTH), const2),
            pl.BlockSpec((1, GMLP_WIDTH), const2),
            pl.BlockSpec((GMLP_GROUPS, CHUNK, CHUNK), lambda b, i: (0, 0, 0)),
            pl.BlockSpec((CHUNK, GMLP_WIDTH), const2),
            pl.BlockSpec((ATTN_WIDTH, D_MODEL), const2),
            pl.BlockSpec((GMLP_WIDTH, D_MODEL), const2),
            pl.BlockSpec((D_MODEL, D_MODEL), const2),
        ],
        out_specs=tile(D_MODEL),
        out_shape=jax.ShapeDtypeStruct(x.shape, x.dtype),
        compiler_params=pltpu.CompilerParams(
            dimension_semantics=("arbitrary", "arbitrary"), vmem_limit_bytes=VMEM_LIMIT_BYTES),
        name="merge",
    )(x, ada3, norm_g2, w_in_bf16, *os_, *lses, ln_g, ln_b, w_spatial, bias_tab, wa, wb, wo)


def kernel(x, c, positions, norm_g, w_ada, b_ada, w_in, q_norm_g, k_norm_g, sgu_ln_g, sgu_ln_b,
           w_spatial, b_spatial, w_branch_a, w_branch_b, w_out):
    batch, seq, d_model = x.shape
    assert d_model == D_MODEL and seq % QKV_TILE == 0 and seq % MERGE_TILE == 0
    assert w_in.shape == (D_MODEL, QKV_COLS + REST_COLS)

    w_qkv_bf16 = w_in[:, :QKV_COLS].astype(BF16)
    w_rest_bf16 = w_in[:, QKV_COLS:].astype(BF16)
    ada = _ada_call(c, w_ada.astype(BF16), b_ada)
    ada3 = ada.reshape(batch, 1, 3 * D_MODEL)
    norm_g2 = norm_g.reshape(1, D_MODEL)

    freq = (ROPE_THETA ** (-np.arange(0, ROPE_DIMS, 2, dtype=np.float32) / ROPE_DIMS)).astype(np.float32)
    freq = jnp.asarray(freq.reshape(ROPE_HALF, 1))
    e_mat = jnp.asarray(_rope_expand_matrix(), dtype=BF16)
    bd_mat = jnp.asarray(_head_mean_matrix(), dtype=BF16)

    gq = jnp.tile(q_norm_g, (1, LANES // HEAD_DIM))
    gk = jnp.tile(k_norm_g, (1, LANES // HEAD_DIM))
    qkv = _qkv_call(x, positions.reshape(batch, 1, seq), ada3, norm_g2, w_qkv_bf16, gq, gk,
                    freq, e_mat, bd_mat)
    outs, lses = [], []
    for g, (window, dilation) in enumerate(ATTN_GROUPS):
        assert window // dilation == Q_BLOCK
        o, lse = _attn_call(g, dilation, *qkv[3 * g:3 * g + 3])
        outs.append(o)
        lses.append(lse)

    bias_tab = jnp.repeat(b_spatial.T, GMLP_GROUP_DIM, axis=1)
    return _merge_call(x, ada3, norm_g2, w_rest_bf16, outs, lses,
                       sgu_ln_g.reshape(1, GMLP_WIDTH), sgu_ln_b.reshape(1, GMLP_WIDTH),
                       w_spatial, bias_tab,
                       w_branch_a.astype(BF16), w_branch_b.astype(BF16), w_out.astype(BF16))
```

```python
import functools
import math

import jax
import jax.numpy as jnp
import numpy as np
from jax import lax
from jax.experimental import pallas as pl
from jax.experimental.pallas import tpu as pltpu

D_MODEL = 1024
HEAD_DIM = 64
HEADS = 8
ATTN_GROUPS = ((128, 1), (512, 4), (2048, 16))
ATTN_WIDTH = HEADS * HEAD_DIM
Q_BLOCK = 128
ROPE_THETA = 500000.0
ROPE_DIMS = HEAD_DIM // 4
ROPE_HALF = ROPE_DIMS // 2
GMLP_WIDTH = 512
GMLP_GROUPS = 8
GMLP_GROUP_DIM = GMLP_WIDTH // GMLP_GROUPS
CHUNK = 128
EPS = 1e-6
GROUP_COLS = 3 * ATTN_WIDTH
QKV_COLS = len(ATTN_GROUPS) * GROUP_COLS
REST_COLS = ATTN_WIDTH + 3 * GMLP_WIDTH + 2 * D_MODEL

Q_SCALE = math.log2(math.e) / math.sqrt(HEAD_DIM)

LANES = 128
QKV_TILE = 512
MERGE_TILE = 512
BLOCK_UNROLL = 4
VMEM_LIMIT_BYTES = 56 * 1024 * 1024

F32 = jnp.float32
BF16 = jnp.bfloat16


def _dot(a, b):
    return jnp.dot(a, b, preferred_element_type=F32)


def _resident(shape):
    return pl.BlockSpec(shape, lambda *_: (0, 0), pipeline_mode=pl.Buffered(1))


def _silu(v):
    return v * (1.0 / (1.0 + jnp.exp(-v)))


def _sigmoid(v):
    return 1.0 / (1.0 + jnp.exp(-v))


def _gelu_exact(v):
    return 0.5 * v * (1.0 + lax.erf(v * (1.0 / math.sqrt(2.0))))


def _ada_ln(xf, ada_ref, norm_g_ref):
    ms = jnp.mean(xf * xf, axis=-1, keepdims=True)
    shift = ada_ref[0, :, 0:D_MODEL]
    scale = ada_ref[0, :, D_MODEL:2 * D_MODEL]
    return xf * lax.rsqrt(ms + EPS) * norm_g_ref[...] * (1.0 + scale) + shift


def _ada_kernel(c_ref, w_ref, b_ref, o_ref):
    o_ref[...] = _dot(_silu(c_ref[...]).astype(BF16), w_ref[...]) + b_ref[...]


def _ada_call(c, w_ada_bf16, b_ada):
    batch = c.shape[0]
    return pl.pallas_call(
        _ada_kernel,
        out_shape=jax.ShapeDtypeStruct((batch, 3 * D_MODEL), F32),
        compiler_params=pltpu.CompilerParams(vmem_limit_bytes=VMEM_LIMIT_BYTES),
        name="ada",
    )(c, w_ada_bf16, b_ada.reshape(1, 3 * D_MODEL))


def _rope_expand_matrix():
    e = np.zeros((64, 3 * LANES), np.float32)
    for lane in range(LANES):
        dim = lane % HEAD_DIM
        if dim < ROPE_DIMS:
            j = dim % ROPE_HALF
            for part in range(3):
                e[part * 8 + j, lane] = 1.0
                if dim >= ROPE_HALF:
                    e[24 + part * 8 + j, LANES + lane] = 1.0
                else:
                    e[24 + part * 8 + j, 2 * LANES + lane] = -1.0
        else:
            e[48, lane] = 1.0
    return e


def _head_mean_matrix():
    head = np.arange(2 * LANES) // HEAD_DIM
    return (head[:, None] == head[None, :]).astype(np.float32) / HEAD_DIM


def _split3(a):
    hi = a.astype(BF16).astype(F32)
    r = a - hi
    mid = r.astype(BF16).astype(F32)
    return hi, mid, r - mid


def _store_residue_major(out, val, dilation, perm_scr):
    tm, width = val.shape
    if dilation == 1:
        out[0, 0] = val.astype(out.dtype)
        return
    for cb in range(width // LANES):
        perm_scr[cb] = val[:, cb * LANES:(cb + 1) * LANES]
    for r in range(dilation):
        for cb in range(width // LANES):
            out[0, r, :, cb * LANES:(cb + 1) * LANES] = (
                perm_scr[cb, pl.ds(r, tm // dilation, stride=dilation), :].astype(out.dtype))


def _qkv_kernel(x_ref, pos_ref, ada_ref, norm_g_ref, w_ref, gq_ref, gk_ref, freq_ref, e_ref, bd_ref,
                *refs):
    out_refs, perm_scr = refs[:-1], refs[-1]
    xf = x_ref[0]
    tm = xf.shape[0]
    hb = _ada_ln(xf, ada_ref, norm_g_ref).astype(BF16)

    ang = freq_ref[...] * pos_ref[0].astype(F32)
    parts = _split3(jnp.cos(ang)) + _split3(jnp.sin(ang))
    lhs_t = jnp.concatenate(list(parts) + [jnp.ones((8, tm), F32), jnp.zeros((8, tm), F32)], axis=0)
    tab = lax.dot_general(lhs_t.astype(BF16), e_ref[...], (((0,), (0,)), ((), ())),
                          preferred_element_type=F32)
    cos_t, s1_t, s2_t = tab[:, 0:LANES], tab[:, LANES:2 * LANES], tab[:, 2 * LANES:3 * LANES]

    n_groups = len(ATTN_GROUPS)
    chunks = ([(g, which) for which in (0, 1) for g in range(n_groups)]
              + [(g, 2) for g in reversed(range(n_groups))])
    for g, which in chunks:
        dilation = ATTN_GROUPS[g][1]
        col0 = g * GROUP_COLS + which * ATTN_WIDTH
        zc = _dot(hb, w_ref[:, col0:col0 + ATTN_WIDTH])
        out = out_refs[3 * g + which]
        if which == 2:
            _store_residue_major(out, zc, dilation, perm_scr)
            continue
        gain = gq_ref[g:g + 1, :] * Q_SCALE if which == 0 else gk_ref[g:g + 1, :]
        cols = []
        for half in range(ATTN_WIDTH // (2 * LANES)):
            a2 = zc[:, half * 2 * LANES:(half + 1) * 2 * LANES]
            ms = _dot((a2 * a2).astype(BF16), bd_ref[...])
            a2n = a2 * lax.rsqrt(ms + EPS)
            for sub in range(2):
                an = a2n[:, sub * LANES:(sub + 1) * LANES] * gain
                cols.append(an * cos_t + pltpu.roll(an, ROPE_HALF, 1) * s1_t
                            + pltpu.roll(an, LANES - ROPE_HALF, 1) * s2_t)
        _store_residue_major(out, jnp.concatenate(cols, axis=1), dilation, perm_scr)


def _qkv_call(x, pos3, ada3, norm_g2, w_qkv_bf16, gq, gk, freq, e_mat, bd_mat):
    batch, seq, _ = x.shape
    const = lambda b, j: (0, 0)
    out_specs, out_shapes = [], []
    for _, dilation in ATTN_GROUPS:
        spec = pl.BlockSpec((1, dilation, QKV_TILE // dilation, ATTN_WIDTH), lambda b, j: (b, 0, j, 0))
        sds = jax.ShapeDtypeStruct((batch, dilation, seq // dilation, ATTN_WIDTH), BF16)
        out_specs += [spec] * 3
        out_shapes += [sds] * 3
    n_groups = len(ATTN_GROUPS)
    return pl.pallas_call(
        _qkv_kernel,
        grid=(batch, seq // QKV_TILE),
        in_specs=[
            pl.BlockSpec((1, QKV_TILE, D_MODEL), lambda b, j: (b, j, 0)),
            pl.BlockSpec((1, 1, QKV_TILE), lambda b, j: (b, 0, j)),
            pl.BlockSpec((1, 1, 3 * D_MODEL), lambda b, j: (b, 0, 0)),
            pl.BlockSpec((1, D_MODEL), const),
            _resident((D_MODEL, QKV_COLS)),
            pl.BlockSpec((n_groups, LANES), const),
            pl.BlockSpec((n_groups, LANES), const),
            pl.BlockSpec((8, 1), const),
            pl.BlockSpec((64, 3 * LANES), const),
            pl.BlockSpec((2 * LANES, 2 * LANES), const),
        ],
        out_specs=out_specs,
        out_shape=out_shapes,
        scratch_shapes=[pltpu.VMEM((ATTN_WIDTH // LANES, QKV_TILE, LANES), F32)],
        compiler_params=pltpu.CompilerParams(
            dimension_semantics=("arbitrary", "arbitrary"), vmem_limit_bytes=VMEM_LIMIT_BYTES),
        name="qkv",
    )(x, pos3, ada3, norm_g2, w_qkv_bf16, gq, gk, freq, e_mat, bd_mat)


def _attn_kernel(q_ref, k_ref, v_ref, o_ref, lse_ref, *scratch, dilation, n_blk):
    row = lax.broadcasted_iota(jnp.int32, (Q_BLOCK, LANES), 0)
    col = lax.broadcasted_iota(jnp.int32, (Q_BLOCK, LANES), 1)
    cur_ok = col <= row
    prev_ok = col >= row
    first_ok = jnp.concatenate([cur_ok, cur_ok], axis=0)
    band_one = jnp.concatenate([prev_ok, cur_ok], axis=1)
    band_ok = jnp.concatenate([band_one, band_one], axis=0)
    lane_lo = col < HEAD_DIM
    p_scr, m_scr = scratch[0], scratch[1]
    n_pairs = ATTN_WIDTH // LANES

    def scores(r, r0, first):
        for hp in range(n_pairs):
            lanes = slice(hp * LANES, (hp + 1) * LANES)
            q_pair = q_ref[0, r, pl.ds(r0, Q_BLOCK), lanes]
            zero = jnp.zeros_like(q_pair)
            q2 = jnp.concatenate([jnp.where(lane_lo, q_pair, zero), jnp.where(lane_lo, zero, q_pair)], axis=0)
            if first:
                k_win, ok = k_ref[0, r, pl.ds(r0, Q_BLOCK), lanes], first_ok
            else:
                k_win, ok = k_ref[0, r, pl.ds(r0 - Q_BLOCK, 2 * Q_BLOCK), lanes], band_ok
            s = lax.dot_general(q2, k_win, (((1,), (1,)), ((), ())), preferred_element_type=F32)
            s = jnp.where(ok, s, -jnp.inf)
            m = jnp.max(s, axis=-1, keepdims=True)
            p_scr[hp, :, 0:s.shape[1]] = jnp.exp2(s - m).astype(BF16)
            m_scr[hp] = jnp.where(lane_lo, m[:Q_BLOCK], m[Q_BLOCK:]) * math.log(2.0)

    def values(r, r0, first):
        for hp in range(n_pairs):
            lanes = slice(hp * LANES, (hp + 1) * LANES)
            if first:
                p, v_win = p_scr[hp, :, 0:Q_BLOCK], v_ref[0, r, pl.ds(r0, Q_BLOCK), lanes]
            else:
                p, v_win = p_scr[hp], v_ref[0, r, pl.ds(r0 - Q_BLOCK, 2 * Q_BLOCK), lanes]
            v_ext = jnp.concatenate([v_win, jnp.ones_like(v_win)], axis=1)
            o2 = _dot(p, v_ext)
            num = jnp.where(lane_lo, o2[:Q_BLOCK, :LANES], o2[Q_BLOCK:, :LANES])
            den = jnp.where(lane_lo, o2[:Q_BLOCK, LANES:], o2[Q_BLOCK:, LANES:])
            o_pair = num * (1.0 / den)
            lse_pair = m_scr[hp] + jnp.log(den)
            if dilation == 1:
                o_ref[0, pl.ds(r0, Q_BLOCK), lanes] = o_pair.astype(o_ref.dtype)
                lse_ref[0, pl.ds(r0, Q_BLOCK), lanes] = lse_pair
            else:
                rows = pl.ds(r + r0 * dilation, Q_BLOCK, stride=dilation)
                scratch[2][hp, rows, :] = o_pair
                scratch[3][hp, rows, :] = lse_pair

    def at(blk):
        return pl.multiple_of(blk * Q_BLOCK, Q_BLOCK)

    scores(0, 0, True)
    if n_blk == 1:
        def res_body(r, carry):
            values(r - 1, 0, True)
            scores(r, 0, True)
            return carry
        lax.fori_loop(1, dilation, res_body, 0, unroll=BLOCK_UNROLL)
        values(dilation - 1, 0, True)
    else:
        for r in range(dilation):
            values(r, 0, True)
            scores(r, Q_BLOCK, False)

            def blk_body(blk, carry, r=r):
                values(r, at(blk - 1), False)
                scores(r, at(blk), False)
                return carry
            lax.fori_loop(2, n_blk, blk_body, 0, unroll=BLOCK_UNROLL)
            values(r, (n_blk - 1) * Q_BLOCK, False)
            if r + 1 < dilation:
                scores(r + 1, 0, True)

    if dilation > 1:
        for hp in range(n_pairs):
            lanes = slice(hp * LANES, (hp + 1) * LANES)
            o_ref[0, :, lanes] = scratch[2][hp].astype(o_ref.dtype)
            lse_ref[0, :, lanes] = scratch[3][hp]


def _attn_call(g, dilation, q, k, v):
    batch, _, res_len, _ = q.shape
    seq = dilation * res_len
    in_spec = pl.BlockSpec((1, dilation, res_len, ATTN_WIDTH), lambda b: (b, 0, 0, 0))
    out_spec = pl.BlockSpec((1, seq, ATTN_WIDTH), lambda b: (b, 0, 0))
    plane = pltpu.VMEM((ATTN_WIDTH // LANES, seq, LANES), F32)
    n_pairs = ATTN_WIDTH // LANES
    scratch = [pltpu.VMEM((n_pairs, 2 * Q_BLOCK, 2 * Q_BLOCK), BF16),
               pltpu.VMEM((n_pairs, Q_BLOCK, LANES), F32)]
    if dilation > 1:
        scratch += [plane, plane]
    return pl.pallas_call(
        functools.partial(_attn_kernel, dilation=dilation, n_blk=res_len // Q_BLOCK),
        grid=(batch,),
        in_specs=[in_spec, in_spec, in_spec],
        out_specs=[out_spec, out_spec],
        out_shape=[jax.ShapeDtypeStruct((batch, seq, ATTN_WIDTH), BF16),
                   jax.ShapeDtypeStruct((batch, seq, ATTN_WIDTH), F32)],
        scratch_shapes=scratch,
        compiler_params=pltpu.CompilerParams(
            dimension_semantics=("arbitrary",), vmem_limit_bytes=VMEM_LIMIT_BYTES),
        name=f"attn_g{g}",
    )(q, k, v)


def _merge_kernel(x_ref, ada_ref, norm_g_ref, w_ref,
                  o0_ref, o1_ref, o2_ref, l0_ref, l1_ref, l2_ref,
                  ln_g_ref, ln_b_ref, ws_ref, bias_ref, wa_ref, wb_ref, wo_ref, out_ref):
    xf = x_ref[0]
    tm = xf.shape[0]
    h = _ada_ln(xf, ada_ref, norm_g_ref)
    z = _dot(h.astype(BF16), w_ref[...])
    off = 0
    gate_a = z[:, off:off + ATTN_WIDTH]; off += ATTN_WIDTH
    u = _gelu_exact(z[:, off:off + GMLP_WIDTH]); off += GMLP_WIDTH
    v = _gelu_exact(z[:, off:off + GMLP_WIDTH]); off += GMLP_WIDTH
    gate_b = z[:, off:off + GMLP_WIDTH]; off += GMLP_WIDTH
    merge_a = z[:, off:off + D_MODEL]; off += D_MODEL
    merge_b = z[:, off:off + D_MODEL]

    l0, l1, l2 = l0_ref[0], l1_ref[0], l2_ref[0]
    lmax = jnp.maximum(jnp.maximum(l0, l1), l2)
    e0, e1, e2 = jnp.exp(l0 - lmax), jnp.exp(l1 - lmax), jnp.exp(l2 - lmax)
    attn = (e0 * o0_ref[0].astype(F32) + e1 * o1_ref[0].astype(F32) + e2 * o2_ref[0].astype(F32)) \
        / (e0 + e1 + e2)
    y_a = attn * _silu(gate_a)

    mu = jnp.mean(v, axis=-1, keepdims=True)
    vc = v - mu
    var = jnp.mean(vc * vc, axis=-1, keepdims=True)
    v_ln = (vc * lax.rsqrt(var + EPS) * ln_g_ref[...] + ln_b_ref[...]).astype(BF16)
    row = lax.broadcasted_iota(jnp.int32, (CHUNK, CHUNK), 0)
    col = lax.broadcasted_iota(jnp.int32, (CHUNK, CHUNK), 1)
    causal = row >= col
    lane_lo = col < GMLP_GROUP_DIM
    n_chunks = tm // CHUNK
    sv_cols = []
    for gp in range(GMLP_WIDTH // LANES):
        lanes = slice(gp * LANES, (gp + 1) * LANES)
        rhs = jnp.concatenate([v_ln[c * CHUNK:(c + 1) * CHUNK, lanes] for c in range(n_chunks)], axis=1)
        res = []
        for half in range(2):
            w_s = jnp.where(causal, ws_ref[2 * gp + half], 0.0).astype(BF16)
            res.append(_dot(w_s, rhs))
        sel = jnp.concatenate(
            [jnp.where(lane_lo, res[0][:, c * LANES:(c + 1) * LANES], res[1][:, c * LANES:(c + 1) * LANES])
             + bias_ref[:, lanes] for c in range(n_chunks)], axis=0)
        sv_cols.append(sel)
    sv = jnp.concatenate(sv_cols, axis=1)
    y_b = u * sv * _silu(gate_b)

    merged = (_sigmoid(merge_a) * _dot(y_a.astype(BF16), wa_ref[...])
              + _sigmoid(merge_b) * _dot(y_b.astype(BF16), wb_ref[...]))
    out = _dot(merged.astype(BF16), wo_ref[...])
    gate = ada_ref[0, :, 2 * D_MODEL:3 * D_MODEL]
    out_ref[0] = xf + gate * out


def _merge_call(x, ada3, norm_g2, w_in_bf16, os_, lses, ln_g, ln_b, w_spatial, bias_tab, wa, wb, wo):
    batch, seq, _ = x.shape
    const2 = lambda b, i: (0, 0)
    tile = lambda width: pl.BlockSpec((1, MERGE_TILE, width), lambda b, i: (b, i, 0))
    return pl.pallas_call(
        _merge_kernel,
        grid=(batch, seq // MERGE_TILE),
        in_specs=[
            tile(D_MODEL),
            pl.BlockSpec((1, 1, 3 * D_MODEL), lambda b, i: (b, 0, 0)),
            pl.BlockSpec((1, D_MODEL), const2),
            _resident((D_MODEL, REST_COLS)),
            tile(ATTN_WIDTH), tile(ATTN_WIDTH), tile(ATTN_WIDTH),
            tile(ATTN_WIDTH), tile(ATTN_WIDTH), tile(ATTN_WIDTH),
            pl.BlockSpec((1, GMLP_WIDTH), const2),
            pl.BlockSpec((1, GMLP_WIDTH), const2),
            pl.BlockSpec((GMLP_GROUPS, CHUNK, CHUNK), lambda b, i: (0, 0, 0)),
            pl.BlockSpec((CHUNK, GMLP_WIDTH), const2),
            _resident((ATTN_WIDTH, D_MODEL)),
            _resident((GMLP_WIDTH, D_MODEL)),
            _resident((D_MODEL, D_MODEL)),
        ],
        out_specs=tile(D_MODEL),
        out_shape=jax.ShapeDtypeStruct(x.shape, x.dtype),
        compiler_params=pltpu.CompilerParams(
            dimension_semantics=("arbitrary", "arbitrary"), vmem_limit_bytes=VMEM_LIMIT_BYTES),
        name="merge",
    )(x, ada3, norm_g2, w_in_bf16, *os_, *lses, ln_g, ln_b, w_spatial, bias_tab, wa, wb, wo)


def kernel(x, c, positions, norm_g, w_ada, b_ada, w_in, q_norm_g, k_norm_g, sgu_ln_g, sgu_ln_b,
           w_spatial, b_spatial, w_branch_a, w_branch_b, w_out):
    batch, seq, d_model = x.shape
    assert d_model == D_MODEL and seq % QKV_TILE == 0 and seq % MERGE_TILE == 0
    assert w_in.shape == (D_MODEL, QKV_COLS + REST_COLS)

    w_qkv_bf16 = w_in[:, :QKV_COLS].astype(BF16)
    w_rest_bf16 = w_in[:, QKV_COLS:].astype(BF16)
    ada = _ada_call(c, w_ada.astype(BF16), b_ada)
    ada3 = ada.reshape(batch, 1, 3 * D_MODEL)
    norm_g2 = norm_g.reshape(1, D_MODEL)

    freq = (ROPE_THETA ** (-np.arange(0, ROPE_DIMS, 2, dtype=np.float32) / ROPE_DIMS)).astype(np.float32)
    freq = jnp.asarray(freq.reshape(ROPE_HALF, 1))
    e_mat = jnp.asarray(_rope_expand_matrix(), dtype=BF16)
    bd_mat = jnp.asarray(_head_mean_matrix(), dtype=BF16)

    gq = jnp.tile(q_norm_g, (1, LANES // HEAD_DIM))
    gk = jnp.tile(k_norm_g, (1, LANES // HEAD_DIM))
    qkv = _qkv_call(x, positions.reshape(batch, 1, seq), ada3, norm_g2, w_qkv_bf16, gq, gk,
                    freq, e_mat, bd_mat)
    outs, lses = [], []
    for g, (window, dilation) in enumerate(ATTN_GROUPS):
        assert window // dilation == Q_BLOCK
        o, lse = _attn_call(g, dilation, *qkv[3 * g:3 * g + 3])
        outs.append(o)
        lses.append(lse)

    bias_tab = jnp.repeat(b_spatial.T, GMLP_GROUP_DIM, axis=1)
    return _merge_call(x, ada3, norm_g2, w_rest_bf16, outs, lses,
                       sgu_ln_g.reshape(1, GMLP_WIDTH), sgu_ln_b.reshape(1, GMLP_WIDTH),
                       w_spatial, bias_tab,
                       w_branch_a.astype(BF16), w_branch_b.astype(BF16), w_out.astype(BF16))
```

```python
import functools
import math

import jax
import jax.numpy as jnp
import numpy as np
from jax import lax
from jax.experimental import pallas as pl
from jax.experimental.pallas import tpu as pltpu

D_MODEL = 1024
HEAD_DIM = 64
HEADS = 8
ATTN_GROUPS = ((128, 1), (512, 4), (2048, 16))
ATTN_WIDTH = HEADS * HEAD_DIM
Q_BLOCK = 128
ROPE_THETA = 500000.0
ROPE_DIMS = HEAD_DIM // 4
ROPE_HALF = ROPE_DIMS // 2
GMLP_WIDTH = 512
GMLP_GROUPS = 8
GMLP_GROUP_DIM = GMLP_WIDTH // GMLP_GROUPS
CHUNK = 128
EPS = 1e-6
GROUP_COLS = 3 * ATTN_WIDTH
QKV_COLS = len(ATTN_GROUPS) * GROUP_COLS
REST_COLS = ATTN_WIDTH + 3 * GMLP_WIDTH + 2 * D_MODEL

Q_SCALE = math.log2(math.e) / math.sqrt(HEAD_DIM)

LANES = 128
QKV_TILE = 512
MERGE_TILE = 512
BLOCK_UNROLL = 4
VMEM_LIMIT_BYTES = 56 * 1024 * 1024

F32 = jnp.float32
BF16 = jnp.bfloat16


def _dot(a, b):
    return jnp.dot(a, b, preferred_element_type=F32)


def _resident(shape):
    return pl.BlockSpec(shape, lambda *_: (0, 0), pipeline_mode=pl.Buffered(1))


def _silu(v):
    return v * (1.0 / (1.0 + jnp.exp(-v)))


def _sigmoid(v):
    return 1.0 / (1.0 + jnp.exp(-v))


def _gelu_exact(v):
    return 0.5 * v * (1.0 + lax.erf(v * (1.0 / math.sqrt(2.0))))


def _ada_ln(xf, ada_ref, norm_g_ref):
    ms = jnp.mean(xf * xf, axis=-1, keepdims=True)
    shift = ada_ref[0, :, 0:D_MODEL]
    scale = ada_ref[0, :, D_MODEL:2 * D_MODEL]
    return xf * lax.rsqrt(ms + EPS) * norm_g_ref[...] * (1.0 + scale) + shift


def _ada_kernel(c_ref, w_ref, b_ref, o_ref):
    o_ref[...] = _dot(_silu(c_ref[...]).astype(BF16), w_ref[...].astype(BF16)) + b_ref[...]


def _ada_call(c, w_ada, b_ada):
    batch = c.shape[0]
    return pl.pallas_call(
        _ada_kernel,
        grid=(3,),
        in_specs=[pl.BlockSpec((batch, D_MODEL), lambda n: (0, 0)),
                  pl.BlockSpec((D_MODEL, D_MODEL), lambda n: (0, n)),
                  pl.BlockSpec((1, D_MODEL), lambda n: (0, n))],
        out_specs=pl.BlockSpec((batch, D_MODEL), lambda n: (0, n)),
        out_shape=jax.ShapeDtypeStruct((batch, 3 * D_MODEL), F32),
        compiler_params=pltpu.CompilerParams(dimension_semantics=("arbitrary",)),
        name="ada",
    )(c, w_ada, b_ada.reshape(1, 3 * D_MODEL))


def _rope_expand_matrix():
    e = np.zeros((64, 3 * LANES), np.float32)
    for lane in range(LANES):
        dim = lane % HEAD_DIM
        if dim < ROPE_DIMS:
            j = dim % ROPE_HALF
            for part in range(3):
                e[part * 8 + j, lane] = 1.0
                if dim >= ROPE_HALF:
                    e[24 + part * 8 + j, LANES + lane] = 1.0
                else:
                    e[24 + part * 8 + j, 2 * LANES + lane] = -1.0
        else:
            e[48, lane] = 1.0
    return e


def _head_mean_matrix():
    head = np.arange(2 * LANES) // HEAD_DIM
    return (head[:, None] == head[None, :]).astype(np.float32) / HEAD_DIM


def _split3(a):
    hi = a.astype(BF16).astype(F32)
    r = a - hi
    mid = r.astype(BF16).astype(F32)
    return hi, mid, r - mid


def _store_residue_major(out, val, dilation, perm_scr):
    tm, width = val.shape
    if dilation == 1:
        out[0, 0] = val.astype(out.dtype)
        return
    for cb in range(width // LANES):
        perm_scr[cb] = val[:, cb * LANES:(cb + 1) * LANES]
    for r in range(dilation):
        for cb in range(width // LANES):
            out[0, r, :, cb * LANES:(cb + 1) * LANES] = (
                perm_scr[cb, pl.ds(r, tm // dilation, stride=dilation), :].astype(out.dtype))


def _qkv_kernel(x_ref, pos_ref, ada_ref, norm_g_ref, gq_ref, gk_ref, freq_ref, e_ref, bd_ref, *refs):
    n_chunks = QKV_COLS // ATTN_WIDTH
    w_chunk_refs, out_refs = refs[:n_chunks], refs[n_chunks:2 * n_chunks]
    perm_scr, w_ref = refs[2 * n_chunks:]

    @pl.when((pl.program_id(0) == 0) & (pl.program_id(1) == 0))
    def _():
        for c, chunk_ref in enumerate(w_chunk_refs):
            w_ref[:, c * ATTN_WIDTH:(c + 1) * ATTN_WIDTH] = chunk_ref[...].astype(BF16)

    xf = x_ref[0]
    tm = xf.shape[0]
    hb = _ada_ln(xf, ada_ref, norm_g_ref).astype(BF16)

    ang = freq_ref[...] * pos_ref[0].astype(F32)
    parts = _split3(jnp.cos(ang)) + _split3(jnp.sin(ang))
    lhs_t = jnp.concatenate(list(parts) + [jnp.ones((8, tm), F32), jnp.zeros((8, tm), F32)], axis=0)
    tab = lax.dot_general(lhs_t.astype(BF16), e_ref[...], (((0,), (0,)), ((), ())),
                          preferred_element_type=F32)
    cos_t, s1_t, s2_t = tab[:, 0:LANES], tab[:, LANES:2 * LANES], tab[:, 2 * LANES:3 * LANES]

    n_groups = len(ATTN_GROUPS)
    chunks = ([(g, which) for which in (0, 1) for g in range(n_groups)]
              + [(g, 2) for g in reversed(range(n_groups))])
    for g, which in chunks:
        dilation = ATTN_GROUPS[g][1]
        col0 = g * GROUP_COLS + which * ATTN_WIDTH
        zc = _dot(hb, w_ref[:, col0:col0 + ATTN_WIDTH])
        out = out_refs[3 * g + which]
        if which == 2:
            _store_residue_major(out, zc, dilation, perm_scr)
            continue
        gain = gq_ref[g:g + 1, :] * Q_SCALE if which == 0 else gk_ref[g:g + 1, :]
        cols = []
        for half in range(ATTN_WIDTH // (2 * LANES)):
            a2 = zc[:, half * 2 * LANES:(half + 1) * 2 * LANES]
            ms = _dot((a2 * a2).astype(BF16), bd_ref[...])
            a2n = a2 * lax.rsqrt(ms + EPS)
            for sub in range(2):
                an = a2n[:, sub * LANES:(sub + 1) * LANES] * gain
                cols.append(an * cos_t + pltpu.roll(an, ROPE_HALF, 1) * s1_t
                            + pltpu.roll(an, LANES - ROPE_HALF, 1) * s2_t)
        _store_residue_major(out, jnp.concatenate(cols, axis=1), dilation, perm_scr)


def _qkv_call(x, pos3, ada3, norm_g2, w_in, gq, gk, freq, e_mat, bd_mat):
    batch, seq, _ = x.shape
    const = lambda b, j: (0, 0)
    n_chunks = QKV_COLS // ATTN_WIDTH
    w_specs = [pl.BlockSpec((D_MODEL, ATTN_WIDTH), functools.partial(lambda c, b, j: (0, c), c),
                            pipeline_mode=pl.Buffered(1)) for c in range(n_chunks)]
    out_specs, out_shapes = [], []
    for _, dilation in ATTN_GROUPS:
        spec = pl.BlockSpec((1, dilation, QKV_TILE // dilation, ATTN_WIDTH), lambda b, j: (b, 0, j, 0))
        sds = jax.ShapeDtypeStruct((batch, dilation, seq // dilation, ATTN_WIDTH), BF16)
        out_specs += [spec] * 3
        out_shapes += [sds] * 3
    n_groups = len(ATTN_GROUPS)
    return pl.pallas_call(
        _qkv_kernel,
        grid=(batch, seq // QKV_TILE),
        in_specs=[
            pl.BlockSpec((1, QKV_TILE, D_MODEL), lambda b, j: (b, j, 0)),
            pl.BlockSpec((1, 1, QKV_TILE), lambda b, j: (b, 0, j)),
            pl.BlockSpec((1, 1, 3 * D_MODEL), lambda b, j: (b, 0, 0)),
            pl.BlockSpec((1, D_MODEL), const),
            pl.BlockSpec((n_groups, LANES), const),
            pl.BlockSpec((n_groups, LANES), const),
            pl.BlockSpec((8, 1), const),
            pl.BlockSpec((64, 3 * LANES), const),
            pl.BlockSpec((2 * LANES, 2 * LANES), const),
        ] + w_specs,
        out_specs=out_specs,
        out_shape=out_shapes,
        scratch_shapes=[pltpu.VMEM((ATTN_WIDTH // LANES, QKV_TILE, LANES), F32),
                        pltpu.VMEM((D_MODEL, QKV_COLS), BF16)],
        compiler_params=pltpu.CompilerParams(
            dimension_semantics=("arbitrary", "arbitrary"), vmem_limit_bytes=VMEM_LIMIT_BYTES),
        name="qkv",
    )(x, pos3, ada3, norm_g2, gq, gk, freq, e_mat, bd_mat, *([w_in] * n_chunks))


def _attn_kernel(q_ref, k_ref, v_ref, o_ref, lse_ref, *scratch, dilation, n_blk):
    row = lax.broadcasted_iota(jnp.int32, (Q_BLOCK, LANES), 0)
    col = lax.broadcasted_iota(jnp.int32, (Q_BLOCK, LANES), 1)
    cur_ok = col <= row
    prev_ok = col >= row
    first_ok = jnp.concatenate([cur_ok, cur_ok], axis=0)
    band_one = jnp.concatenate([prev_ok, cur_ok], axis=1)
    band_ok = jnp.concatenate([band_one, band_one], axis=0)
    lane_lo = col < HEAD_DIM
    p_scr, m_scr = scratch[0], scratch[1]
    n_pairs = ATTN_WIDTH // LANES

    def scores(r, r0, first):
        for hp in range(n_pairs):
            lanes = slice(hp * LANES, (hp + 1) * LANES)
            q_pair = q_ref[0, r, pl.ds(r0, Q_BLOCK), lanes]
            zero = jnp.zeros_like(q_pair)
            q2 = jnp.concatenate([jnp.where(lane_lo, q_pair, zero), jnp.where(lane_lo, zero, q_pair)], axis=0)
            if first:
                k_win, ok = k_ref[0, r, pl.ds(r0, Q_BLOCK), lanes], first_ok
            else:
                k_win, ok = k_ref[0, r, pl.ds(r0 - Q_BLOCK, 2 * Q_BLOCK), lanes], band_ok
            s = lax.dot_general(q2, k_win, (((1,), (1,)), ((), ())), preferred_element_type=F32)
            s = jnp.where(ok, s, -jnp.inf)
            m = jnp.max(s, axis=-1, keepdims=True)
            p_scr[hp, :, 0:s.shape[1]] = jnp.exp2(s - m).astype(BF16)
            m_scr[hp] = jnp.where(lane_lo, m[:Q_BLOCK], m[Q_BLOCK:]) * math.log(2.0)

    def values(r, r0, first):
        for hp in range(n_pairs):
            lanes = slice(hp * LANES, (hp + 1) * LANES)
            if first:
                p, v_win = p_scr[hp, :, 0:Q_BLOCK], v_ref[0, r, pl.ds(r0, Q_BLOCK), lanes]
            else:
                p, v_win = p_scr[hp], v_ref[0, r, pl.ds(r0 - Q_BLOCK, 2 * Q_BLOCK), lanes]
            v_ext = jnp.concatenate([v_win, jnp.ones_like(v_win)], axis=1)
            o2 = _dot(p, v_ext)
            num = jnp.where(lane_lo, o2[:Q_BLOCK, :LANES], o2[Q_BLOCK:, :LANES])
            den = jnp.where(lane_lo, o2[:Q_BLOCK, LANES:], o2[Q_BLOCK:, LANES:])
            o_pair = num * (1.0 / den)
            lse_pair = m_scr[hp] + jnp.log(den)
            if dilation == 1:
                o_ref[0, pl.ds(r0, Q_BLOCK), lanes] = o_pair.astype(o_ref.dtype)
                lse_ref[0, pl.ds(r0, Q_BLOCK), lanes] = lse_pair
            else:
                rows = pl.ds(r + r0 * dilation, Q_BLOCK, stride=dilation)
                scratch[2][hp, rows, :] = o_pair
                scratch[3][hp, rows, :] = lse_pair

    def at(blk):
        return pl.multiple_of(blk * Q_BLOCK, Q_BLOCK)

    scores(0, 0, True)
    if n_blk == 1:
        def res_body(r, carry):
            values(r - 1, 0, True)
            scores(r, 0, True)
            return carry
        lax.fori_loop(1, dilation, res_body, 0, unroll=BLOCK_UNROLL)
        values(dilation - 1, 0, True)
    else:
        for r in range(dilation):
            values(r, 0, True)
            scores(r, Q_BLOCK, False)

            def blk_body(blk, carry, r=r):
                values(r, at(blk - 1), False)
                scores(r, at(blk), False)
                return carry
            lax.fori_loop(2, n_blk, blk_body, 0, unroll=BLOCK_UNROLL)
            values(r, (n_blk - 1) * Q_BLOCK, False)
            if r + 1 < dilation:
                scores(r + 1, 0, True)

    if dilation > 1:
        for hp in range(n_pairs):
            lanes = slice(hp * LANES, (hp + 1) * LANES)
            o_ref[0, :, lanes] = scratch[2][hp].astype(o_ref.dtype)
            lse_ref[0, :, lanes] = scratch[3][hp]


def _attn_call(g, dilation, q, k, v):
    batch, _, res_len, _ = q.shape
    seq = dilation * res_len
    in_spec = pl.BlockSpec((1, dilation, res_len, ATTN_WIDTH), lambda b: (b, 0, 0, 0))
    out_spec = pl.BlockSpec((1, seq, ATTN_WIDTH), lambda b: (b, 0, 0))
    plane = pltpu.VMEM((ATTN_WIDTH // LANES, seq, LANES), F32)
    n_pairs = ATTN_WIDTH // LANES
    scratch = [pltpu.VMEM((n_pairs, 2 * Q_BLOCK, 2 * Q_BLOCK), BF16),
               pltpu.VMEM((n_pairs, Q_BLOCK, LANES), F32)]
    if dilation > 1:
        scratch += [plane, plane]
    return pl.pallas_call(
        functools.partial(_attn_kernel, dilation=dilation, n_blk=res_len // Q_BLOCK),
        grid=(batch,),
        in_specs=[in_spec, in_spec, in_spec],
        out_specs=[out_spec, out_spec],
        out_shape=[jax.ShapeDtypeStruct((batch, seq, ATTN_WIDTH), BF16),
                   jax.ShapeDtypeStruct((batch, seq, ATTN_WIDTH), F32)],
        scratch_shapes=scratch,
        compiler_params=pltpu.CompilerParams(
            dimension_semantics=("arbitrary",), vmem_limit_bytes=VMEM_LIMIT_BYTES),
        name=f"attn_g{g}",
    )(q, k, v)


def _merge_kernel(x_ref, ada_ref, norm_g_ref, w_ref,
                  o0_ref, o1_ref, o2_ref, l0_ref, l1_ref, l2_ref,
                  ln_g_ref, ln_b_ref, ws_ref, bias_ref, wa_ref, wb_ref, wo_ref, out_ref):
    xf = x_ref[0]
    tm = xf.shape[0]
    h = _ada_ln(xf, ada_ref, norm_g_ref)
    z = _dot(h.astype(BF16), w_ref[...])
    off = 0
    gate_a = z[:, off:off + ATTN_WIDTH]; off += ATTN_WIDTH
    u = _gelu_exact(z[:, off:off + GMLP_WIDTH]); off += GMLP_WIDTH
    v = _gelu_exact(z[:, off:off + GMLP_WIDTH]); off += GMLP_WIDTH
    gate_b = z[:, off:off + GMLP_WIDTH]; off += GMLP_WIDTH
    merge_a = z[:, off:off + D_MODEL]; off += D_MODEL
    merge_b = z[:, off:off + D_MODEL]

    l0, l1, l2 = l0_ref[0], l1_ref[0], l2_ref[0]
    lmax = jnp.maximum(jnp.maximum(l0, l1), l2)
    e0, e1, e2 = jnp.exp(l0 - lmax), jnp.exp(l1 - lmax), jnp.exp(l2 - lmax)
    attn = (e0 * o0_ref[0].astype(F32) + e1 * o1_ref[0].astype(F32) + e2 * o2_ref[0].astype(F32)) \
        / (e0 + e1 + e2)
    y_a = attn * _silu(gate_a)

    mu = jnp.mean(v, axis=-1, keepdims=True)
    vc = v - mu
    var = jnp.mean(vc * vc, axis=-1, keepdims=True)
    v_ln = (vc * lax.rsqrt(var + EPS) * ln_g_ref[...] + ln_b_ref[...]).astype(BF16)
    row = lax.broadcasted_iota(jnp.int32, (CHUNK, CHUNK), 0)
    col = lax.broadcasted_iota(jnp.int32, (CHUNK, CHUNK), 1)
    causal = row >= col
    lane_lo = col < GMLP_GROUP_DIM
    n_chunks = tm // CHUNK
    sv_cols = []
    for gp in range(GMLP_WIDTH // LANES):
        lanes = slice(gp * LANES, (gp + 1) * LANES)
        rhs = jnp.concatenate([v_ln[c * CHUNK:(c + 1) * CHUNK, lanes] for c in range(n_chunks)], axis=1)
        res = []
        for half in range(2):
            w_s = jnp.where(causal, ws_ref[2 * gp + half], 0.0).astype(BF16)
            res.append(_dot(w_s, rhs))
        sel = jnp.concatenate(
            [jnp.where(lane_lo, res[0][:, c * LANES:(c + 1) * LANES], res[1][:, c * LANES:(c + 1) * LANES])
             + bias_ref[:, lanes] for c in range(n_chunks)], axis=0)
        sv_cols.append(sel)
    sv = jnp.concatenate(sv_cols, axis=1)
    y_b = u * sv * _silu(gate_b)

    merged = (_sigmoid(merge_a) * _dot(y_a.astype(BF16), wa_ref[...])
              + _sigmoid(merge_b) * _dot(y_b.astype(BF16), wb_ref[...]))
    out = _dot(merged.astype(BF16), wo_ref[...])
    gate = ada_ref[0, :, 2 * D_MODEL:3 * D_MODEL]
    out_ref[0] = xf + gate * out


def _merge_call(x, ada3, norm_g2, w_in_bf16, os_, lses, ln_g, ln_b, w_spatial, bias_tab, wa, wb, wo):
    batch, seq, _ = x.shape
    const2 = lambda b, i: (0, 0)
    tile = lambda width: pl.BlockSpec((1, MERGE_TILE, width), lambda b, i: (b, i, 0))
    return pl.pallas_call(
        _merge_kernel,
        grid=(batch, seq // MERGE_TILE),
        in_specs=[
            tile(D_MODEL),
            pl.BlockSpec((1, 1, 3 * D_MODEL), lambda b, i: (b, 0, 0)),
            pl.BlockSpec((1, D_MODEL), const2),
            _resident((D_MODEL, REST_COLS)),
            tile(ATTN_WIDTH), tile(ATTN_WIDTH), tile(ATTN_WIDTH),
            tile(ATTN_WIDTH), tile(ATTN_WIDTH), tile(ATTN_WIDTH),
            pl.BlockSpec((1, GMLP_WIDTH), const2),
            pl.BlockSpec((1, GMLP_WIDTH), const2),
            pl.BlockSpec((GMLP_GROUPS, CHUNK, CHUNK), lambda b, i: (0, 0, 0)),
            pl.BlockSpec((CHUNK, GMLP_WIDTH), const2),
            _resident((ATTN_WIDTH, D_MODEL)),
            _resident((GMLP_WIDTH, D_MODEL)),
            _resident((D_MODEL, D_MODEL)),
        ],
        out_specs=tile(D_MODEL),
        out_shape=jax.ShapeDtypeStruct(x.shape, x.dtype),
        compiler_params=pltpu.CompilerParams(
            dimension_semantics=("arbitrary", "arbitrary"), vmem_limit_bytes=VMEM_LIMIT_BYTES),
        name="merge",
    )(x, ada3, norm_g2, w_in_bf16, *os_, *lses, ln_g, ln_b, w_spatial, bias_tab, wa, wb, wo)


def kernel(x, c, positions, norm_g, w_ada, b_ada, w_in, q_norm_g, k_norm_g, sgu_ln_g, sgu_ln_b,
           w_spatial, b_spatial, w_branch_a, w_branch_b, w_out):
    batch, seq, d_model = x.shape
    assert d_model == D_MODEL and seq % QKV_TILE == 0 and seq % MERGE_TILE == 0
    assert w_in.shape == (D_MODEL, QKV_COLS + REST_COLS)

    w_rest_bf16 = w_in[:, QKV_COLS:].astype(BF16)
    ada = _ada_call(c, w_ada, b_ada)
    ada3 = ada.reshape(batch, 1, 3 * D_MODEL)
    norm_g2 = norm_g.reshape(1, D_MODEL)

    freq = (ROPE_THETA ** (-np.arange(0, ROPE_DIMS, 2, dtype=np.float32) / ROPE_DIMS)).astype(np.float32)
    freq = jnp.asarray(freq.reshape(ROPE_HALF, 1))
    e_mat = jnp.asarray(_rope_expand_matrix(), dtype=BF16)
    bd_mat = jnp.asarray(_head_mean_matrix(), dtype=BF16)

    gq = jnp.tile(q_norm_g, (1, LANES // HEAD_DIM))
    gk = jnp.tile(k_norm_g, (1, LANES // HEAD_DIM))
    qkv = _qkv_call(x, positions.reshape(batch, 1, seq), ada3, norm_g2, w_in, gq, gk,
                    freq, e_mat, bd_mat)
    outs, lses = [], []
    for g, (window, dilation) in enumerate(ATTN_GROUPS):
        assert window // dilation == Q_BLOCK
        o, lse = _attn_call(g, dilation, *qkv[3 * g:3 * g + 3])
        outs.append(o)
        lses.append(lse)

    bias_tab = jnp.repeat(b_spatial.T, GMLP_GROUP_DIM, axis=1)
    return _merge_call(x, ada3, norm_g2, w_rest_bf16, outs, lses,
                       sgu_ln_g.reshape(1, GMLP_WIDTH), sgu_ln_b.reshape(1, GMLP_WIDTH),
                       w_spatial, bias_tab,
                       w_branch_a.astype(BF16), w_branch_b.astype(BF16), w_out.astype(BF16))
```

```python
import functools
import math

import jax
import jax.numpy as jnp
import numpy as np
from jax import lax
from jax.experimental import pallas as pl
from jax.experimental.pallas import tpu as pltpu

D_MODEL = 1024
HEAD_DIM = 64
HEADS = 8
ATTN_GROUPS = ((128, 1), (512, 4), (2048, 16))
ATTN_WIDTH = HEADS * HEAD_DIM
Q_BLOCK = 128
ROPE_THETA = 500000.0
ROPE_DIMS = HEAD_DIM // 4
ROPE_HALF = ROPE_DIMS // 2
GMLP_WIDTH = 512
GMLP_GROUPS = 8
GMLP_GROUP_DIM = GMLP_WIDTH // GMLP_GROUPS
CHUNK = 128
EPS = 1e-6
GROUP_COLS = 3 * ATTN_WIDTH
QKV_COLS = len(ATTN_GROUPS) * GROUP_COLS
REST_COLS = ATTN_WIDTH + 3 * GMLP_WIDTH + 2 * D_MODEL

Q_SCALE = math.log2(math.e) / math.sqrt(HEAD_DIM)

LANES = 128
QKV_TILE = 512
ROW_BLOCK = 64
QKV_SUB = 512
MERGE_TILE = 512
BLOCK_UNROLL = 4
VMEM_LIMIT_BYTES = 56 * 1024 * 1024

F32 = jnp.float32
BF16 = jnp.bfloat16


def _dot(a, b):
    return jnp.dot(a, b, preferred_element_type=F32)


def _resident(shape):
    return pl.BlockSpec(shape, lambda *_: (0, 0), pipeline_mode=pl.Buffered(1))


def _silu(v):
    return v * (1.0 / (1.0 + jnp.exp(-v)))


def _sigmoid(v):
    return 1.0 / (1.0 + jnp.exp(-v))


def _gelu_exact(v):
    return 0.5 * v * (1.0 + lax.erf(v * (1.0 / math.sqrt(2.0))))


def _ada_ln(xf, ada_ref, norm_g_ref):
    ms = jnp.mean(xf * xf, axis=-1, keepdims=True)
    shift = ada_ref[0, :, 0:D_MODEL]
    scale = ada_ref[0, :, D_MODEL:2 * D_MODEL]
    return xf * lax.rsqrt(ms + EPS) * norm_g_ref[...] * (1.0 + scale) + shift


def _ada_kernel(c_ref, w_ref, b_ref, o_ref):
    o_ref[...] = _dot(_silu(c_ref[...]).astype(BF16), w_ref[...].astype(BF16)) + b_ref[...]


def _ada_call(c, w_ada, b_ada):
    batch = c.shape[0]
    return pl.pallas_call(
        _ada_kernel,
        grid=(3,),
        in_specs=[pl.BlockSpec((batch, D_MODEL), lambda n: (0, 0)),
                  pl.BlockSpec((D_MODEL, D_MODEL), lambda n: (0, n)),
                  pl.BlockSpec((1, D_MODEL), lambda n: (0, n))],
        out_specs=pl.BlockSpec((batch, D_MODEL), lambda n: (0, n)),
        out_shape=jax.ShapeDtypeStruct((batch, 3 * D_MODEL), F32),
        compiler_params=pltpu.CompilerParams(dimension_semantics=("arbitrary",)),
        name="ada",
    )(c, w_ada, b_ada.reshape(1, 3 * D_MODEL))


def _rope_expand_matrix():
    e = np.zeros((64, 3 * LANES), np.float32)
    for lane in range(LANES):
        dim = lane % HEAD_DIM
        if dim < ROPE_DIMS:
            j = dim % ROPE_HALF
            for part in range(3):
                e[part * 8 + j, lane] = 1.0
                if dim >= ROPE_HALF:
                    e[24 + part * 8 + j, LANES + lane] = 1.0
                else:
                    e[24 + part * 8 + j, 2 * LANES + lane] = -1.0
        else:
            e[48, lane] = 1.0
    return e


def _head_mean_matrix():
    head = np.arange(2 * LANES) // HEAD_DIM
    return (head[:, None] == head[None, :]).astype(np.float32) / HEAD_DIM


def _split3(a):
    hi = a.astype(BF16).astype(F32)
    r = a - hi
    mid = r.astype(BF16).astype(F32)
    return hi, mid, r - mid


def _store_residue_major(out, val, row0, dilation, perm_scr):
    n, width = val.shape
    m0, n_m = row0 // dilation, n // dilation
    if dilation == 1:
        out[0, 0, m0:m0 + n_m, :] = val.astype(out.dtype)
        return
    for cb in range(width // LANES):
        perm_scr[cb] = val[:, cb * LANES:(cb + 1) * LANES]
    for r in range(dilation):
        for cb in range(width // LANES):
            out[0, r, m0:m0 + n_m, cb * LANES:(cb + 1) * LANES] = (
                perm_scr[cb, pl.ds(r, n_m, stride=dilation), :].astype(out.dtype))


def _qkv_kernel(x_ref, pos_ref, ada_ref, norm_g_ref, gq_ref, gk_ref, freq_ref, e_ref, bd_ref, *refs):
    n_chunks = QKV_COLS // ATTN_WIDTH
    w_chunk_refs, out_refs = refs[:n_chunks], refs[n_chunks:2 * n_chunks]
    perm_scr, w_ref = refs[2 * n_chunks:]

    @pl.when((pl.program_id(0) == 0) & (pl.program_id(1) == 0))
    def _():
        for c, chunk_ref in enumerate(w_chunk_refs):
            w_ref[:, c * ATTN_WIDTH:(c + 1) * ATTN_WIDTH] = chunk_ref[...].astype(BF16)

    n_groups = len(ATTN_GROUPS)
    chunks = ([(g, which) for which in (0, 1) for g in range(n_groups)]
              + [(g, 2) for g in reversed(range(n_groups))])
    for sub_tile in range(QKV_TILE // QKV_SUB):
        row0 = sub_tile * QKV_SUB
        scr = perm_scr.at[sub_tile]
        xf = x_ref[0, row0:row0 + QKV_SUB, :]
        hb = _ada_ln(xf, ada_ref, norm_g_ref).astype(BF16)

        ang = freq_ref[...] * pos_ref[0, :, row0:row0 + QKV_SUB].astype(F32)
        parts = _split3(jnp.cos(ang)) + _split3(jnp.sin(ang))
        lhs_t = jnp.concatenate(
            list(parts) + [jnp.ones((8, QKV_SUB), F32), jnp.zeros((8, QKV_SUB), F32)], axis=0)
        tab = lax.dot_general(lhs_t.astype(BF16), e_ref[...], (((0,), (0,)), ((), ())),
                              preferred_element_type=F32)
        cos_t, s1_t, s2_t = tab[:, 0:LANES], tab[:, LANES:2 * LANES], tab[:, 2 * LANES:3 * LANES]

        for g, which in chunks:
            dilation = ATTN_GROUPS[g][1]
            col0 = g * GROUP_COLS + which * ATTN_WIDTH
            zc = _dot(hb, w_ref[:, col0:col0 + ATTN_WIDTH])
            out = out_refs[3 * g + which]
            if which == 2:
                _store_residue_major(out, zc, row0, dilation, scr)
                continue
            gain = gq_ref[g:g + 1, :] * Q_SCALE if which == 0 else gk_ref[g:g + 1, :]
            sq = (zc * zc).astype(BF16)
            ms = [_dot(sq[:, half * 2 * LANES:(half + 1) * 2 * LANES], bd_ref[...])
                  for half in range(ATTN_WIDTH // (2 * LANES))]
            blocks = []
            for rb in range(QKV_SUB // ROW_BLOCK):
                rows = slice(rb * ROW_BLOCK, (rb + 1) * ROW_BLOCK)
                cols = []
                for cb in range(ATTN_WIDTH // LANES):
                    a = zc[rows, cb * LANES:(cb + 1) * LANES]
                    m = ms[cb // 2][rows, (cb % 2) * LANES:(cb % 2 + 1) * LANES]
                    an = a * lax.rsqrt(m + EPS) * gain
                    cols.append(an * cos_t[rows] + pltpu.roll(an, ROPE_HALF, 1) * s1_t[rows]
                                + pltpu.roll(an, LANES - ROPE_HALF, 1) * s2_t[rows])
                blocks.append(jnp.concatenate(cols, axis=1))
            _store_residue_major(out, jnp.concatenate(blocks, axis=0), row0, dilation, scr)


def _qkv_call(x, pos3, ada3, norm_g2, w_in, gq, gk, freq, e_mat, bd_mat):
    batch, seq, _ = x.shape
    const = lambda b, j: (0, 0)
    n_chunks = QKV_COLS // ATTN_WIDTH
    w_specs = [pl.BlockSpec((D_MODEL, ATTN_WIDTH), functools.partial(lambda c, b, j: (0, c), c),
                            pipeline_mode=pl.Buffered(1)) for c in range(n_chunks)]
    out_specs, out_shapes = [], []
    for _, dilation in ATTN_GROUPS:
        spec = pl.BlockSpec((1, dilation, QKV_TILE // dilation, ATTN_WIDTH), lambda b, j: (b, 0, j, 0))
        sds = jax.ShapeDtypeStruct((batch, dilation, seq // dilation, ATTN_WIDTH), BF16)
        out_specs += [spec] * 3
        out_shapes += [sds] * 3
    n_groups = len(ATTN_GROUPS)
    return pl.pallas_call(
        _qkv_kernel,
        grid=(batch, seq // QKV_TILE),
        in_specs=[
            pl.BlockSpec((1, QKV_TILE, D_MODEL), lambda b, j: (b, j, 0)),
            pl.BlockSpec((1, 1, QKV_TILE), lambda b, j: (b, 0, j)),
            pl.BlockSpec((1, 1, 3 * D_MODEL), lambda b, j: (b, 0, 0)),
            pl.BlockSpec((1, D_MODEL), const),
            pl.BlockSpec((n_groups, LANES), const),
            pl.BlockSpec((n_groups, LANES), const),
            pl.BlockSpec((8, 1), const),
            pl.BlockSpec((64, 3 * LANES), const),
            pl.BlockSpec((2 * LANES, 2 * LANES), const),
        ] + w_specs,
        out_specs=out_specs,
        out_shape=out_shapes,
        scratch_shapes=[pltpu.VMEM((QKV_TILE // QKV_SUB, ATTN_WIDTH // LANES, QKV_SUB, LANES), F32),
                        pltpu.VMEM((D_MODEL, QKV_COLS), BF16)],
        compiler_params=pltpu.CompilerParams(
            dimension_semantics=("arbitrary", "arbitrary"), vmem_limit_bytes=VMEM_LIMIT_BYTES),
        name="qkv",
    )(x, pos3, ada3, norm_g2, gq, gk, freq, e_mat, bd_mat, *([w_in] * n_chunks))


def _attn_kernel(q_ref, k_ref, v_ref, o_ref, lse_ref, *scratch, dilation, n_blk, token_order_out):
    row = lax.broadcasted_iota(jnp.int32, (Q_BLOCK, LANES), 0)
    col = lax.broadcasted_iota(jnp.int32, (Q_BLOCK, LANES), 1)
    cur_ok = col <= row
    prev_ok = col >= row
    first_ok = jnp.concatenate([cur_ok, cur_ok], axis=0)
    band_one = jnp.concatenate([prev_ok, cur_ok], axis=1)
    band_ok = jnp.concatenate([band_one, band_one], axis=0)
    lane_lo = col < HEAD_DIM
    p_scr, m_scr = scratch[0], scratch[1]
    n_pairs = ATTN_WIDTH // LANES

    def scores(r, r0, first):
        for hp in range(n_pairs):
            lanes = slice(hp * LANES, (hp + 1) * LANES)
            q_pair = q_ref[0, r, pl.ds(r0, Q_BLOCK), lanes]
            zero = jnp.zeros_like(q_pair)
            q2 = jnp.concatenate([jnp.where(lane_lo, q_pair, zero), jnp.where(lane_lo, zero, q_pair)], axis=0)
            if first:
                k_win, ok = k_ref[0, r, pl.ds(r0, Q_BLOCK), lanes], first_ok
            else:
                k_win, ok = k_ref[0, r, pl.ds(r0 - Q_BLOCK, 2 * Q_BLOCK), lanes], band_ok
            s = lax.dot_general(q2, k_win, (((1,), (1,)), ((), ())), preferred_element_type=F32)
            s = jnp.where(ok, s, -jnp.inf)
            m = jnp.max(s, axis=-1, keepdims=True)
            p_scr[hp, :, 0:s.shape[1]] = jnp.exp2(s - m).astype(BF16)
            m_scr[hp] = jnp.where(lane_lo, m[:Q_BLOCK], m[Q_BLOCK:]) * math.log(2.0)

    def values(r, r0, first):
        for hp in range(n_pairs):
            lanes = slice(hp * LANES, (hp + 1) * LANES)
            if first:
                p, v_win = p_scr[hp, :, 0:Q_BLOCK], v_ref[0, r, pl.ds(r0, Q_BLOCK), lanes]
            else:
                p, v_win = p_scr[hp], v_ref[0, r, pl.ds(r0 - Q_BLOCK, 2 * Q_BLOCK), lanes]
            v_ext = jnp.concatenate([v_win, jnp.ones_like(v_win)], axis=1)
            o2 = _dot(p, v_ext)
            num = jnp.where(lane_lo, o2[:Q_BLOCK, :LANES], o2[Q_BLOCK:, :LANES])
            den = jnp.where(lane_lo, o2[:Q_BLOCK, LANES:], o2[Q_BLOCK:, LANES:])
            o_pair = num * (1.0 / den)
            lse_pair = m_scr[hp] + jnp.log(den)
            if dilation == 1:
                o_ref[0, pl.ds(r0, Q_BLOCK), lanes] = o_pair.astype(o_ref.dtype)
                lse_ref[0, pl.ds(r0, Q_BLOCK), lanes] = lse_pair
            elif not token_order_out:
                o_ref[0, r, pl.ds(r0, Q_BLOCK), lanes] = o_pair.astype(o_ref.dtype)
                lse_ref[0, r, pl.ds(r0, Q_BLOCK), lanes] = lse_pair
            else:
                rows = pl.ds(r + r0 * dilation, Q_BLOCK, stride=dilation)
                scratch[2][hp, rows, :] = o_pair
                scratch[3][hp, rows, :] = lse_pair

    def at(blk):
        return pl.multiple_of(blk * Q_BLOCK, Q_BLOCK)

    scores(0, 0, True)
    if n_blk == 1:
        def res_body(r, carry):
            values(r - 1, 0, True)
            scores(r, 0, True)
            return carry
        lax.fori_loop(1, dilation, res_body, 0, unroll=BLOCK_UNROLL)
        values(dilation - 1, 0, True)
    else:
        for r in range(dilation):
            values(r, 0, True)
            scores(r, Q_BLOCK, False)

            def blk_body(blk, carry, r=r):
                values(r, at(blk - 1), False)
                scores(r, at(blk), False)
                return carry
            lax.fori_loop(2, n_blk, blk_body, 0, unroll=BLOCK_UNROLL)
            values(r, (n_blk - 1) * Q_BLOCK, False)
            if r + 1 < dilation:
                scores(r + 1, 0, True)

    if dilation > 1 and token_order_out:
        for hp in range(n_pairs):
            lanes = slice(hp * LANES, (hp + 1) * LANES)
            o_ref[0, :, lanes] = scratch[2][hp].astype(o_ref.dtype)
            lse_ref[0, :, lanes] = scratch[3][hp]


def _attn_call(g, dilation, q, k, v, token_order_out):
    batch, _, res_len, _ = q.shape
    seq = dilation * res_len
    in_spec = pl.BlockSpec((1, dilation, res_len, ATTN_WIDTH), lambda b: (b, 0, 0, 0))
    n_pairs = ATTN_WIDTH // LANES
    scratch = [pltpu.VMEM((n_pairs, 2 * Q_BLOCK, 2 * Q_BLOCK), BF16),
               pltpu.VMEM((n_pairs, Q_BLOCK, LANES), F32)]
    if token_order_out or dilation == 1:
        out_spec, out_dims = pl.BlockSpec((1, seq, ATTN_WIDTH), lambda b: (b, 0, 0)), (batch, seq, ATTN_WIDTH)
        if dilation > 1:
            scratch += [pltpu.VMEM((n_pairs, seq, LANES), F32)] * 2
    else:
        out_spec, out_dims = in_spec, q.shape
    return pl.pallas_call(
        functools.partial(_attn_kernel, dilation=dilation, n_blk=res_len // Q_BLOCK,
                          token_order_out=token_order_out),
        grid=(batch,),
        in_specs=[in_spec, in_spec, in_spec],
        out_specs=[out_spec, out_spec],
        out_shape=[jax.ShapeDtypeStruct(out_dims, BF16), jax.ShapeDtypeStruct(out_dims, F32)],
        scratch_shapes=scratch,
        compiler_params=pltpu.CompilerParams(
            dimension_semantics=("arbitrary",), vmem_limit_bytes=VMEM_LIMIT_BYTES),
        name=f"attn_g{g}",
    )(q, k, v)


def _token_order(ref, scr):
    _, dilation, n, width = ref.shape
    cols = []
    for cb in range(width // LANES):
        for r in range(dilation):
            scr[cb, pl.ds(r, n, stride=dilation), :] = ref[0, r, :, cb * LANES:(cb + 1) * LANES].astype(F32)
        cols.append(scr[cb])
    return jnp.concatenate(cols, axis=1)


def _merge_kernel(x_ref, ada_ref, norm_g_ref, w_ref,
                  o0_ref, o1_ref, o2_ref, l0_ref, l1_ref, l2_ref,
                  ln_g_ref, ln_b_ref, ws_ref, bias_ref, wa_ref, wb_ref, wo_ref, out_ref, o2_scr, l2_scr):
    xf = x_ref[0]
    tm = xf.shape[0]
    h = _ada_ln(xf, ada_ref, norm_g_ref)
    z = _dot(h.astype(BF16), w_ref[...])
    off = 0
    gate_a = z[:, off:off + ATTN_WIDTH]; off += ATTN_WIDTH
    u = _gelu_exact(z[:, off:off + GMLP_WIDTH]); off += GMLP_WIDTH
    v = _gelu_exact(z[:, off:off + GMLP_WIDTH]); off += GMLP_WIDTH
    gate_b = z[:, off:off + GMLP_WIDTH]; off += GMLP_WIDTH
    merge_a = z[:, off:off + D_MODEL]; off += D_MODEL
    merge_b = z[:, off:off + D_MODEL]

    l0, l1, l2 = l0_ref[0], l1_ref[0], _token_order(l2_ref, l2_scr)
    lmax = jnp.maximum(jnp.maximum(l0, l1), l2)
    e0, e1, e2 = jnp.exp(l0 - lmax), jnp.exp(l1 - lmax), jnp.exp(l2 - lmax)
    attn = (e0 * o0_ref[0].astype(F32) + e1 * o1_ref[0].astype(F32) + e2 * _token_order(o2_ref, o2_scr)) \
        / (e0 + e1 + e2)
    y_a = attn * _silu(gate_a)

    mu = jnp.mean(v, axis=-1, keepdims=True)
    vc = v - mu
    var = jnp.mean(vc * vc, axis=-1, keepdims=True)
    v_ln = (vc * lax.rsqrt(var + EPS) * ln_g_ref[...] + ln_b_ref[...]).astype(BF16)
    row = lax.broadcasted_iota(jnp.int32, (CHUNK, CHUNK), 0)
    col = lax.broadcasted_iota(jnp.int32, (CHUNK, CHUNK), 1)
    causal = row >= col
    lane_lo = col < GMLP_GROUP_DIM
    n_chunks = tm // CHUNK
    sv_cols = []
    for gp in range(GMLP_WIDTH // LANES):
        lanes = slice(gp * LANES, (gp + 1) * LANES)
        rhs = jnp.concatenate([v_ln[c * CHUNK:(c + 1) * CHUNK, lanes] for c in range(n_chunks)], axis=1)
        res = []
        for half in range(2):
            w_s = jnp.where(causal, ws_ref[2 * gp + half], 0.0).astype(BF16)
            res.append(_dot(w_s, rhs))
        sel = jnp.concatenate(
            [jnp.where(lane_lo, res[0][:, c * LANES:(c + 1) * LANES], res[1][:, c * LANES:(c + 1) * LANES])
             + bias_ref[:, lanes] for c in range(n_chunks)], axis=0)
        sv_cols.append(sel)
    sv = jnp.concatenate(sv_cols, axis=1)
    y_b = u * sv * _silu(gate_b)

    merged = (_sigmoid(merge_a) * _dot(y_a.astype(BF16), wa_ref[...])
              + _sigmoid(merge_b) * _dot(y_b.astype(BF16), wb_ref[...]))
    out = _dot(merged.astype(BF16), wo_ref[...])
    gate = ada_ref[0, :, 2 * D_MODEL:3 * D_MODEL]
    out_ref[0] = xf + gate * out


def _merge_call(x, ada3, norm_g2, w_in_bf16, os_, lses, ln_g, ln_b, w_spatial, bias_tab, wa, wb, wo):
    batch, seq, _ = x.shape
    const2 = lambda b, i: (0, 0)
    tile = lambda width: pl.BlockSpec((1, MERGE_TILE, width), lambda b, i: (b, i, 0))
    last_dilation = ATTN_GROUPS[-1][1]
    res_tile = pl.BlockSpec((1, last_dilation, MERGE_TILE // last_dilation, ATTN_WIDTH),
                            lambda b, i: (b, 0, i, 0))
    staging = pltpu.VMEM((ATTN_WIDTH // LANES, MERGE_TILE, LANES), F32)
    return pl.pallas_call(
        _merge_kernel,
        grid=(batch, seq // MERGE_TILE),
        in_specs=[
            tile(D_MODEL),
            pl.BlockSpec((1, 1, 3 * D_MODEL), lambda b, i: (b, 0, 0)),
            pl.BlockSpec((1, D_MODEL), const2),
            _resident((D_MODEL, REST_COLS)),
            tile(ATTN_WIDTH), tile(ATTN_WIDTH), res_tile,
            tile(ATTN_WIDTH), tile(ATTN_WIDTH), res_tile,
            pl.BlockSpec((1, GMLP_WIDTH), const2),
            pl.BlockSpec((1, GMLP_WIDTH), const2),
            pl.BlockSpec((GMLP_GROUPS, CHUNK, CHUNK), lambda b, i: (0, 0, 0)),
            pl.BlockSpec((CHUNK, GMLP_WIDTH), const2),
            _resident((ATTN_WIDTH, D_MODEL)),
            _resident((GMLP_WIDTH, D_MODEL)),
            _resident((D_MODEL, D_MODEL)),
        ],
        out_specs=tile(D_MODEL),
        out_shape=jax.ShapeDtypeStruct(x.shape, x.dtype),
        scratch_shapes=[staging, staging],
        compiler_params=pltpu.CompilerParams(
            dimension_semantics=("arbitrary", "arbitrary"), vmem_limit_bytes=VMEM_LIMIT_BYTES),
        name="merge",
    )(x, ada3, norm_g2, w_in_bf16, *os_, *lses, ln_g, ln_b, w_spatial, bias_tab, wa, wb, wo)


def kernel(x, c, positions, norm_g, w_ada, b_ada, w_in, q_norm_g, k_norm_g, sgu_ln_g, sgu_ln_b,
           w_spatial, b_spatial, w_branch_a, w_branch_b, w_out):
    batch, seq, d_model = x.shape
    assert d_model == D_MODEL and seq % QKV_TILE == 0 and seq % MERGE_TILE == 0
    assert w_in.shape == (D_MODEL, QKV_COLS + REST_COLS)

    w_rest_bf16 = w_in[:, QKV_COLS:].astype(BF16)
    ada = _ada_call(c, w_ada, b_ada)
    ada3 = ada.reshape(batch, 1, 3 * D_MODEL)
    norm_g2 = norm_g.reshape(1, D_MODEL)

    freq = (ROPE_THETA ** (-np.arange(0, ROPE_DIMS, 2, dtype=np.float32) / ROPE_DIMS)).astype(np.float32)
    freq = jnp.asarray(freq.reshape(ROPE_HALF, 1))
    e_mat = jnp.asarray(_rope_expand_matrix(), dtype=BF16)
    bd_mat = jnp.asarray(_head_mean_matrix(), dtype=BF16)

    gq = jnp.tile(q_norm_g, (1, LANES // HEAD_DIM))
    gk = jnp.tile(k_norm_g, (1, LANES // HEAD_DIM))
    qkv = _qkv_call(x, positions.reshape(batch, 1, seq), ada3, norm_g2, w_in, gq, gk,
                    freq, e_mat, bd_mat)
    outs, lses = [], []
    for g, (window, dilation) in enumerate(ATTN_GROUPS):
        assert window // dilation == Q_BLOCK
        o, lse = _attn_call(g, dilation, *qkv[3 * g:3 * g + 3], token_order_out=g + 1 < len(ATTN_GROUPS))
        outs.append(o)
        lses.append(lse)

    bias_tab = jnp.repeat(b_spatial.T, GMLP_GROUP_DIM, axis=1)
    return _merge_call(x, ada3, norm_g2, w_rest_bf16, outs, lses,
                       sgu_ln_g.reshape(1, GMLP_WIDTH), sgu_ln_b.reshape(1, GMLP_WIDTH),
                       w_spatial, bias_tab,
                       w_branch_a.astype(BF16), w_branch_b.astype(BF16), w_out.astype(BF16))
```

```python
import functools
import math

import jax
import jax.numpy as jnp
import numpy as np
from jax import lax
from jax.experimental import pallas as pl
from jax.experimental.pallas import tpu as pltpu

D_MODEL = 1024
HEAD_DIM = 64
HEADS = 8
ATTN_GROUPS = ((128, 1), (512, 4), (2048, 16))
ATTN_WIDTH = HEADS * HEAD_DIM
Q_BLOCK = 128
ROPE_THETA = 500000.0
ROPE_DIMS = HEAD_DIM // 4
ROPE_HALF = ROPE_DIMS // 2
GMLP_WIDTH = 512
GMLP_GROUPS = 8
GMLP_GROUP_DIM = GMLP_WIDTH // GMLP_GROUPS
CHUNK = 128
EPS = 1e-6
GROUP_COLS = 3 * ATTN_WIDTH
QKV_COLS = len(ATTN_GROUPS) * GROUP_COLS
REST_COLS = ATTN_WIDTH + 3 * GMLP_WIDTH + 2 * D_MODEL

Q_SCALE = math.log2(math.e) / math.sqrt(HEAD_DIM)

LANES = 128
QKV_TILE = 512
ROW_BLOCK = 64
QKV_SUB = 512
MERGE_TILE = 512
BLOCK_UNROLL = 4
VMEM_LIMIT_BYTES = 56 * 1024 * 1024

F32 = jnp.float32
BF16 = jnp.bfloat16


def _dot(a, b):
    return jnp.dot(a, b, preferred_element_type=F32)


def _resident(shape):
    return pl.BlockSpec(shape, lambda *_: (0, 0), pipeline_mode=pl.Buffered(1))


def _silu(v):
    return v * (1.0 / (1.0 + jnp.exp(-v)))


def _sigmoid(v):
    return 1.0 / (1.0 + jnp.exp(-v))


def _gelu_exact(v):
    return 0.5 * v * (1.0 + lax.erf(v * (1.0 / math.sqrt(2.0))))


def _ada_ln(xf, ada_ref, norm_g_ref):
    ms = jnp.mean(xf * xf, axis=-1, keepdims=True)
    shift = ada_ref[0, :, 0:D_MODEL]
    scale = ada_ref[0, :, D_MODEL:2 * D_MODEL]
    return xf * lax.rsqrt(ms + EPS) * norm_g_ref[...] * (1.0 + scale) + shift


def _ada_kernel(c_ref, w_ref, b_ref, o_ref):
    o_ref[...] = _dot(_silu(c_ref[...]).astype(BF16), w_ref[...].astype(BF16)) + b_ref[...]


def _ada_call(c, w_ada, b_ada):
    batch = c.shape[0]
    return pl.pallas_call(
        _ada_kernel,
        grid=(3,),
        in_specs=[pl.BlockSpec((batch, D_MODEL), lambda n: (0, 0)),
                  pl.BlockSpec((D_MODEL, D_MODEL), lambda n: (0, n)),
                  pl.BlockSpec((1, D_MODEL), lambda n: (0, n))],
        out_specs=pl.BlockSpec((batch, D_MODEL), lambda n: (0, n)),
        out_shape=jax.ShapeDtypeStruct((batch, 3 * D_MODEL), F32),
        compiler_params=pltpu.CompilerParams(dimension_semantics=("arbitrary",)),
        name="ada",
    )(c, w_ada, b_ada.reshape(1, 3 * D_MODEL))


def _rope_expand_matrix():
    e = np.zeros((64, 3 * LANES), np.float32)
    for lane in range(LANES):
        dim = lane % HEAD_DIM
        if dim < ROPE_DIMS:
            j = dim % ROPE_HALF
            for part in range(3):
                e[part * 8 + j, lane] = 1.0
                if dim >= ROPE_HALF:
                    e[24 + part * 8 + j, LANES + lane] = 1.0
                else:
                    e[24 + part * 8 + j, 2 * LANES + lane] = -1.0
        else:
            e[48, lane] = 1.0
    return e


def _head_mean_matrix():
    head = np.arange(2 * LANES) // HEAD_DIM
    return (head[:, None] == head[None, :]).astype(np.float32) / HEAD_DIM


def _split3(a):
    hi = a.astype(BF16).astype(F32)
    r = a - hi
    mid = r.astype(BF16).astype(F32)
    return hi, mid, r - mid


def _store_residue_major(out, val, row0, dilation, perm_scr):
    n, width = val.shape
    m0, n_m = row0 // dilation, n // dilation
    if dilation == 1:
        out[0, 0, m0:m0 + n_m, :] = val.astype(out.dtype)
        return
    for cb in range(width // LANES):
        perm_scr[cb] = val[:, cb * LANES:(cb + 1) * LANES]
    for r in range(dilation):
        for cb in range(width // LANES):
            out[0, r, m0:m0 + n_m, cb * LANES:(cb + 1) * LANES] = (
                perm_scr[cb, pl.ds(r, n_m, stride=dilation), :].astype(out.dtype))


def _qkv_kernel(x_ref, pos_ref, ada_ref, norm_g_ref, gq_ref, gk_ref, freq_ref, e_ref, bd_ref, *refs):
    n_chunks = QKV_COLS // ATTN_WIDTH
    w_chunk_refs, out_refs = refs[:n_chunks], refs[n_chunks:2 * n_chunks]
    perm_scr, w_ref = refs[2 * n_chunks:]

    @pl.when((pl.program_id(0) == 0) & (pl.program_id(1) == 0))
    def _():
        for c, chunk_ref in enumerate(w_chunk_refs):
            w_ref[:, c * ATTN_WIDTH:(c + 1) * ATTN_WIDTH] = chunk_ref[...].astype(BF16)

    n_groups = len(ATTN_GROUPS)
    chunks = ([(g, which) for which in (0, 1) for g in range(n_groups)]
              + [(g, 2) for g in reversed(range(n_groups))])
    for sub_tile in range(QKV_TILE // QKV_SUB):
        row0 = sub_tile * QKV_SUB
        scr = perm_scr.at[sub_tile]
        xf = x_ref[0, row0:row0 + QKV_SUB, :]
        hb = _ada_ln(xf, ada_ref, norm_g_ref).astype(BF16)

        ang = freq_ref[...] * pos_ref[0, :, row0:row0 + QKV_SUB].astype(F32)
        parts = _split3(jnp.cos(ang)) + _split3(jnp.sin(ang))
        lhs_t = jnp.concatenate(
            list(parts) + [jnp.ones((8, QKV_SUB), F32), jnp.zeros((8, QKV_SUB), F32)], axis=0)
        tab = lax.dot_general(lhs_t.astype(BF16), e_ref[...], (((0,), (0,)), ((), ())),
                              preferred_element_type=F32)
        cos_t, s1_t, s2_t = tab[:, 0:LANES], tab[:, LANES:2 * LANES], tab[:, 2 * LANES:3 * LANES]

        for g, which in chunks:
            dilation = ATTN_GROUPS[g][1]
            col0 = g * GROUP_COLS + which * ATTN_WIDTH
            zc = _dot(hb, w_ref[:, col0:col0 + ATTN_WIDTH])
            out = out_refs[3 * g + which]
            if which == 2:
                _store_residue_major(out, zc, row0, dilation, scr)
                continue
            gain = gq_ref[g:g + 1, :] * Q_SCALE if which == 0 else gk_ref[g:g + 1, :]
            sq = (zc * zc).astype(BF16)
            ms = [_dot(sq[:, half * 2 * LANES:(half + 1) * 2 * LANES], bd_ref[...])
                  for half in range(ATTN_WIDTH // (2 * LANES))]
            blocks = []
            for rb in range(QKV_SUB // ROW_BLOCK):
                rows = slice(rb * ROW_BLOCK, (rb + 1) * ROW_BLOCK)
                cols = []
                for cb in range(ATTN_WIDTH // LANES):
                    a = zc[rows, cb * LANES:(cb + 1) * LANES]
                    m = ms[cb // 2][rows, (cb % 2) * LANES:(cb % 2 + 1) * LANES]
                    an = a * lax.rsqrt(m + EPS) * gain
                    cols.append(an * cos_t[rows] + pltpu.roll(an, ROPE_HALF, 1) * s1_t[rows]
                                + pltpu.roll(an, LANES - ROPE_HALF, 1) * s2_t[rows])
                blocks.append(jnp.concatenate(cols, axis=1))
            _store_residue_major(out, jnp.concatenate(blocks, axis=0), row0, dilation, scr)


def _qkv_call(x, pos3, ada3, norm_g2, w_in, gq, gk, freq, e_mat, bd_mat):
    batch, seq, _ = x.shape
    const = lambda b, j: (0, 0)
    n_chunks = QKV_COLS // ATTN_WIDTH
    w_specs = [pl.BlockSpec((D_MODEL, ATTN_WIDTH), functools.partial(lambda c, b, j: (0, c), c),
                            pipeline_mode=pl.Buffered(1)) for c in range(n_chunks)]
    out_specs, out_shapes = [], []
    for _, dilation in ATTN_GROUPS:
        spec = pl.BlockSpec((1, dilation, QKV_TILE // dilation, ATTN_WIDTH), lambda b, j: (b, 0, j, 0))
        sds = jax.ShapeDtypeStruct((batch, dilation, seq // dilation, ATTN_WIDTH), BF16)
        out_specs += [spec] * 3
        out_shapes += [sds] * 3
    n_groups = len(ATTN_GROUPS)
    return pl.pallas_call(
        _qkv_kernel,
        grid=(batch, seq // QKV_TILE),
        in_specs=[
            pl.BlockSpec((1, QKV_TILE, D_MODEL), lambda b, j: (b, j, 0)),
            pl.BlockSpec((1, 1, QKV_TILE), lambda b, j: (b, 0, j)),
            pl.BlockSpec((1, 1, 3 * D_MODEL), lambda b, j: (b, 0, 0)),
            pl.BlockSpec((1, D_MODEL), const),
            pl.BlockSpec((n_groups, LANES), const),
            pl.BlockSpec((n_groups, LANES), const),
            pl.BlockSpec((8, 1), const),
            pl.BlockSpec((64, 3 * LANES), const),
            pl.BlockSpec((2 * LANES, 2 * LANES), const),
        ] + w_specs,
        out_specs=out_specs,
        out_shape=out_shapes,
        scratch_shapes=[pltpu.VMEM((QKV_TILE // QKV_SUB, ATTN_WIDTH // LANES, QKV_SUB, LANES), F32),
                        pltpu.VMEM((D_MODEL, QKV_COLS), BF16)],
        compiler_params=pltpu.CompilerParams(
            dimension_semantics=("arbitrary", "arbitrary"), vmem_limit_bytes=VMEM_LIMIT_BYTES),
        name="qkv",
    )(x, pos3, ada3, norm_g2, gq, gk, freq, e_mat, bd_mat, *([w_in] * n_chunks))


def _attn_kernel(q_ref, k_ref, v_ref, *refs, dilation, n_blk, token_order_out, n_cast):
    cast_in, (o_ref, lse_ref) = refs[:n_cast], refs[n_cast:n_cast + 2]
    cast_out, scratch = refs[n_cast + 2:2 * n_cast + 2], refs[2 * n_cast + 2:]
    for src, dst in zip(cast_in, cast_out):
        dst[...] = src[...].astype(dst.dtype)

    row = lax.broadcasted_iota(jnp.int32, (Q_BLOCK, LANES), 0)
    col = lax.broadcasted_iota(jnp.int32, (Q_BLOCK, LANES), 1)
    cur_ok = col <= row
    prev_ok = col >= row
    first_ok = jnp.concatenate([cur_ok, cur_ok], axis=0)
    band_one = jnp.concatenate([prev_ok, cur_ok], axis=1)
    band_ok = jnp.concatenate([band_one, band_one], axis=0)
    lane_lo = col < HEAD_DIM
    p_scr, m_scr = scratch[0], scratch[1]
    n_pairs = ATTN_WIDTH // LANES

    def scores(r, r0, first):
        for hp in range(n_pairs):
            lanes = slice(hp * LANES, (hp + 1) * LANES)
            q_pair = q_ref[0, r, pl.ds(r0, Q_BLOCK), lanes]
            zero = jnp.zeros_like(q_pair)
            q2 = jnp.concatenate([jnp.where(lane_lo, q_pair, zero), jnp.where(lane_lo, zero, q_pair)], axis=0)
            if first:
                k_win, ok = k_ref[0, r, pl.ds(r0, Q_BLOCK), lanes], first_ok
            else:
                k_win, ok = k_ref[0, r, pl.ds(r0 - Q_BLOCK, 2 * Q_BLOCK), lanes], band_ok
            s = lax.dot_general(q2, k_win, (((1,), (1,)), ((), ())), preferred_element_type=F32)
            s = jnp.where(ok, s, -jnp.inf)
            m = jnp.max(s, axis=-1, keepdims=True)
            p_scr[hp, :, 0:s.shape[1]] = jnp.exp2(s - m).astype(BF16)
            m_scr[hp] = jnp.where(lane_lo, m[:Q_BLOCK], m[Q_BLOCK:]) * math.log(2.0)

    def values(r, r0, first):
        for hp in range(n_pairs):
            lanes = slice(hp * LANES, (hp + 1) * LANES)
            if first:
                p, v_win = p_scr[hp, :, 0:Q_BLOCK], v_ref[0, r, pl.ds(r0, Q_BLOCK), lanes]
            else:
                p, v_win = p_scr[hp], v_ref[0, r, pl.ds(r0 - Q_BLOCK, 2 * Q_BLOCK), lanes]
            v_ext = jnp.concatenate([v_win, jnp.ones_like(v_win)], axis=1)
            o2 = _dot(p, v_ext)
            num = jnp.where(lane_lo, o2[:Q_BLOCK, :LANES], o2[Q_BLOCK:, :LANES])
            den = jnp.where(lane_lo, o2[:Q_BLOCK, LANES:], o2[Q_BLOCK:, LANES:])
            o_pair = num * (1.0 / den)
            lse_pair = m_scr[hp] + jnp.log(den)
            if dilation == 1:
                o_ref[0, pl.ds(r0, Q_BLOCK), lanes] = o_pair.astype(o_ref.dtype)
                lse_ref[0, pl.ds(r0, Q_BLOCK), lanes] = lse_pair
            elif not token_order_out:
                o_ref[0, r, pl.ds(r0, Q_BLOCK), lanes] = o_pair.astype(o_ref.dtype)
                lse_ref[0, r, pl.ds(r0, Q_BLOCK), lanes] = lse_pair
            else:
                rows = pl.ds(r + r0 * dilation, Q_BLOCK, stride=dilation)
                scratch[2][hp, rows, :] = o_pair
                scratch[3][hp, rows, :] = lse_pair

    def at(blk):
        return pl.multiple_of(blk * Q_BLOCK, Q_BLOCK)

    scores(0, 0, True)
    if n_blk == 1:
        def res_body(r, carry):
            values(r - 1, 0, True)
            scores(r, 0, True)
            return carry
        lax.fori_loop(1, dilation, res_body, 0, unroll=BLOCK_UNROLL)
        values(dilation - 1, 0, True)
    else:
        for r in range(dilation):
            values(r, 0, True)
            scores(r, Q_BLOCK, False)

            def blk_body(blk, carry, r=r):
                values(r, at(blk - 1), False)
                scores(r, at(blk), False)
                return carry
            lax.fori_loop(2, n_blk, blk_body, 0, unroll=BLOCK_UNROLL)
            values(r, (n_blk - 1) * Q_BLOCK, False)
            if r + 1 < dilation:
                scores(r + 1, 0, True)

    if dilation > 1 and token_order_out:
        for hp in range(n_pairs):
            lanes = slice(hp * LANES, (hp + 1) * LANES)
            o_ref[0, :, lanes] = scratch[2][hp].astype(o_ref.dtype)
            lse_ref[0, :, lanes] = scratch[3][hp]


def _attn_call(g, dilation, q, k, v, token_order_out, casts=()):
    batch, _, res_len, _ = q.shape
    cast_in_specs, cast_out_specs, cast_out_shapes = [], [], []
    for arr, block, first, out_dims in casts:
        axis = 0 if block[0] != out_dims[0] else 1
        assert out_dims[axis] == batch * block[axis]
        def index(b, first=first, axis=axis, shift=0):
            return (b + shift, 0) if axis == 0 else (0, b + shift)
        cast_in_specs.append(pl.BlockSpec(block, functools.partial(index, shift=first)))
        cast_out_specs.append(pl.BlockSpec(block, index))
        cast_out_shapes.append(jax.ShapeDtypeStruct(out_dims, BF16))
    seq = dilation * res_len
    in_spec = pl.BlockSpec((1, dilation, res_len, ATTN_WIDTH), lambda b: (b, 0, 0, 0))
    n_pairs = ATTN_WIDTH // LANES
    scratch = [pltpu.VMEM((n_pairs, 2 * Q_BLOCK, 2 * Q_BLOCK), BF16),
               pltpu.VMEM((n_pairs, Q_BLOCK, LANES), F32)]
    if token_order_out or dilation == 1:
        out_spec, out_dims = pl.BlockSpec((1, seq, ATTN_WIDTH), lambda b: (b, 0, 0)), (batch, seq, ATTN_WIDTH)
        if dilation > 1:
            scratch += [pltpu.VMEM((n_pairs, seq, LANES), F32)] * 2
    else:
        out_spec, out_dims = in_spec, q.shape
    return pl.pallas_call(
        functools.partial(_attn_kernel, dilation=dilation, n_blk=res_len // Q_BLOCK,
                          token_order_out=token_order_out, n_cast=len(casts)),
        grid=(batch,),
        in_specs=[in_spec, in_spec, in_spec] + cast_in_specs,
        out_specs=[out_spec, out_spec] + cast_out_specs,
        out_shape=[jax.ShapeDtypeStruct(out_dims, BF16), jax.ShapeDtypeStruct(out_dims, F32)] + cast_out_shapes,
        scratch_shapes=scratch,
        compiler_params=pltpu.CompilerParams(
            dimension_semantics=("arbitrary",), vmem_limit_bytes=VMEM_LIMIT_BYTES),
        name=f"attn_g{g}",
    )(q, k, v, *[arr for arr, _, _, _ in casts])


def _token_order(ref, scr):
    _, dilation, n, width = ref.shape
    cols = []
    for cb in range(width // LANES):
        for r in range(dilation):
            scr[cb, pl.ds(r, n, stride=dilation), :] = ref[0, r, :, cb * LANES:(cb + 1) * LANES].astype(F32)
        cols.append(scr[cb])
    return jnp.concatenate(cols, axis=1)


def _merge_kernel(x_ref, ada_ref, norm_g_ref, w_ref,
                  o0_ref, o1_ref, o2_ref, l0_ref, l1_ref, l2_ref,
                  ln_g_ref, ln_b_ref, ws_ref, bias_ref, wa_ref, wb_ref, wo_ref, out_ref, o2_scr, l2_scr):
    xf = x_ref[0]
    tm = xf.shape[0]
    h = _ada_ln(xf, ada_ref, norm_g_ref)
    z = _dot(h.astype(BF16), w_ref[...])
    off = 0
    gate_a = z[:, off:off + ATTN_WIDTH]; off += ATTN_WIDTH
    u = _gelu_exact(z[:, off:off + GMLP_WIDTH]); off += GMLP_WIDTH
    v = _gelu_exact(z[:, off:off + GMLP_WIDTH]); off += GMLP_WIDTH
    gate_b = z[:, off:off + GMLP_WIDTH]; off += GMLP_WIDTH
    merge_a = z[:, off:off + D_MODEL]; off += D_MODEL
    merge_b = z[:, off:off + D_MODEL]

    l0, l1, l2 = l0_ref[0], l1_ref[0], _token_order(l2_ref, l2_scr)
    lmax = jnp.maximum(jnp.maximum(l0, l1), l2)
    e0, e1, e2 = jnp.exp(l0 - lmax), jnp.exp(l1 - lmax), jnp.exp(l2 - lmax)
    attn = (e0 * o0_ref[0].astype(F32) + e1 * o1_ref[0].astype(F32) + e2 * _token_order(o2_ref, o2_scr)) \
        / (e0 + e1 + e2)
    y_a = attn * _silu(gate_a)

    mu = jnp.mean(v, axis=-1, keepdims=True)
    vc = v - mu
    var = jnp.mean(vc * vc, axis=-1, keepdims=True)
    v_ln = (vc * lax.rsqrt(var + EPS) * ln_g_ref[...] + ln_b_ref[...]).astype(BF16)
    row = lax.broadcasted_iota(jnp.int32, (CHUNK, CHUNK), 0)
    col = lax.broadcasted_iota(jnp.int32, (CHUNK, CHUNK), 1)
    causal = row >= col
    lane_lo = col < GMLP_GROUP_DIM
    n_chunks = tm // CHUNK
    sv_cols = []
    for gp in range(GMLP_WIDTH // LANES):
        lanes = slice(gp * LANES, (gp + 1) * LANES)
        rhs = jnp.concatenate([v_ln[c * CHUNK:(c + 1) * CHUNK, lanes] for c in range(n_chunks)], axis=1)
        res = []
        for half in range(2):
            w_s = jnp.where(causal, ws_ref[2 * gp + half], 0.0).astype(BF16)
            res.append(_dot(w_s, rhs))
        sel = jnp.concatenate(
            [jnp.where(lane_lo, res[0][:, c * LANES:(c + 1) * LANES], res[1][:, c * LANES:(c + 1) * LANES])
             + bias_ref[:, lanes] for c in range(n_chunks)], axis=0)
        sv_cols.append(sel)
    sv = jnp.concatenate(sv_cols, axis=1)
    y_b = u * sv * _silu(gate_b)

    merged = (_sigmoid(merge_a) * _dot(y_a.astype(BF16), wa_ref[...])
              + _sigmoid(merge_b) * _dot(y_b.astype(BF16), wb_ref[...]))
    out = _dot(merged.astype(BF16), wo_ref[...])
    gate = ada_ref[0, :, 2 * D_MODEL:3 * D_MODEL]
    out_ref[0] = xf + gate * out


def _merge_call(x, ada3, norm_g2, w_in_bf16, os_, lses, ln_g, ln_b, w_spatial, bias_tab, wa, wb, wo):
    batch, seq, _ = x.shape
    const2 = lambda b, i: (0, 0)
    tile = lambda width: pl.BlockSpec((1, MERGE_TILE, width), lambda b, i: (b, i, 0))
    last_dilation = ATTN_GROUPS[-1][1]
    res_tile = pl.BlockSpec((1, last_dilation, MERGE_TILE // last_dilation, ATTN_WIDTH),
                            lambda b, i: (b, 0, i, 0))
    staging = pltpu.VMEM((ATTN_WIDTH // LANES, MERGE_TILE, LANES), F32)
    return pl.pallas_call(
        _merge_kernel,
        grid=(batch, seq // MERGE_TILE),
        in_specs=[
            tile(D_MODEL),
            pl.BlockSpec((1, 1, 3 * D_MODEL), lambda b, i: (b, 0, 0)),
            pl.BlockSpec((1, D_MODEL), const2),
            _resident((D_MODEL, REST_COLS)),
            tile(ATTN_WIDTH), tile(ATTN_WIDTH), res_tile,
            tile(ATTN_WIDTH), tile(ATTN_WIDTH), res_tile,
            pl.BlockSpec((1, GMLP_WIDTH), const2),
            pl.BlockSpec((1, GMLP_WIDTH), const2),
            pl.BlockSpec((GMLP_GROUPS, CHUNK, CHUNK), lambda b, i: (0, 0, 0)),
            pl.BlockSpec((CHUNK, GMLP_WIDTH), const2),
            _resident((ATTN_WIDTH, D_MODEL)),
            _resident((GMLP_WIDTH, D_MODEL)),
            _resident((D_MODEL, D_MODEL)),
        ],
        out_specs=tile(D_MODEL),
        out_shape=jax.ShapeDtypeStruct(x.shape, x.dtype),
        scratch_shapes=[staging, staging],
        compiler_params=pltpu.CompilerParams(
            dimension_semantics=("arbitrary", "arbitrary"), vmem_limit_bytes=VMEM_LIMIT_BYTES),
        name="merge",
    )(x, ada3, norm_g2, w_in_bf16, *os_, *lses, ln_g, ln_b, w_spatial, bias_tab, wa, wb, wo)


def kernel(x, c, positions, norm_g, w_ada, b_ada, w_in, q_norm_g, k_norm_g, sgu_ln_g, sgu_ln_b,
           w_spatial, b_spatial, w_branch_a, w_branch_b, w_out):
    batch, seq, d_model = x.shape
    assert d_model == D_MODEL and seq % QKV_TILE == 0 and seq % MERGE_TILE == 0
    assert w_in.shape == (D_MODEL, QKV_COLS + REST_COLS)

    ada = _ada_call(c, w_ada, b_ada)
    ada3 = ada.reshape(batch, 1, 3 * D_MODEL)
    norm_g2 = norm_g.reshape(1, D_MODEL)

    freq = (ROPE_THETA ** (-np.arange(0, ROPE_DIMS, 2, dtype=np.float32) / ROPE_DIMS)).astype(np.float32)
    freq = jnp.asarray(freq.reshape(ROPE_HALF, 1))
    e_mat = jnp.asarray(_rope_expand_matrix(), dtype=BF16)
    bd_mat = jnp.asarray(_head_mean_matrix(), dtype=BF16)

    gq = jnp.tile(q_norm_g, (1, LANES // HEAD_DIM))
    gk = jnp.tile(k_norm_g, (1, LANES // HEAD_DIM))
    qkv = _qkv_call(x, positions.reshape(batch, 1, seq), ada3, norm_g2, w_in, gq, gk,
                    freq, e_mat, bd_mat)
    rest_block = REST_COLS // batch
    assert QKV_COLS % rest_block == 0
    casts = [
        [(w_in, (D_MODEL, rest_block), QKV_COLS // rest_block, (D_MODEL, REST_COLS))],
        [(w_branch_a, (ATTN_WIDTH // batch, D_MODEL), 0, w_branch_a.shape),
         (w_branch_b, (GMLP_WIDTH // batch, D_MODEL), 0, w_branch_b.shape),
         (w_out, (D_MODEL // batch, D_MODEL), 0, w_out.shape)],
        [],
    ]
    outs, lses, weights = [], [], []
    for g, (window, dilation) in enumerate(ATTN_GROUPS):
        assert window // dilation == Q_BLOCK
        o, lse, *w_bf16 = _attn_call(g, dilation, *qkv[3 * g:3 * g + 3],
                                     token_order_out=g + 1 < len(ATTN_GROUPS), casts=casts[g])
        outs.append(o)
        lses.append(lse)
        weights += w_bf16
    w_rest_bf16, wa_bf16, wb_bf16, wo_bf16 = weights

    bias_tab = jnp.repeat(b_spatial.T, GMLP_GROUP_DIM, axis=1)
    return _merge_call(x, ada3, norm_g2, w_rest_bf16, outs, lses,
                       sgu_ln_g.reshape(1, GMLP_WIDTH), sgu_ln_b.reshape(1, GMLP_WIDTH),
                       w_spatial, bias_tab, wa_bf16, wb_bf16, wo_bf16)
```

```python
import functools
import math

import jax
import jax.numpy as jnp
import numpy as np
from jax import lax
from jax.experimental import pallas as pl
from jax.experimental.pallas import tpu as pltpu

D_MODEL = 1024
HEAD_DIM = 64
HEADS = 8
ATTN_GROUPS = ((128, 1), (512, 4), (2048, 16))
ATTN_WIDTH = HEADS * HEAD_DIM
Q_BLOCK = 128
ROPE_THETA = 500000.0
ROPE_DIMS = HEAD_DIM // 4
ROPE_HALF = ROPE_DIMS // 2
GMLP_WIDTH = 512
GMLP_GROUPS = 8
GMLP_GROUP_DIM = GMLP_WIDTH // GMLP_GROUPS
CHUNK = 128
EPS = 1e-6
GROUP_COLS = 3 * ATTN_WIDTH
QKV_COLS = len(ATTN_GROUPS) * GROUP_COLS
REST_COLS = ATTN_WIDTH + 3 * GMLP_WIDTH + 2 * D_MODEL

Q_SCALE = math.log2(math.e) / math.sqrt(HEAD_DIM)

LANES = 128
QKV_TILE = 512
ROW_BLOCK = 64
QKV_SUB = 512
MERGE_TILE = 512
BLOCK_UNROLL = 8
VMEM_LIMIT_BYTES = 56 * 1024 * 1024

F32 = jnp.float32
BF16 = jnp.bfloat16


def _dot(a, b):
    return jnp.dot(a, b, preferred_element_type=F32)


def _resident(shape):
    return pl.BlockSpec(shape, lambda *_: (0, 0), pipeline_mode=pl.Buffered(1))


def _silu(v):
    return v * (1.0 / (1.0 + jnp.exp(-v)))


def _sigmoid(v):
    return 1.0 / (1.0 + jnp.exp(-v))


def _gelu_exact(v):
    return 0.5 * v * (1.0 + lax.erf(v * (1.0 / math.sqrt(2.0))))


def _ada_ln(xf, ada_ref, norm_g_ref):
    ms = jnp.mean(xf * xf, axis=-1, keepdims=True)
    shift = ada_ref[0, :, 0:D_MODEL]
    scale = ada_ref[0, :, D_MODEL:2 * D_MODEL]
    return xf * lax.rsqrt(ms + EPS) * norm_g_ref[...] * (1.0 + scale) + shift


def _ada_kernel(c_ref, w_ref, b_ref, o_ref):
    o_ref[...] = _dot(_silu(c_ref[...]).astype(BF16), w_ref[...].astype(BF16)) + b_ref[...]


def _ada_call(c, w_ada, b_ada):
    batch = c.shape[0]
    return pl.pallas_call(
        _ada_kernel,
        grid=(3,),
        in_specs=[pl.BlockSpec((batch, D_MODEL), lambda n: (0, 0)),
                  pl.BlockSpec((D_MODEL, D_MODEL), lambda n: (0, n)),
                  pl.BlockSpec((1, D_MODEL), lambda n: (0, n))],
        out_specs=pl.BlockSpec((batch, D_MODEL), lambda n: (0, n)),
        out_shape=jax.ShapeDtypeStruct((batch, 3 * D_MODEL), F32),
        compiler_params=pltpu.CompilerParams(dimension_semantics=("arbitrary",)),
        name="ada",
    )(c, w_ada, b_ada.reshape(1, 3 * D_MODEL))


def _rope_expand_matrix():
    e = np.zeros((64, 3 * LANES), np.float32)
    for lane in range(LANES):
        dim = lane % HEAD_DIM
        if dim < ROPE_DIMS:
            j = dim % ROPE_HALF
            for part in range(3):
                e[part * 8 + j, lane] = 1.0
                if dim >= ROPE_HALF:
                    e[24 + part * 8 + j, LANES + lane] = 1.0
                else:
                    e[24 + part * 8 + j, 2 * LANES + lane] = -1.0
        else:
            e[48, lane] = 1.0
    return e


def _head_mean_matrix():
    head = np.arange(2 * LANES) // HEAD_DIM
    return (head[:, None] == head[None, :]).astype(np.float32) / HEAD_DIM


def _split3(a):
    hi = a.astype(BF16).astype(F32)
    r = a - hi
    mid = r.astype(BF16).astype(F32)
    return hi, mid, r - mid


def _store_residue_major(out, val, row0, dilation, perm_scr):
    n, width = val.shape
    m0, n_m = row0 // dilation, n // dilation
    if dilation == 1:
        out[0, 0, m0:m0 + n_m, :] = val.astype(out.dtype)
        return
    for cb in range(width // LANES):
        perm_scr[cb] = val[:, cb * LANES:(cb + 1) * LANES]
    for r in range(dilation):
        for cb in range(width // LANES):
            out[0, r, m0:m0 + n_m, cb * LANES:(cb + 1) * LANES] = (
                perm_scr[cb, pl.ds(r, n_m, stride=dilation), :].astype(out.dtype))


def _qkv_kernel(x_ref, pos_ref, ada_ref, norm_g_ref, gq_ref, gk_ref, freq_ref, e_ref, bd_ref, *refs):
    n_chunks = QKV_COLS // ATTN_WIDTH
    w_chunk_refs, out_refs = refs[:n_chunks], refs[n_chunks:2 * n_chunks]
    perm_scr, w_ref = refs[2 * n_chunks:]

    @pl.when((pl.program_id(0) == 0) & (pl.program_id(1) == 0))
    def _():
        for c, chunk_ref in enumerate(w_chunk_refs):
            w_ref[:, c * ATTN_WIDTH:(c + 1) * ATTN_WIDTH] = chunk_ref[...].astype(BF16)

    n_groups = len(ATTN_GROUPS)
    chunks = ([(g, which) for which in (0, 1) for g in range(n_groups)]
              + [(g, 2) for g in reversed(range(n_groups))])
    for sub_tile in range(QKV_TILE // QKV_SUB):
        row0 = sub_tile * QKV_SUB
        scr = perm_scr.at[sub_tile]
        xf = x_ref[0, row0:row0 + QKV_SUB, :]
        hb = _ada_ln(xf, ada_ref, norm_g_ref).astype(BF16)

        ang = freq_ref[...] * pos_ref[0, :, row0:row0 + QKV_SUB].astype(F32)
        parts = _split3(jnp.cos(ang)) + _split3(jnp.sin(ang))
        lhs_t = jnp.concatenate(
            list(parts) + [jnp.ones((8, QKV_SUB), F32), jnp.zeros((8, QKV_SUB), F32)], axis=0)
        tab = lax.dot_general(lhs_t.astype(BF16), e_ref[...], (((0,), (0,)), ((), ())),
                              preferred_element_type=F32)
        cos_t, s1_t, s2_t = tab[:, 0:LANES], tab[:, LANES:2 * LANES], tab[:, 2 * LANES:3 * LANES]

        for g, which in chunks:
            dilation = ATTN_GROUPS[g][1]
            col0 = g * GROUP_COLS + which * ATTN_WIDTH
            zc = _dot(hb, w_ref[:, col0:col0 + ATTN_WIDTH])
            out = out_refs[3 * g + which]
            if which == 2:
                _store_residue_major(out, zc, row0, dilation, scr)
                continue
            gain = gq_ref[g:g + 1, :] * Q_SCALE if which == 0 else gk_ref[g:g + 1, :]
            sq = (zc * zc).astype(BF16)
            ms = [_dot(sq[:, half * 2 * LANES:(half + 1) * 2 * LANES], bd_ref[...])
                  for half in range(ATTN_WIDTH // (2 * LANES))]
            blocks = []
            for rb in range(QKV_SUB // ROW_BLOCK):
                rows = slice(rb * ROW_BLOCK, (rb + 1) * ROW_BLOCK)
                cols = []
                for cb in range(ATTN_WIDTH // LANES):
                    a = zc[rows, cb * LANES:(cb + 1) * LANES]
                    m = ms[cb // 2][rows, (cb % 2) * LANES:(cb % 2 + 1) * LANES]
                    an = a * lax.rsqrt(m + EPS) * gain
                    cols.append(an * cos_t[rows] + pltpu.roll(an, ROPE_HALF, 1) * s1_t[rows]
                                + pltpu.roll(an, LANES - ROPE_HALF, 1) * s2_t[rows])
                blocks.append(jnp.concatenate(cols, axis=1))
            _store_residue_major(out, jnp.concatenate(blocks, axis=0), row0, dilation, scr)


def _qkv_call(x, pos3, ada3, norm_g2, w_in, gq, gk, freq, e_mat, bd_mat):
    batch, seq, _ = x.shape
    const = lambda b, j: (0, 0)
    n_chunks = QKV_COLS // ATTN_WIDTH
    w_specs = [pl.BlockSpec((D_MODEL, ATTN_WIDTH), functools.partial(lambda c, b, j: (0, c), c),
                            pipeline_mode=pl.Buffered(1)) for c in range(n_chunks)]
    out_specs, out_shapes = [], []
    for _, dilation in ATTN_GROUPS:
        spec = pl.BlockSpec((1, dilation, QKV_TILE // dilation, ATTN_WIDTH), lambda b, j: (b, 0, j, 0))
        sds = jax.ShapeDtypeStruct((batch, dilation, seq // dilation, ATTN_WIDTH), BF16)
        out_specs += [spec] * 3
        out_shapes += [sds] * 3
    n_groups = len(ATTN_GROUPS)
    return pl.pallas_call(
        _qkv_kernel,
        grid=(batch, seq // QKV_TILE),
        in_specs=[
            pl.BlockSpec((1, QKV_TILE, D_MODEL), lambda b, j: (b, j, 0)),
            pl.BlockSpec((1, 1, QKV_TILE), lambda b, j: (b, 0, j)),
            pl.BlockSpec((1, 1, 3 * D_MODEL), lambda b, j: (b, 0, 0)),
            pl.BlockSpec((1, D_MODEL), const),
            pl.BlockSpec((n_groups, LANES), const),
            pl.BlockSpec((n_groups, LANES), const),
            pl.BlockSpec((8, 1), const),
            pl.BlockSpec((64, 3 * LANES), const),
            pl.BlockSpec((2 * LANES, 2 * LANES), const),
        ] + w_specs,
        out_specs=out_specs,
        out_shape=out_shapes,
        scratch_shapes=[pltpu.VMEM((QKV_TILE // QKV_SUB, ATTN_WIDTH // LANES, QKV_SUB, LANES), F32),
                        pltpu.VMEM((D_MODEL, QKV_COLS), BF16)],
        compiler_params=pltpu.CompilerParams(
            dimension_semantics=("arbitrary", "arbitrary"), vmem_limit_bytes=VMEM_LIMIT_BYTES),
        name="qkv",
    )(x, pos3, ada3, norm_g2, gq, gk, freq, e_mat, bd_mat, *([w_in] * n_chunks))


def _attn_kernel(q_ref, k_ref, v_ref, *refs, dilation, n_blk, token_order_out, n_cast):
    cast_in, (o_ref, lse_ref) = refs[:n_cast], refs[n_cast:n_cast + 2]
    cast_out, scratch = refs[n_cast + 2:2 * n_cast + 2], refs[2 * n_cast + 2:]
    for src, dst in zip(cast_in, cast_out):
        dst[...] = src[...].astype(dst.dtype)

    row = lax.broadcasted_iota(jnp.int32, (Q_BLOCK, LANES), 0)
    col = lax.broadcasted_iota(jnp.int32, (Q_BLOCK, LANES), 1)
    cur_ok = col <= row
    prev_ok = col >= row
    band_one = jnp.concatenate([prev_ok, cur_ok], axis=1)
    band_ok = jnp.concatenate([band_one, band_one], axis=0)
    lane_lo = col < HEAD_DIM
    p_scr, m_scr = scratch[0], scratch[1]
    n_pairs = ATTN_WIDTH // LANES

    def split_heads(a):
        zero = jnp.zeros_like(a)
        lo = lax.broadcasted_iota(jnp.int32, a.shape, 1) < HEAD_DIM
        return jnp.concatenate([jnp.where(lo, a, zero), jnp.where(lo, zero, a)], axis=0)

    def scores(r, r0, first):
        for hp in range(n_pairs):
            lanes = slice(hp * LANES, (hp + 1) * LANES)
            q_pair = q_ref[0, r, pl.ds(r0, Q_BLOCK), lanes]
            if first:
                k2 = split_heads(k_ref[0, r, pl.ds(r0, Q_BLOCK), lanes])
                s = lax.dot_general(q_pair, k2, (((1,), (1,)), ((), ())), preferred_element_type=F32)
                s = jnp.where(jnp.concatenate([cur_ok, cur_ok], axis=1), s, -jnp.inf)
                m0 = jnp.max(s[:, :Q_BLOCK], axis=-1, keepdims=True)
                m1 = jnp.max(s[:, Q_BLOCK:], axis=-1, keepdims=True)
                p = jnp.concatenate([jnp.exp2(s[:, :Q_BLOCK] - m0), jnp.exp2(s[:, Q_BLOCK:] - m1)], axis=1)
                p_scr[hp, 0:Q_BLOCK, :] = p.astype(BF16)
            else:
                k_win = k_ref[0, r, pl.ds(r0 - Q_BLOCK, 2 * Q_BLOCK), lanes]
                s = lax.dot_general(split_heads(q_pair), k_win, (((1,), (1,)), ((), ())),
                                    preferred_element_type=F32)
                s = jnp.where(band_ok, s, -jnp.inf)
                m = jnp.max(s, axis=-1, keepdims=True)
                m0, m1 = m[:Q_BLOCK], m[Q_BLOCK:]
                p_scr[hp] = jnp.exp2(s - m).astype(BF16)
            m_scr[hp] = jnp.where(lane_lo, m0, m1) * math.log(2.0)

    ones_lo = jnp.where(lane_lo, 1.0, 0.0).astype(BF16)
    ones_hi = jnp.where(lane_lo, 0.0, 1.0).astype(BF16)

    def values(r, r0, first):
        for hp in range(n_pairs):
            lanes = slice(hp * LANES, (hp + 1) * LANES)
            if first:
                p, v_win = p_scr[hp, 0:Q_BLOCK, :], v_ref[0, r, pl.ds(r0, Q_BLOCK), lanes]
                ind_lo, ind_hi = ones_lo, ones_hi
            else:
                p = jnp.concatenate([p_scr[hp, 0:Q_BLOCK, :], p_scr[hp, Q_BLOCK:, :]], axis=1)
                v_win = v_ref[0, r, pl.ds(r0 - Q_BLOCK, 2 * Q_BLOCK), lanes]
                ind_lo = jnp.concatenate([ones_lo, ones_lo], axis=0)
                ind_hi = jnp.concatenate([ones_hi, ones_hi], axis=0)
            v2 = jnp.concatenate([split_heads(v_win), jnp.concatenate([ind_lo, ind_hi], axis=0)], axis=1)
            o2 = _dot(p, v2)
            den = o2[:, LANES:]
            o_pair = o2[:, :LANES] * (1.0 / den)
            lse_pair = m_scr[hp] + jnp.log(den)
            if dilation == 1:
                o_ref[0, pl.ds(r0, Q_BLOCK), lanes] = o_pair.astype(o_ref.dtype)
                lse_ref[0, pl.ds(r0, Q_BLOCK), lanes] = lse_pair
            elif not token_order_out:
                o_ref[0, r, pl.ds(r0, Q_BLOCK), lanes] = o_pair.astype(o_ref.dtype)
                lse_ref[0, r, pl.ds(r0, Q_BLOCK), lanes] = lse_pair
            else:
                rows = pl.ds(r + r0 * dilation, Q_BLOCK, stride=dilation)
                scratch[2][hp, rows, :] = o_pair
                scratch[3][hp, rows, :] = lse_pair

    def at(blk):
        return pl.multiple_of(blk * Q_BLOCK, Q_BLOCK)

    scores(0, 0, True)
    if n_blk == 1:
        def res_body(r, carry):
            values(r - 1, 0, True)
            scores(r, 0, True)
            return carry
        lax.fori_loop(1, dilation, res_body, 0, unroll=BLOCK_UNROLL)
        values(dilation - 1, 0, True)
    else:
        for r in range(dilation):
            values(r, 0, True)
            scores(r, Q_BLOCK, False)

            def blk_body(blk, carry, r=r):
                values(r, at(blk - 1), False)
                scores(r, at(blk), False)
                return carry
            lax.fori_loop(2, n_blk, blk_body, 0, unroll=BLOCK_UNROLL)
            values(r, (n_blk - 1) * Q_BLOCK, False)
            if r + 1 < dilation:
                scores(r + 1, 0, True)

    if dilation > 1 and token_order_out:
        for hp in range(n_pairs):
            lanes = slice(hp * LANES, (hp + 1) * LANES)
            o_ref[0, :, lanes] = scratch[2][hp].astype(o_ref.dtype)
            lse_ref[0, :, lanes] = scratch[3][hp]


def _attn_call(g, dilation, q, k, v, token_order_out, casts=()):
    batch, _, res_len, _ = q.shape
    cast_in_specs, cast_out_specs, cast_out_shapes = [], [], []
    for arr, block, first, out_dims in casts:
        axis = 0 if block[0] != out_dims[0] else 1
        assert out_dims[axis] == batch * block[axis]
        def index(b, first=first, axis=axis, shift=0):
            return (b + shift, 0) if axis == 0 else (0, b + shift)
        cast_in_specs.append(pl.BlockSpec(block, functools.partial(index, shift=first)))
        cast_out_specs.append(pl.BlockSpec(block, index))
        cast_out_shapes.append(jax.ShapeDtypeStruct(out_dims, BF16))
    seq = dilation * res_len
    in_spec = pl.BlockSpec((1, dilation, res_len, ATTN_WIDTH), lambda b: (b, 0, 0, 0))
    n_pairs = ATTN_WIDTH // LANES
    scratch = [pltpu.VMEM((n_pairs, 2 * Q_BLOCK, 2 * Q_BLOCK), BF16),
               pltpu.VMEM((n_pairs, Q_BLOCK, LANES), F32)]
    if token_order_out or dilation == 1:
        out_spec, out_dims = pl.BlockSpec((1, seq, ATTN_WIDTH), lambda b: (b, 0, 0)), (batch, seq, ATTN_WIDTH)
        if dilation > 1:
            scratch += [pltpu.VMEM((n_pairs, seq, LANES), F32)] * 2
    else:
        out_spec, out_dims = in_spec, q.shape
    return pl.pallas_call(
        functools.partial(_attn_kernel, dilation=dilation, n_blk=res_len // Q_BLOCK,
                          token_order_out=token_order_out, n_cast=len(casts)),
        grid=(batch,),
        in_specs=[in_spec, in_spec, in_spec] + cast_in_specs,
        out_specs=[out_spec, out_spec] + cast_out_specs,
        out_shape=[jax.ShapeDtypeStruct(out_dims, BF16), jax.ShapeDtypeStruct(out_dims, F32)] + cast_out_shapes,
        scratch_shapes=scratch,
        compiler_params=pltpu.CompilerParams(
            dimension_semantics=("arbitrary",), vmem_limit_bytes=VMEM_LIMIT_BYTES),
        name=f"attn_g{g}",
    )(q, k, v, *[arr for arr, _, _, _ in casts])


def _token_order(ref, scr):
    _, dilation, n, width = ref.shape
    cols = []
    for cb in range(width // LANES):
        for r in range(dilation):
            scr[cb, pl.ds(r, n, stride=dilation), :] = ref[0, r, :, cb * LANES:(cb + 1) * LANES].astype(F32)
        cols.append(scr[cb])
    return jnp.concatenate(cols, axis=1)


def _merge_kernel(x_ref, ada_ref, norm_g_ref, w_ref,
                  o0_ref, o1_ref, o2_ref, l0_ref, l1_ref, l2_ref,
                  ln_g_ref, ln_b_ref, ws_ref, bias_ref, wa_ref, wb_ref, wo_ref, out_ref, o2_scr, l2_scr):
    xf = x_ref[0]
    tm = xf.shape[0]
    h = _ada_ln(xf, ada_ref, norm_g_ref)
    z = _dot(h.astype(BF16), w_ref[...])
    off = 0
    gate_a = z[:, off:off + ATTN_WIDTH]; off += ATTN_WIDTH
    u = _gelu_exact(z[:, off:off + GMLP_WIDTH]); off += GMLP_WIDTH
    v = _gelu_exact(z[:, off:off + GMLP_WIDTH]); off += GMLP_WIDTH
    gate_b = z[:, off:off + GMLP_WIDTH]; off += GMLP_WIDTH
    merge_a = z[:, off:off + D_MODEL]; off += D_MODEL
    merge_b = z[:, off:off + D_MODEL]

    l0, l1, l2 = l0_ref[0], l1_ref[0], _token_order(l2_ref, l2_scr)
    lmax = jnp.maximum(jnp.maximum(l0, l1), l2)
    e0, e1, e2 = jnp.exp(l0 - lmax), jnp.exp(l1 - lmax), jnp.exp(l2 - lmax)
    attn = (e0 * o0_ref[0].astype(F32) + e1 * o1_ref[0].astype(F32) + e2 * _token_order(o2_ref, o2_scr)) \
        / (e0 + e1 + e2)
    y_a = attn * _silu(gate_a)

    mu = jnp.mean(v, axis=-1, keepdims=True)
    vc = v - mu
    var = jnp.mean(vc * vc, axis=-1, keepdims=True)
    v_ln = (vc * lax.rsqrt(var + EPS) * ln_g_ref[...] + ln_b_ref[...]).astype(BF16)
    row = lax.broadcasted_iota(jnp.int32, (CHUNK, CHUNK), 0)
    col = lax.broadcasted_iota(jnp.int32, (CHUNK, CHUNK), 1)
    causal = row >= col
    lane_lo = col < GMLP_GROUP_DIM
    n_chunks = tm // CHUNK
    sv_cols = []
    for gp in range(GMLP_WIDTH // LANES):
        lanes = slice(gp * LANES, (gp + 1) * LANES)
        rhs = jnp.concatenate([v_ln[c * CHUNK:(c + 1) * CHUNK, lanes] for c in range(n_chunks)], axis=1)
        res = []
        for half in range(2):
            w_s = jnp.where(causal, ws_ref[2 * gp + half], 0.0).astype(BF16)
            res.append(_dot(w_s, rhs))
        sel = jnp.concatenate(
            [jnp.where(lane_lo, res[0][:, c * LANES:(c + 1) * LANES], res[1][:, c * LANES:(c + 1) * LANES])
             + bias_ref[:, lanes] for c in range(n_chunks)], axis=0)
        sv_cols.append(sel)
    sv = jnp.concatenate(sv_cols, axis=1)
    y_b = u * sv * _silu(gate_b)

    merged = (_sigmoid(merge_a) * _dot(y_a.astype(BF16), wa_ref[...])
              + _sigmoid(merge_b) * _dot(y_b.astype(BF16), wb_ref[...]))
    out = _dot(merged.astype(BF16), wo_ref[...])
    gate = ada_ref[0, :, 2 * D_MODEL:3 * D_MODEL]
    out_ref[0] = xf + gate * out


def _merge_call(x, ada3, norm_g2, w_in_bf16, os_, lses, ln_g, ln_b, w_spatial, bias_tab, wa, wb, wo):
    batch, seq, _ = x.shape
    const2 = lambda b, i: (0, 0)
    tile = lambda width: pl.BlockSpec((1, MERGE_TILE, width), lambda b, i: (b, i, 0))
    last_dilation = ATTN_GROUPS[-1][1]
    res_tile = pl.BlockSpec((1, last_dilation, MERGE_TILE // last_dilation, ATTN_WIDTH),
                            lambda b, i: (b, 0, i, 0))
    staging = pltpu.VMEM((ATTN_WIDTH // LANES, MERGE_TILE, LANES), F32)
    return pl.pallas_call(
        _merge_kernel,
        grid=(batch, seq // MERGE_TILE),
        in_specs=[
            tile(D_MODEL),
            pl.BlockSpec((1, 1, 3 * D_MODEL), lambda b, i: (b, 0, 0)),
            pl.BlockSpec((1, D_MODEL), const2),
            _resident((D_MODEL, REST_COLS)),
            tile(ATTN_WIDTH), tile(ATTN_WIDTH), res_tile,
            tile(ATTN_WIDTH), tile(ATTN_WIDTH), res_tile,
            pl.BlockSpec((1, GMLP_WIDTH), const2),
            pl.BlockSpec((1, GMLP_WIDTH), const2),
            pl.BlockSpec((GMLP_GROUPS, CHUNK, CHUNK), lambda b, i: (0, 0, 0)),
            pl.BlockSpec((CHUNK, GMLP_WIDTH), const2),
            _resident((ATTN_WIDTH, D_MODEL)),
            _resident((GMLP_WIDTH, D_MODEL)),
            _resident((D_MODEL, D_MODEL)),
        ],
        out_specs=tile(D_MODEL),
        out_shape=jax.ShapeDtypeStruct(x.shape, x.dtype),
        scratch_shapes=[staging, staging],
        compiler_params=pltpu.CompilerParams(
            dimension_semantics=("arbitrary", "arbitrary"), vmem_limit_bytes=VMEM_LIMIT_BYTES),
        name="merge",
    )(x, ada3, norm_g2, w_in_bf16, *os_, *lses, ln_g, ln_b, w_spatial, bias_tab, wa, wb, wo)


def kernel(x, c, positions, norm_g, w_ada, b_ada, w_in, q_norm_g, k_norm_g, sgu_ln_g, sgu_ln_b,
           w_spatial, b_spatial, w_branch_a, w_branch_b, w_out):
    batch, seq, d_model = x.shape
    assert d_model == D_MODEL and seq % QKV_TILE == 0 and seq % MERGE_TILE == 0
    assert w_in.shape == (D_MODEL, QKV_COLS + REST_COLS)

    ada = _ada_call(c, w_ada, b_ada)
    ada3 = ada.reshape(batch, 1, 3 * D_MODEL)
    norm_g2 = norm_g.reshape(1, D_MODEL)

    freq = (ROPE_THETA ** (-np.arange(0, ROPE_DIMS, 2, dtype=np.float32) / ROPE_DIMS)).astype(np.float32)
    freq = jnp.asarray(freq.reshape(ROPE_HALF, 1))
    e_mat = jnp.asarray(_rope_expand_matrix(), dtype=BF16)
    bd_mat = jnp.asarray(_head_mean_matrix(), dtype=BF16)

    gq = jnp.tile(q_norm_g, (1, LANES // HEAD_DIM))
    gk = jnp.tile(k_norm_g, (1, LANES // HEAD_DIM))
    qkv = _qkv_call(x, positions.reshape(batch, 1, seq), ada3, norm_g2, w_in, gq, gk,
                    freq, e_mat, bd_mat)
    rest_block = REST_COLS // batch
    assert QKV_COLS % rest_block == 0
    casts = [
        [(w_in, (D_MODEL, rest_block), QKV_COLS // rest_block, (D_MODEL, REST_COLS))],
        [(w_branch_a, (ATTN_WIDTH // batch, D_MODEL), 0, w_branch_a.shape),
         (w_branch_b, (GMLP_WIDTH // batch, D_MODEL), 0, w_branch_b.shape),
         (w_out, (D_MODEL // batch, D_MODEL), 0, w_out.shape)],
        [],
    ]
    outs, lses, weights = [], [], []
    for g, (window, dilation) in enumerate(ATTN_GROUPS):
        assert window // dilation == Q_BLOCK
        o, lse, *w_bf16 = _attn_call(g, dilation, *qkv[3 * g:3 * g + 3],
                                     token_order_out=g + 1 < len(ATTN_GROUPS), casts=casts[g])
        outs.append(o)
        lses.append(lse)
        weights += w_bf16
    w_rest_bf16, wa_bf16, wb_bf16, wo_bf16 = weights

    bias_tab = jnp.repeat(b_spatial.T, GMLP_GROUP_DIM, axis=1)
    return _merge_call(x, ada3, norm_g2, w_rest_bf16, outs, lses,
                       sgu_ln_g.reshape(1, GMLP_WIDTH), sgu_ln_b.reshape(1, GMLP_WIDTH),
                       w_spatial, bias_tab, wa_bf16, wb_bf16, wo_bf16)
```

```python
import functools
import math

import jax
import jax.numpy as jnp
import numpy as np
from jax import lax
from jax.experimental import pallas as pl
from jax.experimental.pallas import tpu as pltpu

D_MODEL = 1024
HEAD_DIM = 64
HEADS = 8
ATTN_GROUPS = ((128, 1), (512, 4), (2048, 16))
ATTN_WIDTH = HEADS * HEAD_DIM
Q_BLOCK = 128
ROPE_THETA = 500000.0
ROPE_DIMS = HEAD_DIM // 4
ROPE_HALF = ROPE_DIMS // 2
GMLP_WIDTH = 512
GMLP_GROUPS = 8
GMLP_GROUP_DIM = GMLP_WIDTH // GMLP_GROUPS
CHUNK = 128
EPS = 1e-6
GROUP_COLS = 3 * ATTN_WIDTH
QKV_COLS = len(ATTN_GROUPS) * GROUP_COLS
REST_COLS = ATTN_WIDTH + 3 * GMLP_WIDTH + 2 * D_MODEL

Q_SCALE = math.log2(math.e) / math.sqrt(HEAD_DIM)

LANES = 128
QKV_TILE = 512
MERGE_TILE = 512
VMEM_LIMIT_BYTES = 56 * 1024 * 1024

F32 = jnp.float32
BF16 = jnp.bfloat16


def _dot(a, b):
    return jnp.dot(a, b, preferred_element_type=F32)


def _resident(shape):
    return pl.BlockSpec(shape, lambda *_: (0, 0), pipeline_mode=pl.Buffered(1))


def _silu(v):
    return v * (1.0 / (1.0 + jnp.exp(-v)))


def _sigmoid(v):
    return 1.0 / (1.0 + jnp.exp(-v))


def _gelu_exact(v):
    return 0.5 * v * (1.0 + lax.erf(v * (1.0 / math.sqrt(2.0))))


def _ada_ln(xf, ada_ref, norm_g_ref):
    ms = jnp.mean(xf * xf, axis=-1, keepdims=True)
    shift = ada_ref[0, :, 0:D_MODEL]
    scale = ada_ref[0, :, D_MODEL:2 * D_MODEL]
    return xf * lax.rsqrt(ms + EPS) * norm_g_ref[...] * (1.0 + scale) + shift


def _ada_kernel(c_ref, w_ref, b_ref, o_ref):
    o_ref[...] = _dot(_silu(c_ref[...]).astype(BF16), w_ref[...].astype(BF16)) + b_ref[...]


def _ada_call(c, w_ada, b_ada):
    batch = c.shape[0]
    return pl.pallas_call(
        _ada_kernel,
        grid=(3,),
        in_specs=[pl.BlockSpec((batch, D_MODEL), lambda n: (0, 0)),
                  pl.BlockSpec((D_MODEL, D_MODEL), lambda n: (0, n)),
                  pl.BlockSpec((1, D_MODEL), lambda n: (0, n))],
        out_specs=pl.BlockSpec((batch, D_MODEL), lambda n: (0, n)),
        out_shape=jax.ShapeDtypeStruct((batch, 3 * D_MODEL), F32),
        compiler_params=pltpu.CompilerParams(dimension_semantics=("arbitrary",)),
        name="ada",
    )(c, w_ada, b_ada.reshape(1, 3 * D_MODEL))


def _rope_expand_matrix():
    e = np.zeros((64, 3 * LANES), np.float32)
    for lane in range(LANES):
        dim = lane % HEAD_DIM
        if dim < ROPE_DIMS:
            j = dim % ROPE_HALF
            for part in range(3):
                e[part * 8 + j, lane] = 1.0
                if dim >= ROPE_HALF:
                    e[24 + part * 8 + j, LANES + lane] = 1.0
                else:
                    e[24 + part * 8 + j, 2 * LANES + lane] = -1.0
        else:
            e[48, lane] = 1.0
    return e


def _head_mean_matrix():
    head = np.arange(2 * LANES) // HEAD_DIM
    return (head[:, None] == head[None, :]).astype(np.float32) / HEAD_DIM


def _split3(a):
    hi = a.astype(BF16).astype(F32)
    r = a - hi
    mid = r.astype(BF16).astype(F32)
    return hi, mid, r - mid


def _store_residue_major(out, val, dilation, perm_scr):
    tm, width = val.shape
    if dilation == 1:
        out[0, 0] = val.astype(out.dtype)
        return
    for cb in range(width // LANES):
        perm_scr[cb] = val[:, cb * LANES:(cb + 1) * LANES]
    for r in range(dilation):
        for cb in range(width // LANES):
            out[0, r, :, cb * LANES:(cb + 1) * LANES] = (
                perm_scr[cb, pl.ds(r, tm // dilation, stride=dilation), :].astype(out.dtype))


def _qkv_kernel(x_ref, pos_ref, ada_ref, norm_g_ref, gq_ref, gk_ref, freq_ref, e_ref, bd_ref, *refs):
    n_chunks = QKV_COLS // ATTN_WIDTH
    w_chunk_refs, out_refs = refs[:n_chunks], refs[n_chunks:2 * n_chunks]
    perm_scr, w_ref = refs[2 * n_chunks:]

    @pl.when((pl.program_id(0) == 0) & (pl.program_id(1) == 0))
    def _():
        for c, chunk_ref in enumerate(w_chunk_refs):
            w_ref[:, c * ATTN_WIDTH:(c + 1) * ATTN_WIDTH] = chunk_ref[...].astype(BF16)

    xf = x_ref[0]
    tm = xf.shape[0]
    hb = _ada_ln(xf, ada_ref, norm_g_ref).astype(BF16)

    ang = freq_ref[...] * pos_ref[0].astype(F32)
    parts = _split3(jnp.cos(ang)) + _split3(jnp.sin(ang))
    lhs_t = jnp.concatenate(list(parts) + [jnp.ones((8, tm), F32), jnp.zeros((8, tm), F32)], axis=0)
    tab = lax.dot_general(lhs_t.astype(BF16), e_ref[...], (((0,), (0,)), ((), ())),
                          preferred_element_type=F32)
    cos_t, s1_t, s2_t = tab[:, 0:LANES], tab[:, LANES:2 * LANES], tab[:, 2 * LANES:3 * LANES]

    n_groups = len(ATTN_GROUPS)
    chunks = ([(g, which) for which in (0, 1) for g in range(n_groups)]
              + [(g, 2) for g in reversed(range(n_groups))])
    for g, which in chunks:
        dilation = ATTN_GROUPS[g][1]
        col0 = g * GROUP_COLS + which * ATTN_WIDTH
        zc = _dot(hb, w_ref[:, col0:col0 + ATTN_WIDTH])
        out = out_refs[3 * g + which]
        if which == 2:
            _store_residue_major(out, zc, dilation, perm_scr)
            continue
        gain = gq_ref[g:g + 1, :] * Q_SCALE if which == 0 else gk_ref[g:g + 1, :]
        cols = []
        for half in range(ATTN_WIDTH // (2 * LANES)):
            a2 = zc[:, half * 2 * LANES:(half + 1) * 2 * LANES]
            ms = _dot((a2 * a2).astype(BF16), bd_ref[...])
            a2n = a2 * lax.rsqrt(ms + EPS)
            for sub in range(2):
                an = a2n[:, sub * LANES:(sub + 1) * LANES] * gain
                cols.append(an * cos_t + pltpu.roll(an, ROPE_HALF, 1) * s1_t
                            + pltpu.roll(an, LANES - ROPE_HALF, 1) * s2_t)
        _store_residue_major(out, jnp.concatenate(cols, axis=1), dilation, perm_scr)


def _qkv_call(x, pos3, ada3, norm_g2, w_in, gq, gk, freq, e_mat, bd_mat):
    batch, seq, _ = x.shape
    const = lambda b, j: (0, 0)
    n_chunks = QKV_COLS // ATTN_WIDTH
    w_specs = [pl.BlockSpec((D_MODEL, ATTN_WIDTH), functools.partial(lambda c, b, j: (0, c), c),
                            pipeline_mode=pl.Buffered(1)) for c in range(n_chunks)]
    out_specs, out_shapes = [], []
    for _, dilation in ATTN_GROUPS:
        spec = pl.BlockSpec((1, dilation, QKV_TILE // dilation, ATTN_WIDTH), lambda b, j: (b, 0, j, 0))
        sds = jax.ShapeDtypeStruct((batch, dilation, seq // dilation, ATTN_WIDTH), BF16)
        out_specs += [spec] * 3
        out_shapes += [sds] * 3
    n_groups = len(ATTN_GROUPS)
    return pl.pallas_call(
        _qkv_kernel,
        grid=(batch, seq // QKV_TILE),
        in_specs=[
            pl.BlockSpec((1, QKV_TILE, D_MODEL), lambda b, j: (b, j, 0)),
            pl.BlockSpec((1, 1, QKV_TILE), lambda b, j: (b, 0, j)),
            pl.BlockSpec((1, 1, 3 * D_MODEL), lambda b, j: (b, 0, 0)),
            pl.BlockSpec((1, D_MODEL), const),
            pl.BlockSpec((n_groups, LANES), const),
            pl.BlockSpec((n_groups, LANES), const),
            pl.BlockSpec((8, 1), const),
            pl.BlockSpec((64, 3 * LANES), const),
            pl.BlockSpec((2 * LANES, 2 * LANES), const),
        ] + w_specs,
        out_specs=out_specs,
        out_shape=out_shapes,
        scratch_shapes=[pltpu.VMEM((ATTN_WIDTH // LANES, QKV_TILE, LANES), F32),
                        pltpu.VMEM((D_MODEL, QKV_COLS), BF16)],
        compiler_params=pltpu.CompilerParams(
            dimension_semantics=("arbitrary", "arbitrary"), vmem_limit_bytes=VMEM_LIMIT_BYTES),
        name="qkv",
    )(x, pos3, ada3, norm_g2, gq, gk, freq, e_mat, bd_mat, *([w_in] * n_chunks))


def _attn_kernel(q_ref, k_ref, v_ref, *refs, dilation, n_blk, token_order_out, n_cast):
    cast_in, (o_ref, lse_ref) = refs[:n_cast], refs[n_cast:n_cast + 2]
    cast_out, scratch = refs[n_cast + 2:2 * n_cast + 2], refs[2 * n_cast + 2:]
    for src, dst in zip(cast_in, cast_out):
        dst[...] = src[...].astype(dst.dtype)

    row = lax.broadcasted_iota(jnp.int32, (Q_BLOCK, LANES), 0)
    col = lax.broadcasted_iota(jnp.int32, (Q_BLOCK, LANES), 1)
    cur_ok = col <= row
    prev_ok = col >= row
    lane_lo = col < HEAD_DIM
    ones_lo = jnp.where(lane_lo, 1.0, 0.0).astype(BF16)
    ones_hi = jnp.where(lane_lo, 0.0, 1.0).astype(BF16)
    p_scr, m_scr = scratch[0], scratch[1]
    n_pairs = ATTN_WIDTH // LANES

    def split_heads(a):
        zero = jnp.zeros_like(a)
        lo = lax.broadcasted_iota(jnp.int32, a.shape, 1) < HEAD_DIM
        return jnp.concatenate([jnp.where(lo, a, zero), jnp.where(lo, zero, a)], axis=0)

    def window(ref, r, r0, first, lanes):
        if first:
            return ref[0, r, r0:r0 + Q_BLOCK, lanes]
        return ref[0, r, r0 - Q_BLOCK:r0 + Q_BLOCK, lanes]

    def scores(r, r0):
        first = r0 == 0
        ok_head = cur_ok if first else jnp.concatenate([prev_ok, cur_ok], axis=1)
        ok = jnp.concatenate([ok_head, ok_head], axis=1)
        n_keys = ok_head.shape[1]
        for hp in range(n_pairs):
            lanes = slice(hp * LANES, (hp + 1) * LANES)
            q_pair = q_ref[0, r, r0:r0 + Q_BLOCK, lanes]
            k2 = split_heads(window(k_ref, r, r0, first, lanes))
            s = lax.dot_general(q_pair, k2, (((1,), (1,)), ((), ())), preferred_element_type=F32)
            s = jnp.where(ok, s, -jnp.inf)
            m0 = jnp.max(s[:, :n_keys], axis=-1, keepdims=True)
            m1 = jnp.max(s[:, n_keys:], axis=-1, keepdims=True)
            p = jnp.concatenate([jnp.exp2(s[:, :n_keys] - m0), jnp.exp2(s[:, n_keys:] - m1)], axis=1)
            p_scr[hp, :, 0:2 * n_keys] = p.astype(BF16)
            m_scr[hp] = jnp.where(lane_lo, m0, m1) * math.log(2.0)

    def values(r, r0):
        first = r0 == 0
        reps = 1 if first else 2
        ind = jnp.concatenate([ones_lo] * reps + [ones_hi] * reps, axis=0)
        for hp in range(n_pairs):
            lanes = slice(hp * LANES, (hp + 1) * LANES)
            p = p_scr[hp, :, 0:2 * reps * Q_BLOCK]
            v2 = jnp.concatenate([split_heads(window(v_ref, r, r0, first, lanes)), ind], axis=1)
            o2 = _dot(p, v2)
            den = o2[:, LANES:]
            o_pair = o2[:, :LANES] * (1.0 / den)
            lse_pair = m_scr[hp] + jnp.log(den)
            if dilation == 1:
                o_ref[0, r0:r0 + Q_BLOCK, lanes] = o_pair.astype(o_ref.dtype)
                lse_ref[0, r0:r0 + Q_BLOCK, lanes] = lse_pair
            elif not token_order_out:
                o_ref[0, r, r0:r0 + Q_BLOCK, lanes] = o_pair.astype(o_ref.dtype)
                lse_ref[0, r, r0:r0 + Q_BLOCK, lanes] = lse_pair
            else:
                rows = pl.ds(r + r0 * dilation, Q_BLOCK, stride=dilation)
                scratch[2][hp, rows, :] = o_pair
                scratch[3][hp, rows, :] = lse_pair

    blocks = [(r, blk * Q_BLOCK) for r in range(dilation) for blk in range(n_blk)]
    scores(*blocks[0])
    for prev, cur in zip(blocks[:-1], blocks[1:]):
        values(*prev)
        scores(*cur)
    values(*blocks[-1])

    if dilation > 1 and token_order_out:
        for hp in range(n_pairs):
            lanes = slice(hp * LANES, (hp + 1) * LANES)
            o_ref[0, :, lanes] = scratch[2][hp].astype(o_ref.dtype)
            lse_ref[0, :, lanes] = scratch[3][hp]


def _attn_call(g, dilation, q, k, v, token_order_out, casts=()):
    batch, _, res_len, _ = q.shape
    cast_in_specs, cast_out_specs, cast_out_shapes = [], [], []
    for arr, block, first, out_dims in casts:
        axis = 0 if block[0] != out_dims[0] else 1
        assert out_dims[axis] == batch * block[axis]
        def index(b, axis=axis, shift=0):
            return (b + shift, 0) if axis == 0 else (0, b + shift)
        cast_in_specs.append(pl.BlockSpec(block, functools.partial(index, shift=first)))
        cast_out_specs.append(pl.BlockSpec(block, index))
        cast_out_shapes.append(jax.ShapeDtypeStruct(out_dims, BF16))
    seq = dilation * res_len
    in_spec = pl.BlockSpec((1, dilation, res_len, ATTN_WIDTH), lambda b: (b, 0, 0, 0))
    n_pairs = ATTN_WIDTH // LANES
    scratch = [pltpu.VMEM((n_pairs, Q_BLOCK, 4 * Q_BLOCK), BF16),
               pltpu.VMEM((n_pairs, Q_BLOCK, LANES), F32)]
    if token_order_out or dilation == 1:
        out_spec, out_dims = pl.BlockSpec((1, seq, ATTN_WIDTH), lambda b: (b, 0, 0)), (batch, seq, ATTN_WIDTH)
        if dilation > 1:
            scratch += [pltpu.VMEM((n_pairs, seq, LANES), F32)] * 2
    else:
        out_spec, out_dims = in_spec, q.shape
    return pl.pallas_call(
        functools.partial(_attn_kernel, dilation=dilation, n_blk=res_len // Q_BLOCK,
                          token_order_out=token_order_out, n_cast=len(casts)),
        grid=(batch,),
        in_specs=[in_spec, in_spec, in_spec] + cast_in_specs,
        out_specs=[out_spec, out_spec] + cast_out_specs,
        out_shape=[jax.ShapeDtypeStruct(out_dims, BF16), jax.ShapeDtypeStruct(out_dims, F32)] + cast_out_shapes,
        scratch_shapes=scratch,
        compiler_params=pltpu.CompilerParams(
            dimension_semantics=("arbitrary",), vmem_limit_bytes=VMEM_LIMIT_BYTES),
        name=f"attn_g{g}",
    )(q, k, v, *[arr for arr, _, _, _ in casts])


def _token_order(ref, scr):
    _, dilation, n, width = ref.shape
    cols = []
    for cb in range(width // LANES):
        for r in range(dilation):
            scr[cb, pl.ds(r, n, stride=dilation), :] = ref[0, r, :, cb * LANES:(cb + 1) * LANES].astype(F32)
        cols.append(scr[cb])
    return jnp.concatenate(cols, axis=1)


def _merge_kernel(x_ref, ada_ref, norm_g_ref, w_ref,
                  o0_ref, o1_ref, o2_ref, l0_ref, l1_ref, l2_ref,
                  ln_g_ref, ln_b_ref, ws_ref, bias_ref, wa_ref, wb_ref, wo_ref, out_ref, o2_scr, l2_scr):
    xf = x_ref[0]
    tm = xf.shape[0]
    h = _ada_ln(xf, ada_ref, norm_g_ref)
    z = _dot(h.astype(BF16), w_ref[...])
    off = 0
    gate_a = z[:, off:off + ATTN_WIDTH]; off += ATTN_WIDTH
    u = _gelu_exact(z[:, off:off + GMLP_WIDTH]); off += GMLP_WIDTH
    v = _gelu_exact(z[:, off:off + GMLP_WIDTH]); off += GMLP_WIDTH
    gate_b = z[:, off:off + GMLP_WIDTH]; off += GMLP_WIDTH
    merge_a = z[:, off:off + D_MODEL]; off += D_MODEL
    merge_b = z[:, off:off + D_MODEL]

    l0, l1, l2 = l0_ref[0], l1_ref[0], _token_order(l2_ref, l2_scr)
    lmax = jnp.maximum(jnp.maximum(l0, l1), l2)
    e0, e1, e2 = jnp.exp(l0 - lmax), jnp.exp(l1 - lmax), jnp.exp(l2 - lmax)
    attn = (e0 * o0_ref[0].astype(F32) + e1 * o1_ref[0].astype(F32) + e2 * _token_order(o2_ref, o2_scr)) \
        / (e0 + e1 + e2)
    y_a = attn * _silu(gate_a)

    mu = jnp.mean(v, axis=-1, keepdims=True)
    vc = v - mu
    var = jnp.mean(vc * vc, axis=-1, keepdims=True)
    v_ln = (vc * lax.rsqrt(var + EPS) * ln_g_ref[...] + ln_b_ref[...]).astype(BF16)
    row = lax.broadcasted_iota(jnp.int32, (CHUNK, CHUNK), 0)
    col = lax.broadcasted_iota(jnp.int32, (CHUNK, CHUNK), 1)
    causal = row >= col
    lane_lo = col < GMLP_GROUP_DIM
    n_chunks = tm // CHUNK
    sv_cols = []
    for gp in range(GMLP_WIDTH // LANES):
        lanes = slice(gp * LANES, (gp + 1) * LANES)
        rhs = jnp.concatenate([v_ln[c * CHUNK:(c + 1) * CHUNK, lanes] for c in range(n_chunks)], axis=1)
        res = []
        for half in range(2):
            w_s = jnp.where(causal, ws_ref[2 * gp + half], 0.0).astype(BF16)
            res.append(_dot(w_s, rhs))
        sel = jnp.concatenate(
            [jnp.where(lane_lo, res[0][:, c * LANES:(c + 1) * LANES], res[1][:, c * LANES:(c + 1) * LANES])
             + bias_ref[:, lanes] for c in range(n_chunks)], axis=0)
        sv_cols.append(sel)
    sv = jnp.concatenate(sv_cols, axis=1)
    y_b = u * sv * _silu(gate_b)

    merged = (_sigmoid(merge_a) * _dot(y_a.astype(BF16), wa_ref[...])
              + _sigmoid(merge_b) * _dot(y_b.astype(BF16), wb_ref[...]))
    out = _dot(merged.astype(BF16), wo_ref[...])
    gate = ada_ref[0, :, 2 * D_MODEL:3 * D_MODEL]
    out_ref[0] = xf + gate * out


def _merge_call(x, ada3, norm_g2, w_rest, os_, lses, ln_g, ln_b, w_spatial, bias_tab, wa, wb, wo):
    batch, seq, _ = x.shape
    const2 = lambda b, i: (0, 0)
    tile = lambda width: pl.BlockSpec((1, MERGE_TILE, width), lambda b, i: (b, i, 0))
    last_dilation = ATTN_GROUPS[-1][1]
    res_tile = pl.BlockSpec((1, last_dilation, MERGE_TILE // last_dilation, ATTN_WIDTH),
                            lambda b, i: (b, 0, i, 0))
    staging = pltpu.VMEM((ATTN_WIDTH // LANES, MERGE_TILE, LANES), F32)
    return pl.pallas_call(
        _merge_kernel,
        grid=(batch, seq // MERGE_TILE),
        in_specs=[
            tile(D_MODEL),
            pl.BlockSpec((1, 1, 3 * D_MODEL), lambda b, i: (b, 0, 0)),
            pl.BlockSpec((1, D_MODEL), const2),
            _resident((D_MODEL, REST_COLS)),
            tile(ATTN_WIDTH), tile(ATTN_WIDTH), res_tile,
            tile(ATTN_WIDTH), tile(ATTN_WIDTH), res_tile,
            pl.BlockSpec((1, GMLP_WIDTH), const2),
            pl.BlockSpec((1, GMLP_WIDTH), const2),
            pl.BlockSpec((GMLP_GROUPS, CHUNK, CHUNK), lambda b, i: (0, 0, 0)),
            pl.BlockSpec((CHUNK, GMLP_WIDTH), const2),
            _resident((ATTN_WIDTH, D_MODEL)),
            _resident((GMLP_WIDTH, D_MODEL)),
            _resident((D_MODEL, D_MODEL)),
        ],
        out_specs=tile(D_MODEL),
        out_shape=jax.ShapeDtypeStruct(x.shape, x.dtype),
        scratch_shapes=[staging, staging],
        compiler_params=pltpu.CompilerParams(
            dimension_semantics=("arbitrary", "arbitrary"), vmem_limit_bytes=VMEM_LIMIT_BYTES),
        name="merge",
    )(x, ada3, norm_g2, w_rest, *os_, *lses, ln_g, ln_b, w_spatial, bias_tab, wa, wb, wo)


def kernel(x, c, positions, norm_g, w_ada, b_ada, w_in, q_norm_g, k_norm_g, sgu_ln_g, sgu_ln_b,
           w_spatial, b_spatial, w_branch_a, w_branch_b, w_out):
    batch, seq, d_model = x.shape
    assert d_model == D_MODEL and seq % QKV_TILE == 0 and seq % MERGE_TILE == 0
    assert w_in.shape == (D_MODEL, QKV_COLS + REST_COLS)

    ada = _ada_call(c, w_ada, b_ada)
    ada3 = ada.reshape(batch, 1, 3 * D_MODEL)
    norm_g2 = norm_g.reshape(1, D_MODEL)

    freq = (ROPE_THETA ** (-np.arange(0, ROPE_DIMS, 2, dtype=np.float32) / ROPE_DIMS)).astype(np.float32)
    freq = jnp.asarray(freq.reshape(ROPE_HALF, 1))
    e_mat = jnp.asarray(_rope_expand_matrix(), dtype=BF16)
    bd_mat = jnp.asarray(_head_mean_matrix(), dtype=BF16)

    gq = jnp.tile(q_norm_g, (1, LANES // HEAD_DIM))
    gk = jnp.tile(k_norm_g, (1, LANES // HEAD_DIM))
    qkv = _qkv_call(x, positions.reshape(batch, 1, seq), ada3, norm_g2, w_in, gq, gk,
                    freq, e_mat, bd_mat)
    rest_block = REST_COLS // batch
    assert QKV_COLS % rest_block == 0
    casts = [
        [(w_in, (D_MODEL, rest_block), QKV_COLS // rest_block, (D_MODEL, REST_COLS))],
        [(w_branch_a, (ATTN_WIDTH // batch, D_MODEL), 0, w_branch_a.shape),
         (w_branch_b, (GMLP_WIDTH // batch, D_MODEL), 0, w_branch_b.shape),
         (w_out, (D_MODEL // batch, D_MODEL), 0, w_out.shape)],
        [],
    ]
    outs, lses, weights = [], [], []
    for g, (window, dilation) in enumerate(ATTN_GROUPS):
        assert window // dilation == Q_BLOCK
        o, lse, *w_bf16 = _attn_call(g, dilation, *qkv[3 * g:3 * g + 3],
                                     token_order_out=g + 1 < len(ATTN_GROUPS), casts=casts[g])
        outs.append(o)
        lses.append(lse)
        weights += w_bf16
    w_rest_bf16, wa_bf16, wb_bf16, wo_bf16 = weights

    bias_tab = jnp.repeat(b_spatial.T, GMLP_GROUP_DIM, axis=1)
    return _merge_call(x, ada3, norm_g2, w_rest_bf16, outs, lses,
                       sgu_ln_g.reshape(1, GMLP_WIDTH), sgu_ln_b.reshape(1, GMLP_WIDTH),
                       w_spatial, bias_tab, wa_bf16, wb_bf16, wo_bf16)
```

```python
import functools
import math

import jax
import jax.numpy as jnp
import numpy as np
from jax import lax
from jax.experimental import pallas as pl
from jax.experimental.pallas import tpu as pltpu

D_MODEL = 1024
HEAD_DIM = 64
HEADS = 8
ATTN_GROUPS = ((128, 1), (512, 4), (2048, 16))
ATTN_WIDTH = HEADS * HEAD_DIM
Q_BLOCK = 128
ROPE_THETA = 500000.0
ROPE_DIMS = HEAD_DIM // 4
ROPE_HALF = ROPE_DIMS // 2
GMLP_WIDTH = 512
GMLP_GROUPS = 8
GMLP_GROUP_DIM = GMLP_WIDTH // GMLP_GROUPS
CHUNK = 128
EPS = 1e-6
GROUP_COLS = 3 * ATTN_WIDTH
QKV_COLS = len(ATTN_GROUPS) * GROUP_COLS
REST_COLS = ATTN_WIDTH + 3 * GMLP_WIDTH + 2 * D_MODEL

Q_SCALE = math.log2(math.e) / math.sqrt(HEAD_DIM)

LANES = 128
QKV_TILE = 512
MERGE_TILE = 512
VMEM_LIMIT_BYTES = 56 * 1024 * 1024

F32 = jnp.float32
BF16 = jnp.bfloat16


def _dot(a, b):
    return jnp.dot(a, b, preferred_element_type=F32)


def _resident(shape):
    return pl.BlockSpec(shape, lambda *_: (0, 0), pipeline_mode=pl.Buffered(1))


def _silu(v):
    return v * (1.0 / (1.0 + jnp.exp(-v)))


def _sigmoid(v):
    return 1.0 / (1.0 + jnp.exp(-v))


def _gelu_exact(v):
    return 0.5 * v * (1.0 + lax.erf(v * (1.0 / math.sqrt(2.0))))


def _ada_ln(xf, ada_ref, norm_g_ref):
    ms = jnp.mean(xf * xf, axis=-1, keepdims=True)
    shift = ada_ref[0, :, 0:D_MODEL]
    scale = ada_ref[0, :, D_MODEL:2 * D_MODEL]
    return xf * lax.rsqrt(ms + EPS) * norm_g_ref[...] * (1.0 + scale) + shift


def _ada_kernel(c_ref, w_ref, b_ref, o_ref):
    o_ref[...] = _dot(_silu(c_ref[...]).astype(BF16), w_ref[...].astype(BF16)) + b_ref[...]


def _ada_call(c, w_ada, b_ada):
    batch = c.shape[0]
    return pl.pallas_call(
        _ada_kernel,
        grid=(3,),
        in_specs=[pl.BlockSpec((batch, D_MODEL), lambda n: (0, 0)),
                  pl.BlockSpec((D_MODEL, D_MODEL), lambda n: (0, n)),
                  pl.BlockSpec((1, D_MODEL), lambda n: (0, n))],
        out_specs=pl.BlockSpec((batch, D_MODEL), lambda n: (0, n)),
        out_shape=jax.ShapeDtypeStruct((batch, 3 * D_MODEL), F32),
        compiler_params=pltpu.CompilerParams(dimension_semantics=("arbitrary",)),
        name="ada",
    )(c, w_ada, b_ada.reshape(1, 3 * D_MODEL))


def _rope_expand_matrix():
    e = np.zeros((64, 3 * LANES), np.float32)
    for lane in range(LANES):
        dim = lane % HEAD_DIM
        if dim < ROPE_DIMS:
            j = dim % ROPE_HALF
            for part in range(3):
                e[part * 8 + j, lane] = 1.0
                if dim >= ROPE_HALF:
                    e[24 + part * 8 + j, LANES + lane] = 1.0
                else:
                    e[24 + part * 8 + j, 2 * LANES + lane] = -1.0
        else:
            e[48, lane] = 1.0
    return e


def _head_mean_matrix():
    head = np.arange(2 * LANES) // HEAD_DIM
    return (head[:, None] == head[None, :]).astype(np.float32) / HEAD_DIM


def _split3(a):
    hi = a.astype(BF16).astype(F32)
    r = a - hi
    mid = r.astype(BF16).astype(F32)
    return hi, mid, r - mid


def _store_residue_major(out, val, dilation, perm_scr):
    tm, width = val.shape
    if dilation == 1:
        out[0, 0] = val.astype(out.dtype)
        return
    for cb in range(width // LANES):
        perm_scr[cb] = val[:, cb * LANES:(cb + 1) * LANES]
    for r in range(dilation):
        for cb in range(width // LANES):
            out[0, r, :, cb * LANES:(cb + 1) * LANES] = (
                perm_scr[cb, pl.ds(r, tm // dilation, stride=dilation), :].astype(out.dtype))


def _qkv_kernel(x_ref, pos_ref, ada_ref, norm_g_ref, gq_ref, gk_ref, freq_ref, e_ref, bd_ref, *refs):
    n_chunks = QKV_COLS // ATTN_WIDTH
    w_chunk_refs, out_refs = refs[:n_chunks], refs[n_chunks:2 * n_chunks]
    perm_scr, w_ref = refs[2 * n_chunks:]

    @pl.when((pl.program_id(0) == 0) & (pl.program_id(1) == 0))
    def _():
        for c, chunk_ref in enumerate(w_chunk_refs):
            w_ref[:, c * ATTN_WIDTH:(c + 1) * ATTN_WIDTH] = chunk_ref[...].astype(BF16)

    xf = x_ref[0]
    tm = xf.shape[0]
    hb = _ada_ln(xf, ada_ref, norm_g_ref).astype(BF16)

    ang = freq_ref[...] * pos_ref[0].astype(F32)
    parts = _split3(jnp.cos(ang)) + _split3(jnp.sin(ang))
    lhs_t = jnp.concatenate(list(parts) + [jnp.ones((8, tm), F32), jnp.zeros((8, tm), F32)], axis=0)
    tab = lax.dot_general(lhs_t.astype(BF16), e_ref[...], (((0,), (0,)), ((), ())),
                          preferred_element_type=F32)
    cos_t, s1_t, s2_t = tab[:, 0:LANES], tab[:, LANES:2 * LANES], tab[:, 2 * LANES:3 * LANES]

    n_groups = len(ATTN_GROUPS)
    chunks = ([(g, which) for which in (0, 1) for g in range(n_groups)]
              + [(g, 2) for g in reversed(range(n_groups))])
    for g, which in chunks:
        dilation = ATTN_GROUPS[g][1]
        col0 = g * GROUP_COLS + which * ATTN_WIDTH
        zc = _dot(hb, w_ref[:, col0:col0 + ATTN_WIDTH])
        out = out_refs[3 * g + which]
        if which == 2:
            _store_residue_major(out, zc, dilation, perm_scr)
            continue
        gain = gq_ref[g:g + 1, :] * Q_SCALE if which == 0 else gk_ref[g:g + 1, :]
        cols = []
        for half in range(ATTN_WIDTH // (2 * LANES)):
            a2 = zc[:, half * 2 * LANES:(half + 1) * 2 * LANES]
            ms = _dot((a2 * a2).astype(BF16), bd_ref[...])
            a2n = a2 * lax.rsqrt(ms + EPS)
            for sub in range(2):
                an = a2n[:, sub * LANES:(sub + 1) * LANES] * gain
                cols.append(an * cos_t + pltpu.roll(an, ROPE_HALF, 1) * s1_t
                            + pltpu.roll(an, LANES - ROPE_HALF, 1) * s2_t)
        _store_residue_major(out, jnp.concatenate(cols, axis=1), dilation, perm_scr)


def _qkv_call(x, pos3, ada3, norm_g2, w_in, gq, gk, freq, e_mat, bd_mat):
    batch, seq, _ = x.shape
    const = lambda b, j: (0, 0)
    n_chunks = QKV_COLS // ATTN_WIDTH
    w_specs = [pl.BlockSpec((D_MODEL, ATTN_WIDTH), functools.partial(lambda c, b, j: (0, c), c),
                            pipeline_mode=pl.Buffered(1)) for c in range(n_chunks)]
    out_specs, out_shapes = [], []
    for _, dilation in ATTN_GROUPS:
        spec = pl.BlockSpec((1, dilation, QKV_TILE // dilation, ATTN_WIDTH), lambda b, j: (b, 0, j, 0))
        sds = jax.ShapeDtypeStruct((batch, dilation, seq // dilation, ATTN_WIDTH), BF16)
        out_specs += [spec] * 3
        out_shapes += [sds] * 3
    n_groups = len(ATTN_GROUPS)
    return pl.pallas_call(
        _qkv_kernel,
        grid=(batch, seq // QKV_TILE),
        in_specs=[
            pl.BlockSpec((1, QKV_TILE, D_MODEL), lambda b, j: (b, j, 0)),
            pl.BlockSpec((1, 1, QKV_TILE), lambda b, j: (b, 0, j)),
            pl.BlockSpec((1, 1, 3 * D_MODEL), lambda b, j: (b, 0, 0)),
            pl.BlockSpec((1, D_MODEL), const),
            pl.BlockSpec((n_groups, LANES), const),
            pl.BlockSpec((n_groups, LANES), const),
            pl.BlockSpec((8, 1), const),
            pl.BlockSpec((64, 3 * LANES), const),
            pl.BlockSpec((2 * LANES, 2 * LANES), const),
        ] + w_specs,
        out_specs=out_specs,
        out_shape=out_shapes,
        scratch_shapes=[pltpu.VMEM((ATTN_WIDTH // LANES, QKV_TILE, LANES), F32),
                        pltpu.VMEM((D_MODEL, QKV_COLS), BF16)],
        compiler_params=pltpu.CompilerParams(
            dimension_semantics=("arbitrary", "arbitrary"), vmem_limit_bytes=VMEM_LIMIT_BYTES),
        name="qkv",
    )(x, pos3, ada3, norm_g2, gq, gk, freq, e_mat, bd_mat, *([w_in] * n_chunks))


def _attn_kernel(q_ref, k_ref, v_ref, *refs, dilation, n_blk, token_order_out, n_cast):
    cast_in, (o_ref, lse_ref) = refs[:n_cast], refs[n_cast:n_cast + 2]
    cast_out, scratch = refs[n_cast + 2:2 * n_cast + 2], refs[2 * n_cast + 2:]
    for src, dst in zip(cast_in, cast_out):
        dst[...] = src[...].astype(dst.dtype)

    row = lax.broadcasted_iota(jnp.int32, (Q_BLOCK, LANES), 0)
    col = lax.broadcasted_iota(jnp.int32, (Q_BLOCK, LANES), 1)
    cur_ok = col <= row
    prev_ok = col >= row
    lane_lo = col < HEAD_DIM
    ones_lo = jnp.where(lane_lo, 1.0, 0.0).astype(BF16)
    ones_hi = jnp.where(lane_lo, 0.0, 1.0).astype(BF16)
    n_pairs = ATTN_WIDTH // LANES
    stage_in_vmem = n_blk > 1
    scratch = list(scratch)
    if stage_in_vmem:
        p_scr, m_scr = scratch.pop(0), scratch.pop(0)
    if dilation > 1 and token_order_out:
        o_plane, lse_plane = scratch

    def split_heads(a):
        zero = jnp.zeros_like(a)
        lo = lax.broadcasted_iota(jnp.int32, a.shape, 1) < HEAD_DIM
        return jnp.concatenate([jnp.where(lo, a, zero), jnp.where(lo, zero, a)], axis=0)

    def window(ref, r, r0, first, lanes):
        if first:
            return ref[0, r, r0:r0 + Q_BLOCK, lanes]
        return ref[0, r, r0 - Q_BLOCK:r0 + Q_BLOCK, lanes]

    def scores(r, r0):
        first = r0 == 0
        ok_head = cur_ok if first else jnp.concatenate([prev_ok, cur_ok], axis=1)
        ok = jnp.concatenate([ok_head, ok_head], axis=1)
        n_keys = ok_head.shape[1]
        stats = []
        for hp in range(n_pairs):
            lanes = slice(hp * LANES, (hp + 1) * LANES)
            q_pair = q_ref[0, r, r0:r0 + Q_BLOCK, lanes]
            k2 = split_heads(window(k_ref, r, r0, first, lanes))
            s = lax.dot_general(q_pair, k2, (((1,), (1,)), ((), ())), preferred_element_type=F32)
            s = jnp.where(ok, s, -jnp.inf)
            m0 = jnp.max(s[:, :n_keys], axis=-1, keepdims=True)
            m1 = jnp.max(s[:, n_keys:], axis=-1, keepdims=True)
            p = jnp.concatenate([jnp.exp2(s[:, :n_keys] - m0), jnp.exp2(s[:, n_keys:] - m1)], axis=1)
            p, m_ln2 = p.astype(BF16), jnp.where(lane_lo, m0, m1) * math.log(2.0)
            if stage_in_vmem:
                p_scr[hp, :, 0:2 * n_keys] = p
                m_scr[hp] = m_ln2
            stats.append((p, m_ln2))
        return stats

    def values(r, r0, stats):
        first = r0 == 0
        reps = 1 if first else 2
        ind = jnp.concatenate([ones_lo] * reps + [ones_hi] * reps, axis=0)
        for hp, (p, m_ln2) in enumerate(stats):
            if stage_in_vmem:
                p, m_ln2 = p_scr[hp, :, 0:2 * reps * Q_BLOCK], m_scr[hp]
            lanes = slice(hp * LANES, (hp + 1) * LANES)
            v2 = jnp.concatenate([split_heads(window(v_ref, r, r0, first, lanes)), ind], axis=1)
            o2 = _dot(p, v2)
            den = o2[:, LANES:]
            o_pair = o2[:, :LANES] * (1.0 / den)
            lse_pair = m_ln2 + jnp.log(den)
            if dilation == 1:
                o_ref[0, r0:r0 + Q_BLOCK, lanes] = o_pair.astype(o_ref.dtype)
                lse_ref[0, r0:r0 + Q_BLOCK, lanes] = lse_pair
            elif not token_order_out:
                o_ref[0, r, r0:r0 + Q_BLOCK, lanes] = o_pair.astype(o_ref.dtype)
                lse_ref[0, r, r0:r0 + Q_BLOCK, lanes] = lse_pair
            else:
                rows = pl.ds(r + r0 * dilation, Q_BLOCK, stride=dilation)
                o_plane[hp, rows, :] = o_pair
                lse_plane[hp, rows, :] = lse_pair

    blocks = [(r, blk * Q_BLOCK) for r in range(dilation) for blk in range(n_blk)]
    stats = scores(*blocks[0])
    for prev, cur in zip(blocks[:-1], blocks[1:]):
        if stage_in_vmem:
            values(*prev, stats)
            stats = scores(*cur)
        else:
            next_stats = scores(*cur)
            values(*prev, stats)
            stats = next_stats
    values(*blocks[-1], stats)

    if dilation > 1 and token_order_out:
        for hp in range(n_pairs):
            lanes = slice(hp * LANES, (hp + 1) * LANES)
            o_ref[0, :, lanes] = o_plane[hp].astype(o_ref.dtype)
            lse_ref[0, :, lanes] = lse_plane[hp]


def _attn_call(g, dilation, q, k, v, token_order_out, casts=()):
    batch, _, res_len, _ = q.shape
    cast_in_specs, cast_out_specs, cast_out_shapes = [], [], []
    for arr, block, first, out_dims in casts:
        axis = 0 if block[0] != out_dims[0] else 1
        assert out_dims[axis] == batch * block[axis]
        def index(b, axis=axis, shift=0):
            return (b + shift, 0) if axis == 0 else (0, b + shift)
        cast_in_specs.append(pl.BlockSpec(block, functools.partial(index, shift=first)))
        cast_out_specs.append(pl.BlockSpec(block, index))
        cast_out_shapes.append(jax.ShapeDtypeStruct(out_dims, BF16))
    seq = dilation * res_len
    in_spec = pl.BlockSpec((1, dilation, res_len, ATTN_WIDTH), lambda b: (b, 0, 0, 0))
    n_pairs = ATTN_WIDTH // LANES
    scratch = []
    if res_len > Q_BLOCK:
        scratch += [pltpu.VMEM((n_pairs, Q_BLOCK, 4 * Q_BLOCK), BF16),
                    pltpu.VMEM((n_pairs, Q_BLOCK, LANES), F32)]
    if token_order_out or dilation == 1:
        out_spec, out_dims = pl.BlockSpec((1, seq, ATTN_WIDTH), lambda b: (b, 0, 0)), (batch, seq, ATTN_WIDTH)
        if dilation > 1:
            scratch += [pltpu.VMEM((n_pairs, seq, LANES), F32)] * 2
    else:
        out_spec, out_dims = in_spec, q.shape
    return pl.pallas_call(
        functools.partial(_attn_kernel, dilation=dilation, n_blk=res_len // Q_BLOCK,
                          token_order_out=token_order_out, n_cast=len(casts)),
        grid=(batch,),
        in_specs=[in_spec, in_spec, in_spec] + cast_in_specs,
        out_specs=[out_spec, out_spec] + cast_out_specs,
        out_shape=[jax.ShapeDtypeStruct(out_dims, BF16), jax.ShapeDtypeStruct(out_dims, F32)] + cast_out_shapes,
        scratch_shapes=scratch,
        compiler_params=pltpu.CompilerParams(
            dimension_semantics=("arbitrary",), vmem_limit_bytes=VMEM_LIMIT_BYTES),
        name=f"attn_g{g}",
    )(q, k, v, *[arr for arr, _, _, _ in casts])


def _token_order(ref, scr):
    _, dilation, n, width = ref.shape
    cols = []
    for cb in range(width // LANES):
        for r in range(dilation):
            scr[cb, pl.ds(r, n, stride=dilation), :] = ref[0, r, :, cb * LANES:(cb + 1) * LANES].astype(F32)
        cols.append(scr[cb])
    return jnp.concatenate(cols, axis=1)


def _merge_kernel(x_ref, ada_ref, norm_g_ref, w_ref,
                  o0_ref, o1_ref, o2_ref, l0_ref, l1_ref, l2_ref,
                  ln_g_ref, ln_b_ref, ws_ref, bias_ref, wa_ref, wb_ref, wo_ref, out_ref, o2_scr, l2_scr):
    xf = x_ref[0]
    tm = xf.shape[0]
    h = _ada_ln(xf, ada_ref, norm_g_ref)
    z = _dot(h.astype(BF16), w_ref[...])
    off = 0
    gate_a = z[:, off:off + ATTN_WIDTH]; off += ATTN_WIDTH
    u = _gelu_exact(z[:, off:off + GMLP_WIDTH]); off += GMLP_WIDTH
    v = _gelu_exact(z[:, off:off + GMLP_WIDTH]); off += GMLP_WIDTH
    gate_b = z[:, off:off + GMLP_WIDTH]; off += GMLP_WIDTH
    merge_a = z[:, off:off + D_MODEL]; off += D_MODEL
    merge_b = z[:, off:off + D_MODEL]

    l0, l1, l2 = l0_ref[0], l1_ref[0], _token_order(l2_ref, l2_scr)
    lmax = jnp.maximum(jnp.maximum(l0, l1), l2)
    e0, e1, e2 = jnp.exp(l0 - lmax), jnp.exp(l1 - lmax), jnp.exp(l2 - lmax)
    attn = (e0 * o0_ref[0].astype(F32) + e1 * o1_ref[0].astype(F32) + e2 * _token_order(o2_ref, o2_scr)) \
        / (e0 + e1 + e2)
    y_a = attn * _silu(gate_a)

    mu = jnp.mean(v, axis=-1, keepdims=True)
    vc = v - mu
    var = jnp.mean(vc * vc, axis=-1, keepdims=True)
    v_ln = (vc * lax.rsqrt(var + EPS) * ln_g_ref[...] + ln_b_ref[...]).astype(BF16)
    row = lax.broadcasted_iota(jnp.int32, (CHUNK, CHUNK), 0)
    col = lax.broadcasted_iota(jnp.int32, (CHUNK, CHUNK), 1)
    causal = row >= col
    lane_lo = col < GMLP_GROUP_DIM
    n_chunks = tm // CHUNK
    sv_cols = []
    for gp in range(GMLP_WIDTH // LANES):
        lanes = slice(gp * LANES, (gp + 1) * LANES)
        rhs = jnp.concatenate([v_ln[c * CHUNK:(c + 1) * CHUNK, lanes] for c in range(n_chunks)], axis=1)
        res = []
        for half in range(2):
            w_s = jnp.where(causal, ws_ref[2 * gp + half], 0.0).astype(BF16)
            res.append(_dot(w_s, rhs))
        sel = jnp.concatenate(
            [jnp.where(lane_lo, res[0][:, c * LANES:(c + 1) * LANES], res[1][:, c * LANES:(c + 1) * LANES])
             + bias_ref[:, lanes] for c in range(n_chunks)], axis=0)
        sv_cols.append(sel)
    sv = jnp.concatenate(sv_cols, axis=1)
    y_b = u * sv * _silu(gate_b)

    merged = (_sigmoid(merge_a) * _dot(y_a.astype(BF16), wa_ref[...])
              + _sigmoid(merge_b) * _dot(y_b.astype(BF16), wb_ref[...]))
    out = _dot(merged.astype(BF16), wo_ref[...])
    gate = ada_ref[0, :, 2 * D_MODEL:3 * D_MODEL]
    out_ref[0] = xf + gate * out


def _merge_call(x, ada3, norm_g2, w_rest, os_, lses, ln_g, ln_b, w_spatial, bias_tab, wa, wb, wo):
    batch, seq, _ = x.shape
    const2 = lambda b, i: (0, 0)
    tile = lambda width: pl.BlockSpec((1, MERGE_TILE, width), lambda b, i: (b, i, 0))
    last_dilation = ATTN_GROUPS[-1][1]
    res_tile = pl.BlockSpec((1, last_dilation, MERGE_TILE // last_dilation, ATTN_WIDTH),
                            lambda b, i: (b, 0, i, 0))
    staging = pltpu.VMEM((ATTN_WIDTH // LANES, MERGE_TILE, LANES), F32)
    return pl.pallas_call(
        _merge_kernel,
        grid=(batch, seq // MERGE_TILE),
        in_specs=[
            tile(D_MODEL),
            pl.BlockSpec((1, 1, 3 * D_MODEL), lambda b, i: (b, 0, 0)),
            pl.BlockSpec((1, D_MODEL), const2),
            _resident((D_MODEL, REST_COLS)),
            tile(ATTN_WIDTH), tile(ATTN_WIDTH), res_tile,
            tile(ATTN_WIDTH), tile(ATTN_WIDTH), res_tile,
            pl.BlockSpec((1, GMLP_WIDTH), const2),
            pl.BlockSpec((1, GMLP_WIDTH), const2),
            pl.BlockSpec((GMLP_GROUPS, CHUNK, CHUNK), lambda b, i: (0, 0, 0)),
            pl.BlockSpec((CHUNK, GMLP_WIDTH), const2),
            _resident((ATTN_WIDTH, D_MODEL)),
            _resident((GMLP_WIDTH, D_MODEL)),
            _resident((D_MODEL, D_MODEL)),
        ],
        out_specs=tile(D_MODEL),
        out_shape=jax.ShapeDtypeStruct(x.shape, x.dtype),
        scratch_shapes=[staging, staging],
        compiler_params=pltpu.CompilerParams(
            dimension_semantics=("arbitrary", "arbitrary"), vmem_limit_bytes=VMEM_LIMIT_BYTES),
        name="merge",
    )(x, ada3, norm_g2, w_rest, *os_, *lses, ln_g, ln_b, w_spatial, bias_tab, wa, wb, wo)


def kernel(x, c, positions, norm_g, w_ada, b_ada, w_in, q_norm_g, k_norm_g, sgu_ln_g, sgu_ln_b,
           w_spatial, b_spatial, w_branch_a, w_branch_b, w_out):
    batch, seq, d_model = x.shape
    assert d_model == D_MODEL and seq % QKV_TILE == 0 and seq % MERGE_TILE == 0
    assert w_in.shape == (D_MODEL, QKV_COLS + REST_COLS)

    ada = _ada_call(c, w_ada, b_ada)
    ada3 = ada.reshape(batch, 1, 3 * D_MODEL)
    norm_g2 = norm_g.reshape(1, D_MODEL)

    freq = (ROPE_THETA ** (-np.arange(0, ROPE_DIMS, 2, dtype=np.float32) / ROPE_DIMS)).astype(np.float32)
    freq = jnp.asarray(freq.reshape(ROPE_HALF, 1))
    e_mat = jnp.asarray(_rope_expand_matrix(), dtype=BF16)
    bd_mat = jnp.asarray(_head_mean_matrix(), dtype=BF16)

    gq = jnp.tile(q_norm_g, (1, LANES // HEAD_DIM))
    gk = jnp.tile(k_norm_g, (1, LANES // HEAD_DIM))
    qkv = _qkv_call(x, positions.reshape(batch, 1, seq), ada3, norm_g2, w_in, gq, gk,
                    freq, e_mat, bd_mat)
    rest_block = REST_COLS // batch
    assert QKV_COLS % rest_block == 0
    casts = [
        [(w_in, (D_MODEL, rest_block), QKV_COLS // rest_block, (D_MODEL, REST_COLS))],
        [(w_branch_a, (ATTN_WIDTH // batch, D_MODEL), 0, w_branch_a.shape),
         (w_branch_b, (GMLP_WIDTH // batch, D_MODEL), 0, w_branch_b.shape),
         (w_out, (D_MODEL // batch, D_MODEL), 0, w_out.shape)],
        [],
    ]
    outs, lses, weights = [], [], []
    for g, (window, dilation) in enumerate(ATTN_GROUPS):
        assert window // dilation == Q_BLOCK
        o, lse, *w_bf16 = _attn_call(g, dilation, *qkv[3 * g:3 * g + 3],
                                     token_order_out=g + 1 < len(ATTN_GROUPS), casts=casts[g])
        outs.append(o)
        lses.append(lse)
        weights += w_bf16
    w_rest_bf16, wa_bf16, wb_bf16, wo_bf16 = weights

    bias_tab = jnp.repeat(b_spatial.T, GMLP_GROUP_DIM, axis=1)
    return _merge_call(x, ada3, norm_g2, w_rest_bf16, outs, lses,
                       sgu_ln_g.reshape(1, GMLP_WIDTH), sgu_ln_b.reshape(1, GMLP_WIDTH),
                       w_spatial, bias_tab, wa_bf16, wb_bf16, wo_bf16)
```

```python
import functools
import math

import jax
import jax.numpy as jnp
import numpy as np
from jax import lax
from jax.experimental import pallas as pl
from jax.experimental.pallas import tpu as pltpu

D_MODEL = 1024
HEAD_DIM = 64
HEADS = 8
ATTN_GROUPS = ((128, 1), (512, 4), (2048, 16))
ATTN_WIDTH = HEADS * HEAD_DIM
Q_BLOCK = 128
ROPE_THETA = 500000.0
ROPE_DIMS = HEAD_DIM // 4
ROPE_HALF = ROPE_DIMS // 2
GMLP_WIDTH = 512
GMLP_GROUPS = 8
GMLP_GROUP_DIM = GMLP_WIDTH // GMLP_GROUPS
CHUNK = 128
EPS = 1e-6
GROUP_COLS = 3 * ATTN_WIDTH
QKV_COLS = len(ATTN_GROUPS) * GROUP_COLS
REST_COLS = ATTN_WIDTH + 3 * GMLP_WIDTH + 2 * D_MODEL

Q_SCALE = math.log2(math.e) / math.sqrt(HEAD_DIM)

LANES = 128
QKV_TILE = 512
MERGE_TILE = 512
VMEM_LIMIT_BYTES = 56 * 1024 * 1024

F32 = jnp.float32
BF16 = jnp.bfloat16


def _dot(a, b):
    return jnp.dot(a, b, preferred_element_type=F32)


def _resident(shape):
    return pl.BlockSpec(shape, lambda *_: (0, 0), pipeline_mode=pl.Buffered(1))


def _silu(v):
    return v * (1.0 / (1.0 + jnp.exp(-v)))


def _sigmoid(v):
    return 1.0 / (1.0 + jnp.exp(-v))


def _gelu_exact(v):
    return 0.5 * v * (1.0 + lax.erf(v * (1.0 / math.sqrt(2.0))))


def _ada_ln(xf, ada_ref, norm_g_ref):
    ms = jnp.mean(xf * xf, axis=-1, keepdims=True)
    row = pl.ds(pl.program_id(0), 1)
    shift = ada_ref[row, 0:D_MODEL]
    scale = ada_ref[row, D_MODEL:2 * D_MODEL]
    return xf * lax.rsqrt(ms + EPS) * norm_g_ref[...] * (1.0 + scale) + shift


def _ada_kernel(c_ref, w_ref, b_ref, o_ref):
    o_ref[...] = _dot(_silu(c_ref[...]).astype(BF16), w_ref[...].astype(BF16)) + b_ref[...]


def _ada_call(c, w_ada, b_ada):
    batch = c.shape[0]
    return pl.pallas_call(
        _ada_kernel,
        grid=(3,),
        in_specs=[pl.BlockSpec((batch, D_MODEL), lambda n: (0, 0)),
                  pl.BlockSpec((D_MODEL, D_MODEL), lambda n: (0, n)),
                  pl.BlockSpec((1, D_MODEL), lambda n: (0, n))],
        out_specs=pl.BlockSpec((batch, D_MODEL), lambda n: (0, n)),
        out_shape=jax.ShapeDtypeStruct((batch, 3 * D_MODEL), F32),
        compiler_params=pltpu.CompilerParams(dimension_semantics=("arbitrary",)),
        name="ada",
    )(c, w_ada, b_ada.reshape(1, 3 * D_MODEL))


def _rope_expand_matrix():
    e = np.zeros((64, 3 * LANES), np.float32)
    for lane in range(LANES):
        dim = lane % HEAD_DIM
        if dim < ROPE_DIMS:
            j = dim % ROPE_HALF
            for part in range(3):
                e[part * 8 + j, lane] = 1.0
                if dim >= ROPE_HALF:
                    e[24 + part * 8 + j, LANES + lane] = 1.0
                else:
                    e[24 + part * 8 + j, 2 * LANES + lane] = -1.0
        else:
            e[48, lane] = 1.0
    return e


def _head_mean_matrix():
    head = np.arange(2 * LANES) // HEAD_DIM
    return (head[:, None] == head[None, :]).astype(np.float32) / HEAD_DIM


def _split3(a):
    hi = a.astype(BF16).astype(F32)
    r = a - hi
    mid = r.astype(BF16).astype(F32)
    return hi, mid, r - mid


def _store_residue_major(out, val, dilation, perm_scr):
    tm, width = val.shape
    if dilation == 1:
        out[0, 0] = val.astype(out.dtype)
        return
    for cb in range(width // LANES):
        perm_scr[cb] = val[:, cb * LANES:(cb + 1) * LANES]
    for r in range(dilation):
        for cb in range(width // LANES):
            out[0, r, :, cb * LANES:(cb + 1) * LANES] = (
                perm_scr[cb, pl.ds(r, tm // dilation, stride=dilation), :].astype(out.dtype))


def _qkv_kernel(x_ref, pos_ref, ada_ref, norm_g_ref, gq_ref, gk_ref, freq_ref, e_ref, bd_ref, *refs):
    n_chunks = QKV_COLS // ATTN_WIDTH
    w_chunk_refs, out_refs = refs[:n_chunks], refs[n_chunks:2 * n_chunks]
    perm_scr, w_ref = refs[2 * n_chunks:]

    @pl.when((pl.program_id(0) == 0) & (pl.program_id(1) == 0))
    def _():
        for c, chunk_ref in enumerate(w_chunk_refs):
            w_ref[:, c * ATTN_WIDTH:(c + 1) * ATTN_WIDTH] = chunk_ref[...].astype(BF16)

    xf = x_ref[0]
    tm = xf.shape[0]
    hb = _ada_ln(xf, ada_ref, norm_g_ref).astype(BF16)

    ang = freq_ref[...] * pos_ref[pl.ds(pl.program_id(0), 1), :].astype(F32)
    parts = _split3(jnp.cos(ang)) + _split3(jnp.sin(ang))
    lhs_t = jnp.concatenate(list(parts) + [jnp.ones((8, tm), F32), jnp.zeros((8, tm), F32)], axis=0)
    tab = lax.dot_general(lhs_t.astype(BF16), e_ref[...], (((0,), (0,)), ((), ())),
                          preferred_element_type=F32)
    cos_t, s1_t, s2_t = tab[:, 0:LANES], tab[:, LANES:2 * LANES], tab[:, 2 * LANES:3 * LANES]

    n_groups = len(ATTN_GROUPS)
    chunks = ([(g, which) for which in (0, 1) for g in range(n_groups)]
              + [(g, 2) for g in reversed(range(n_groups))])
    for g, which in chunks:
        dilation = ATTN_GROUPS[g][1]
        col0 = g * GROUP_COLS + which * ATTN_WIDTH
        zc = _dot(hb, w_ref[:, col0:col0 + ATTN_WIDTH])
        out = out_refs[3 * g + which]
        if which == 2:
            _store_residue_major(out, zc, dilation, perm_scr)
            continue
        gain = gq_ref[g:g + 1, :] * Q_SCALE if which == 0 else gk_ref[g:g + 1, :]
        cols = []
        for half in range(ATTN_WIDTH // (2 * LANES)):
            a2 = zc[:, half * 2 * LANES:(half + 1) * 2 * LANES]
            ms = _dot((a2 * a2).astype(BF16), bd_ref[...])
            a2n = a2 * lax.rsqrt(ms + EPS)
            for sub in range(2):
                an = a2n[:, sub * LANES:(sub + 1) * LANES] * gain
                cols.append(an * cos_t + pltpu.roll(an, ROPE_HALF, 1) * s1_t
                            + pltpu.roll(an, LANES - ROPE_HALF, 1) * s2_t)
        _store_residue_major(out, jnp.concatenate(cols, axis=1), dilation, perm_scr)


def _qkv_call(x, positions, ada, norm_g2, w_in, gq, gk, freq, e_mat, bd_mat):
    batch, seq, _ = x.shape
    const = lambda b, j: (0, 0)
    n_chunks = QKV_COLS // ATTN_WIDTH
    w_specs = [pl.BlockSpec((D_MODEL, ATTN_WIDTH), functools.partial(lambda c, b, j: (0, c), c),
                            pipeline_mode=pl.Buffered(1)) for c in range(n_chunks)]
    out_specs, out_shapes = [], []
    for _, dilation in ATTN_GROUPS:
        spec = pl.BlockSpec((1, dilation, QKV_TILE // dilation, ATTN_WIDTH), lambda b, j: (b, 0, j, 0))
        sds = jax.ShapeDtypeStruct((batch, dilation, seq // dilation, ATTN_WIDTH), BF16)
        out_specs += [spec] * 3
        out_shapes += [sds] * 3
    n_groups = len(ATTN_GROUPS)
    return pl.pallas_call(
        _qkv_kernel,
        grid=(batch, seq // QKV_TILE),
        in_specs=[
            pl.BlockSpec((1, QKV_TILE, D_MODEL), lambda b, j: (b, j, 0)),
            pl.BlockSpec((batch, QKV_TILE), lambda b, j: (0, j)),
            pl.BlockSpec((batch, 3 * D_MODEL), const),
            pl.BlockSpec((1, D_MODEL), const),
            pl.BlockSpec((n_groups, LANES), const),
            pl.BlockSpec((n_groups, LANES), const),
            pl.BlockSpec((8, 1), const),
            pl.BlockSpec((64, 3 * LANES), const),
            pl.BlockSpec((2 * LANES, 2 * LANES), const),
        ] + w_specs,
        out_specs=out_specs,
        out_shape=out_shapes,
        scratch_shapes=[pltpu.VMEM((ATTN_WIDTH // LANES, QKV_TILE, LANES), F32),
                        pltpu.VMEM((D_MODEL, QKV_COLS), BF16)],
        compiler_params=pltpu.CompilerParams(
            dimension_semantics=("arbitrary", "arbitrary"), vmem_limit_bytes=VMEM_LIMIT_BYTES),
        name="qkv",
    )(x, positions, ada, norm_g2, gq, gk, freq, e_mat, bd_mat, *([w_in] * n_chunks))


def _attn_kernel(q_ref, k_ref, v_ref, *refs, dilation, n_blk, token_order_out, n_cast):
    cast_in, (o_ref, lse_ref) = refs[:n_cast], refs[n_cast:n_cast + 2]
    cast_out, scratch = refs[n_cast + 2:2 * n_cast + 2], refs[2 * n_cast + 2:]
    for src, dst in zip(cast_in, cast_out):
        dst[...] = src[...].astype(dst.dtype)

    row = lax.broadcasted_iota(jnp.int32, (Q_BLOCK, LANES), 0)
    col = lax.broadcasted_iota(jnp.int32, (Q_BLOCK, LANES), 1)
    cur_ok = col <= row
    prev_ok = col >= row
    lane_lo = col < HEAD_DIM
    ones_lo = jnp.where(lane_lo, 1.0, 0.0).astype(BF16)
    ones_hi = jnp.where(lane_lo, 0.0, 1.0).astype(BF16)
    n_pairs = ATTN_WIDTH // LANES
    stage_in_vmem = n_blk > 1
    scratch = list(scratch)
    if stage_in_vmem:
        p_scr, m_scr = scratch.pop(0), scratch.pop(0)
    if dilation > 1 and token_order_out:
        o_plane, lse_plane = scratch

    def split_heads(a):
        zero = jnp.zeros_like(a)
        lo = lax.broadcasted_iota(jnp.int32, a.shape, 1) < HEAD_DIM
        return jnp.concatenate([jnp.where(lo, a, zero), jnp.where(lo, zero, a)], axis=0)

    def window(ref, r, r0, first, lanes):
        if first:
            return ref[0, r, r0:r0 + Q_BLOCK, lanes]
        return ref[0, r, r0 - Q_BLOCK:r0 + Q_BLOCK, lanes]

    def scores(r, r0):
        first = r0 == 0
        ok_head = cur_ok if first else jnp.concatenate([prev_ok, cur_ok], axis=1)
        ok = jnp.concatenate([ok_head, ok_head], axis=1)
        n_keys = ok_head.shape[1]
        stats = []
        for hp in range(n_pairs):
            lanes = slice(hp * LANES, (hp + 1) * LANES)
            q_pair = q_ref[0, r, r0:r0 + Q_BLOCK, lanes]
            k2 = split_heads(window(k_ref, r, r0, first, lanes))
            s = lax.dot_general(q_pair, k2, (((1,), (1,)), ((), ())), preferred_element_type=F32)
            s = jnp.where(ok, s, -jnp.inf)
            m0 = jnp.max(s[:, :n_keys], axis=-1, keepdims=True)
            m1 = jnp.max(s[:, n_keys:], axis=-1, keepdims=True)
            p = jnp.concatenate([jnp.exp2(s[:, :n_keys] - m0), jnp.exp2(s[:, n_keys:] - m1)], axis=1)
            p, m_ln2 = p.astype(BF16), jnp.where(lane_lo, m0, m1) * math.log(2.0)
            if stage_in_vmem:
                p_scr[hp, :, 0:2 * n_keys] = p
                m_scr[hp] = m_ln2
            stats.append((p, m_ln2))
        return stats

    def values(r, r0, stats):
        first = r0 == 0
        reps = 1 if first else 2
        ind = jnp.concatenate([ones_lo] * reps + [ones_hi] * reps, axis=0)
        for hp, (p, m_ln2) in enumerate(stats):
            if stage_in_vmem:
                p, m_ln2 = p_scr[hp, :, 0:2 * reps * Q_BLOCK], m_scr[hp]
            lanes = slice(hp * LANES, (hp + 1) * LANES)
            v2 = jnp.concatenate([split_heads(window(v_ref, r, r0, first, lanes)), ind], axis=1)
            o2 = _dot(p, v2)
            den = o2[:, LANES:]
            o_pair = o2[:, :LANES] * (1.0 / den)
            lse_pair = m_ln2 + jnp.log(den)
            if dilation == 1:
                o_ref[0, r0:r0 + Q_BLOCK, lanes] = o_pair.astype(o_ref.dtype)
                lse_ref[0, r0:r0 + Q_BLOCK, lanes] = lse_pair
            elif not token_order_out:
                o_ref[0, r, r0:r0 + Q_BLOCK, lanes] = o_pair.astype(o_ref.dtype)
                lse_ref[0, r, r0:r0 + Q_BLOCK, lanes] = lse_pair
            else:
                rows = pl.ds(r + r0 * dilation, Q_BLOCK, stride=dilation)
                o_plane[hp, rows, :] = o_pair
                lse_plane[hp, rows, :] = lse_pair

    blocks = [(r, blk * Q_BLOCK) for r in range(dilation) for blk in range(n_blk)]
    stats = scores(*blocks[0])
    for prev, cur in zip(blocks[:-1], blocks[1:]):
        if stage_in_vmem:
            values(*prev, stats)
            stats = scores(*cur)
        else:
            next_stats = scores(*cur)
            values(*prev, stats)
            stats = next_stats
    values(*blocks[-1], stats)

    if dilation > 1 and token_order_out:
        for hp in range(n_pairs):
            lanes = slice(hp * LANES, (hp + 1) * LANES)
            o_ref[0, :, lanes] = o_plane[hp].astype(o_ref.dtype)
            lse_ref[0, :, lanes] = lse_plane[hp]


def _attn_call(g, dilation, q, k, v, token_order_out, casts=()):
    batch, _, res_len, _ = q.shape
    cast_in_specs, cast_out_specs, cast_out_shapes = [], [], []
    for arr, block, first, out_dims in casts:
        axis = 0 if block[0] != out_dims[0] else 1
        assert out_dims[axis] == batch * block[axis]
        def index(b, axis=axis, shift=0):
            return (b + shift, 0) if axis == 0 else (0, b + shift)
        cast_in_specs.append(pl.BlockSpec(block, functools.partial(index, shift=first)))
        cast_out_specs.append(pl.BlockSpec(block, index))
        cast_out_shapes.append(jax.ShapeDtypeStruct(out_dims, BF16))
    seq = dilation * res_len
    in_spec = pl.BlockSpec((1, dilation, res_len, ATTN_WIDTH), lambda b: (b, 0, 0, 0))
    n_pairs = ATTN_WIDTH // LANES
    scratch = []
    if res_len > Q_BLOCK:
        scratch += [pltpu.VMEM((n_pairs, Q_BLOCK, 4 * Q_BLOCK), BF16),
                    pltpu.VMEM((n_pairs, Q_BLOCK, LANES), F32)]
    if token_order_out or dilation == 1:
        out_spec, out_dims = pl.BlockSpec((1, seq, ATTN_WIDTH), lambda b: (b, 0, 0)), (batch, seq, ATTN_WIDTH)
        if dilation > 1:
            scratch += [pltpu.VMEM((n_pairs, seq, LANES), F32)] * 2
    else:
        out_spec, out_dims = in_spec, q.shape
    return pl.pallas_call(
        functools.partial(_attn_kernel, dilation=dilation, n_blk=res_len // Q_BLOCK,
                          token_order_out=token_order_out, n_cast=len(casts)),
        grid=(batch,),
        in_specs=[in_spec, in_spec, in_spec] + cast_in_specs,
        out_specs=[out_spec, out_spec] + cast_out_specs,
        out_shape=[jax.ShapeDtypeStruct(out_dims, BF16), jax.ShapeDtypeStruct(out_dims, F32)] + cast_out_shapes,
        scratch_shapes=scratch,
        compiler_params=pltpu.CompilerParams(
            dimension_semantics=("arbitrary",), vmem_limit_bytes=VMEM_LIMIT_BYTES),
        name=f"attn_g{g}",
    )(q, k, v, *[arr for arr, _, _, _ in casts])


def _token_order(ref, scr):
    _, dilation, n, width = ref.shape
    cols = []
    for cb in range(width // LANES):
        for r in range(dilation):
            scr[cb, pl.ds(r, n, stride=dilation), :] = ref[0, r, :, cb * LANES:(cb + 1) * LANES].astype(F32)
        cols.append(scr[cb])
    return jnp.concatenate(cols, axis=1)


def _merge_kernel(x_ref, ada_ref, norm_g_ref, w_ref,
                  o0_ref, o1_ref, o2_ref, l0_ref, l1_ref, l2_ref,
                  ln_g_ref, ln_b_ref, ws_ref, bias_ref, wa_ref, wb_ref, wo_ref, out_ref, o2_scr, l2_scr):
    xf = x_ref[0]
    tm = xf.shape[0]
    h = _ada_ln(xf, ada_ref, norm_g_ref)
    z = _dot(h.astype(BF16), w_ref[...])
    off = 0
    gate_a = z[:, off:off + ATTN_WIDTH]; off += ATTN_WIDTH
    u = _gelu_exact(z[:, off:off + GMLP_WIDTH]); off += GMLP_WIDTH
    v = _gelu_exact(z[:, off:off + GMLP_WIDTH]); off += GMLP_WIDTH
    gate_b = z[:, off:off + GMLP_WIDTH]; off += GMLP_WIDTH
    merge_a = z[:, off:off + D_MODEL]; off += D_MODEL
    merge_b = z[:, off:off + D_MODEL]

    l0, l1, l2 = l0_ref[0], l1_ref[0], _token_order(l2_ref, l2_scr)
    lmax = jnp.maximum(jnp.maximum(l0, l1), l2)
    e0, e1, e2 = jnp.exp(l0 - lmax), jnp.exp(l1 - lmax), jnp.exp(l2 - lmax)
    attn = (e0 * o0_ref[0].astype(F32) + e1 * o1_ref[0].astype(F32) + e2 * _token_order(o2_ref, o2_scr)) \
        / (e0 + e1 + e2)
    y_a = attn * _silu(gate_a)

    mu = jnp.mean(v, axis=-1, keepdims=True)
    vc = v - mu
    var = jnp.mean(vc * vc, axis=-1, keepdims=True)
    v_ln = (vc * lax.rsqrt(var + EPS) * ln_g_ref[...] + ln_b_ref[...]).astype(BF16)
    row = lax.broadcasted_iota(jnp.int32, (CHUNK, CHUNK), 0)
    col = lax.broadcasted_iota(jnp.int32, (CHUNK, CHUNK), 1)
    causal = row >= col
    lane_lo = col < GMLP_GROUP_DIM
    n_chunks = tm // CHUNK
    sv_cols = []
    for gp in range(GMLP_WIDTH // LANES):
        lanes = slice(gp * LANES, (gp + 1) * LANES)
        rhs = jnp.concatenate([v_ln[c * CHUNK:(c + 1) * CHUNK, lanes] for c in range(n_chunks)], axis=1)
        res = []
        for half in range(2):
            w_s = jnp.where(causal, ws_ref[2 * gp + half], 0.0).astype(BF16)
            res.append(_dot(w_s, rhs))
        sel = jnp.concatenate(
            [jnp.where(lane_lo, res[0][:, c * LANES:(c + 1) * LANES], res[1][:, c * LANES:(c + 1) * LANES])
             + bias_ref[:, lanes] for c in range(n_chunks)], axis=0)
        sv_cols.append(sel)
    sv = jnp.concatenate(sv_cols, axis=1)
    y_b = u * sv * _silu(gate_b)

    merged = (_sigmoid(merge_a) * _dot(y_a.astype(BF16), wa_ref[...])
              + _sigmoid(merge_b) * _dot(y_b.astype(BF16), wb_ref[...]))
    out = _dot(merged.astype(BF16), wo_ref[...])
    gate = ada_ref[pl.ds(pl.program_id(0), 1), 2 * D_MODEL:3 * D_MODEL]
    out_ref[0] = xf + gate * out


def _merge_call(x, ada, norm_g2, w_rest, os_, lses, ln_g, ln_b, w_spatial, bias_tab, wa, wb, wo):
    batch, seq, _ = x.shape
    const2 = lambda b, i: (0, 0)
    tile = lambda width: pl.BlockSpec((1, MERGE_TILE, width), lambda b, i: (b, i, 0))
    last_dilation = ATTN_GROUPS[-1][1]
    res_tile = pl.BlockSpec((1, last_dilation, MERGE_TILE // last_dilation, ATTN_WIDTH),
                            lambda b, i: (b, 0, i, 0))
    staging = pltpu.VMEM((ATTN_WIDTH // LANES, MERGE_TILE, LANES), F32)
    return pl.pallas_call(
        _merge_kernel,
        grid=(batch, seq // MERGE_TILE),
        in_specs=[
            tile(D_MODEL),
            pl.BlockSpec((batch, 3 * D_MODEL), const2),
            pl.BlockSpec((1, D_MODEL), const2),
            _resident((D_MODEL, REST_COLS)),
            tile(ATTN_WIDTH), tile(ATTN_WIDTH), res_tile,
            tile(ATTN_WIDTH), tile(ATTN_WIDTH), res_tile,
            pl.BlockSpec((1, GMLP_WIDTH), const2),
            pl.BlockSpec((1, GMLP_WIDTH), const2),
            pl.BlockSpec((GMLP_GROUPS, CHUNK, CHUNK), lambda b, i: (0, 0, 0)),
            pl.BlockSpec((CHUNK, GMLP_WIDTH), const2),
            _resident((ATTN_WIDTH, D_MODEL)),
            _resident((GMLP_WIDTH, D_MODEL)),
            _resident((D_MODEL, D_MODEL)),
        ],
        out_specs=tile(D_MODEL),
        out_shape=jax.ShapeDtypeStruct(x.shape, x.dtype),
        scratch_shapes=[staging, staging],
        compiler_params=pltpu.CompilerParams(
            dimension_semantics=("arbitrary", "arbitrary"), vmem_limit_bytes=VMEM_LIMIT_BYTES),
        name="merge",
    )(x, ada, norm_g2, w_rest, *os_, *lses, ln_g, ln_b, w_spatial, bias_tab, wa, wb, wo)


def kernel(x, c, positions, norm_g, w_ada, b_ada, w_in, q_norm_g, k_norm_g, sgu_ln_g, sgu_ln_b,
           w_spatial, b_spatial, w_branch_a, w_branch_b, w_out):
    batch, seq, d_model = x.shape
    assert d_model == D_MODEL and seq % QKV_TILE == 0 and seq % MERGE_TILE == 0
    assert w_in.shape == (D_MODEL, QKV_COLS + REST_COLS)

    ada = _ada_call(c, w_ada, b_ada)
    norm_g2 = norm_g.reshape(1, D_MODEL)

    freq = (ROPE_THETA ** (-np.arange(0, ROPE_DIMS, 2, dtype=np.float32) / ROPE_DIMS)).astype(np.float32)
    freq = jnp.asarray(freq.reshape(ROPE_HALF, 1))
    e_mat = jnp.asarray(_rope_expand_matrix(), dtype=BF16)
    bd_mat = jnp.asarray(_head_mean_matrix(), dtype=BF16)

    gq = jnp.tile(q_norm_g, (1, LANES // HEAD_DIM))
    gk = jnp.tile(k_norm_g, (1, LANES // HEAD_DIM))
    qkv = _qkv_call(x, positions, ada, norm_g2, w_in, gq, gk, freq, e_mat, bd_mat)
    rest_block = REST_COLS // batch
    assert QKV_COLS % rest_block == 0
    casts = [
        [(w_in, (D_MODEL, rest_block), QKV_COLS // rest_block, (D_MODEL, REST_COLS))],
        [(w_branch_a, (ATTN_WIDTH // batch, D_MODEL), 0, w_branch_a.shape),
         (w_branch_b, (GMLP_WIDTH // batch, D_MODEL), 0, w_branch_b.shape),
         (w_out, (D_MODEL // batch, D_MODEL), 0, w_out.shape)],
        [],
    ]
    outs, lses, weights = [], [], []
    for g, (window, dilation) in enumerate(ATTN_GROUPS):
        assert window // dilation == Q_BLOCK
        o, lse, *w_bf16 = _attn_call(g, dilation, *qkv[3 * g:3 * g + 3],
                                     token_order_out=g + 1 < len(ATTN_GROUPS), casts=casts[g])
        outs.append(o)
        lses.append(lse)
        weights += w_bf16
    w_rest_bf16, wa_bf16, wb_bf16, wo_bf16 = weights

    bias_tab = jnp.repeat(b_spatial.T, GMLP_GROUP_DIM, axis=1)
    return _merge_call(x, ada, norm_g2, w_rest_bf16, outs, lses,
                       sgu_ln_g.reshape(1, GMLP_WIDTH), sgu_ln_b.reshape(1, GMLP_WIDTH),
                       w_spatial, bias_tab, wa_bf16, wb_bf16, wo_bf16)
```

```python
import functools
import math

import jax
import jax.numpy as jnp
import numpy as np
from jax import lax
from jax.experimental import pallas as pl
from jax.experimental.pallas import tpu as pltpu

D_MODEL = 1024
HEAD_DIM = 64
HEADS = 8
ATTN_GROUPS = ((128, 1), (512, 4), (2048, 16))
ATTN_WIDTH = HEADS * HEAD_DIM
Q_BLOCK = 128
ROPE_THETA = 500000.0
ROPE_DIMS = HEAD_DIM // 4
ROPE_HALF = ROPE_DIMS // 2
GMLP_WIDTH = 512
GMLP_GROUPS = 8
GMLP_GROUP_DIM = GMLP_WIDTH // GMLP_GROUPS
CHUNK = 128
EPS = 1e-6
GROUP_COLS = 3 * ATTN_WIDTH
QKV_COLS = len(ATTN_GROUPS) * GROUP_COLS
REST_COLS = ATTN_WIDTH + 3 * GMLP_WIDTH + 2 * D_MODEL

Q_SCALE = math.log2(math.e) / math.sqrt(HEAD_DIM)

LANES = 128
QKV_TILE = 1024
MERGE_TILE = 512
VMEM_LIMIT_BYTES = 56 * 1024 * 1024

F32 = jnp.float32
BF16 = jnp.bfloat16


def _dot(a, b):
    return jnp.dot(a, b, preferred_element_type=F32)


def _resident(shape):
    return pl.BlockSpec(shape, lambda *_: (0, 0), pipeline_mode=pl.Buffered(1))


def _silu(v):
    return v * (1.0 / (1.0 + jnp.exp(-v)))


def _sigmoid(v):
    return 1.0 / (1.0 + jnp.exp(-v))


def _gelu_exact(v):
    return 0.5 * v * (1.0 + lax.erf(v * (1.0 / math.sqrt(2.0))))


def _ada_ln(xf, ada_ref, norm_g_ref):
    ms = jnp.mean(xf * xf, axis=-1, keepdims=True)
    row = pl.ds(pl.program_id(0), 1)
    shift = ada_ref[row, 0:D_MODEL]
    scale = ada_ref[row, D_MODEL:2 * D_MODEL]
    return xf * lax.rsqrt(ms + EPS) * norm_g_ref[...] * (1.0 + scale) + shift


def _ada_kernel(c_ref, w_ref, b_ref, o_ref):
    o_ref[...] = _dot(_silu(c_ref[...]).astype(BF16), w_ref[...].astype(BF16)) + b_ref[...]


def _ada_call(c, w_ada, b_ada):
    batch = c.shape[0]
    return pl.pallas_call(
        _ada_kernel,
        grid=(3,),
        in_specs=[pl.BlockSpec((batch, D_MODEL), lambda n: (0, 0)),
                  pl.BlockSpec((D_MODEL, D_MODEL), lambda n: (0, n)),
                  pl.BlockSpec((1, D_MODEL), lambda n: (0, n))],
        out_specs=pl.BlockSpec((batch, D_MODEL), lambda n: (0, n)),
        out_shape=jax.ShapeDtypeStruct((batch, 3 * D_MODEL), F32),
        compiler_params=pltpu.CompilerParams(dimension_semantics=("arbitrary",)),
        name="ada",
    )(c, w_ada, b_ada.reshape(1, 3 * D_MODEL))


def _rope_expand_matrix():
    e = np.zeros((64, 3 * LANES), np.float32)
    for lane in range(LANES):
        dim = lane % HEAD_DIM
        if dim < ROPE_DIMS:
            j = dim % ROPE_HALF
            for part in range(3):
                e[part * 8 + j, lane] = 1.0
                if dim >= ROPE_HALF:
                    e[24 + part * 8 + j, LANES + lane] = 1.0
                else:
                    e[24 + part * 8 + j, 2 * LANES + lane] = -1.0
        else:
            e[48, lane] = 1.0
    return e


def _head_mean_matrix():
    head = np.arange(2 * LANES) // HEAD_DIM
    return (head[:, None] == head[None, :]).astype(np.float32) / HEAD_DIM


def _split3(a):
    hi = a.astype(BF16).astype(F32)
    r = a - hi
    mid = r.astype(BF16).astype(F32)
    return hi, mid, r - mid


def _store_residue_major(out, val, dilation, perm_scr):
    tm, width = val.shape
    if dilation == 1:
        out[0, 0] = val.astype(out.dtype)
        return
    for cb in range(width // LANES):
        perm_scr[cb] = val[:, cb * LANES:(cb + 1) * LANES]
    for r in range(dilation):
        for cb in range(width // LANES):
            out[0, r, :, cb * LANES:(cb + 1) * LANES] = (
                perm_scr[cb, pl.ds(r, tm // dilation, stride=dilation), :].astype(out.dtype))


def _qkv_kernel(x_ref, pos_ref, ada_ref, norm_g_ref, gq_ref, gk_ref, freq_ref, e_ref, bd_ref, *refs):
    n_chunks = QKV_COLS // ATTN_WIDTH
    w_ref, out_refs, perm_scr = refs[0], refs[1:1 + n_chunks], refs[1 + n_chunks]

    xf = x_ref[0]
    tm = xf.shape[0]
    hb = _ada_ln(xf, ada_ref, norm_g_ref).astype(BF16)

    ang = freq_ref[...] * pos_ref[pl.ds(pl.program_id(0), 1), :].astype(F32)
    parts = _split3(jnp.cos(ang)) + _split3(jnp.sin(ang))
    lhs_t = jnp.concatenate(list(parts) + [jnp.ones((8, tm), F32), jnp.zeros((8, tm), F32)], axis=0)
    tab = lax.dot_general(lhs_t.astype(BF16), e_ref[...], (((0,), (0,)), ((), ())),
                          preferred_element_type=F32)
    cos_t, s1_t, s2_t = tab[:, 0:LANES], tab[:, LANES:2 * LANES], tab[:, 2 * LANES:3 * LANES]

    n_groups = len(ATTN_GROUPS)
    chunks = ([(g, which) for which in (0, 1) for g in range(n_groups)]
              + [(g, 2) for g in reversed(range(n_groups))])
    for g, which in chunks:
        dilation = ATTN_GROUPS[g][1]
        col0 = g * GROUP_COLS + which * ATTN_WIDTH
        zc = _dot(hb, w_ref[:, col0:col0 + ATTN_WIDTH])
        out = out_refs[3 * g + which]
        if which == 2:
            _store_residue_major(out, zc, dilation, perm_scr)
            continue
        gain = gq_ref[g:g + 1, :] * Q_SCALE if which == 0 else gk_ref[g:g + 1, :]
        cols = []
        for half in range(ATTN_WIDTH // (2 * LANES)):
            a2 = zc[:, half * 2 * LANES:(half + 1) * 2 * LANES]
            ms = _dot((a2 * a2).astype(BF16), bd_ref[...])
            a2n = a2 * lax.rsqrt(ms + EPS)
            for sub in range(2):
                an = a2n[:, sub * LANES:(sub + 1) * LANES] * gain
                cols.append(an * cos_t + pltpu.roll(an, ROPE_HALF, 1) * s1_t
                            + pltpu.roll(an, LANES - ROPE_HALF, 1) * s2_t)
        _store_residue_major(out, jnp.concatenate(cols, axis=1), dilation, perm_scr)


def _qkv_call(x, positions, ada, norm_g2, w_in, gq, gk, freq, e_mat, bd_mat):
    batch, seq, _ = x.shape
    const = lambda b, j: (0, 0)
    n_chunks = QKV_COLS // ATTN_WIDTH
    w_specs = [pl.BlockSpec((D_MODEL, ATTN_WIDTH), functools.partial(lambda c, b, j: (0, c), c),
                            pipeline_mode=pl.Buffered(1)) for c in range(n_chunks)]
    out_specs, out_shapes = [], []
    for _, dilation in ATTN_GROUPS:
        spec = pl.BlockSpec((1, dilation, QKV_TILE // dilation, ATTN_WIDTH), lambda b, j: (b, 0, j, 0))
        sds = jax.ShapeDtypeStruct((batch, dilation, seq // dilation, ATTN_WIDTH), BF16)
        out_specs += [spec] * 3
        out_shapes += [sds] * 3
    n_groups = len(ATTN_GROUPS)
    return pl.pallas_call(
        _qkv_kernel,
        grid=(batch, seq // QKV_TILE),
        in_specs=[
            pl.BlockSpec((1, QKV_TILE, D_MODEL), lambda b, j: (b, j, 0)),
            pl.BlockSpec((batch, QKV_TILE), lambda b, j: (0, j)),
            pl.BlockSpec((batch, 3 * D_MODEL), const),
            pl.BlockSpec((1, D_MODEL), const),
            pl.BlockSpec((n_groups, LANES), const),
            pl.BlockSpec((n_groups, LANES), const),
            pl.BlockSpec((8, 1), const),
            pl.BlockSpec((64, 3 * LANES), const),
            pl.BlockSpec((2 * LANES, 2 * LANES), const),
            _resident((D_MODEL, QKV_COLS)),
        ],
        out_specs=out_specs,
        out_shape=out_shapes,
        scratch_shapes=[pltpu.VMEM((ATTN_WIDTH // LANES, QKV_TILE, LANES), F32)],
        compiler_params=pltpu.CompilerParams(
            dimension_semantics=("arbitrary", "arbitrary"), vmem_limit_bytes=VMEM_LIMIT_BYTES),
        name="qkv",
    )(x, positions, ada, norm_g2, gq, gk, freq, e_mat, bd_mat, w_in[:, :QKV_COLS].astype(BF16))


def _attn_kernel(q_ref, k_ref, v_ref, *refs, dilation, n_blk, token_order_out, n_cast):
    cast_in, (o_ref, lse_ref) = refs[:n_cast], refs[n_cast:n_cast + 2]
    cast_out, scratch = refs[n_cast + 2:2 * n_cast + 2], refs[2 * n_cast + 2:]
    for src, dst in zip(cast_in, cast_out):
        dst[...] = src[...].astype(dst.dtype)

    row = lax.broadcasted_iota(jnp.int32, (Q_BLOCK, LANES), 0)
    col = lax.broadcasted_iota(jnp.int32, (Q_BLOCK, LANES), 1)
    cur_ok = col <= row
    prev_ok = col >= row
    lane_lo = col < HEAD_DIM
    ones_lo = jnp.where(lane_lo, 1.0, 0.0).astype(BF16)
    ones_hi = jnp.where(lane_lo, 0.0, 1.0).astype(BF16)
    n_pairs = ATTN_WIDTH // LANES
    stage_in_vmem = n_blk > 1
    scratch = list(scratch)
    if stage_in_vmem:
        p_scr, m_scr = scratch.pop(0), scratch.pop(0)
    if dilation > 1 and token_order_out:
        o_plane, lse_plane = scratch

    def split_heads(a):
        zero = jnp.zeros_like(a)
        lo = lax.broadcasted_iota(jnp.int32, a.shape, 1) < HEAD_DIM
        return jnp.concatenate([jnp.where(lo, a, zero), jnp.where(lo, zero, a)], axis=0)

    def window(ref, r, r0, first, lanes):
        if first:
            return ref[0, r, r0:r0 + Q_BLOCK, lanes]
        return ref[0, r, r0 - Q_BLOCK:r0 + Q_BLOCK, lanes]

    def scores(r, r0):
        first = r0 == 0
        ok_head = cur_ok if first else jnp.concatenate([prev_ok, cur_ok], axis=1)
        ok = jnp.concatenate([ok_head, ok_head], axis=1)
        n_keys = ok_head.shape[1]
        stats = []
        for hp in range(n_pairs):
            lanes = slice(hp * LANES, (hp + 1) * LANES)
            q_pair = q_ref[0, r, r0:r0 + Q_BLOCK, lanes]
            k2 = split_heads(window(k_ref, r, r0, first, lanes))
            s = lax.dot_general(q_pair, k2, (((1,), (1,)), ((), ())), preferred_element_type=F32)
            s = jnp.where(ok, s, -jnp.inf)
            m0 = jnp.max(s[:, :n_keys], axis=-1, keepdims=True)
            m1 = jnp.max(s[:, n_keys:], axis=-1, keepdims=True)
            p = jnp.concatenate([jnp.exp2(s[:, :n_keys] - m0), jnp.exp2(s[:, n_keys:] - m1)], axis=1)
            p, m_ln2 = p.astype(BF16), jnp.where(lane_lo, m0, m1) * math.log(2.0)
            if stage_in_vmem:
                p_scr[hp, :, 0:2 * n_keys] = p
                m_scr[hp] = m_ln2
            stats.append((p, m_ln2))
        return stats

    def values(r, r0, stats):
        first = r0 == 0
        reps = 1 if first else 2
        ind = jnp.concatenate([ones_lo] * reps + [ones_hi] * reps, axis=0)
        for hp, (p, m_ln2) in enumerate(stats):
            if stage_in_vmem:
                p, m_ln2 = p_scr[hp, :, 0:2 * reps * Q_BLOCK], m_scr[hp]
            lanes = slice(hp * LANES, (hp + 1) * LANES)
            v2 = jnp.concatenate([split_heads(window(v_ref, r, r0, first, lanes)), ind], axis=1)
            o2 = _dot(p, v2)
            den = o2[:, LANES:]
            o_pair = o2[:, :LANES] * (1.0 / den)
            lse_pair = m_ln2 + jnp.log(den)
            if dilation == 1:
                o_ref[0, r0:r0 + Q_BLOCK, lanes] = o_pair.astype(o_ref.dtype)
                lse_ref[0, r0:r0 + Q_BLOCK, lanes] = lse_pair
            elif not token_order_out:
                o_ref[0, r, r0:r0 + Q_BLOCK, lanes] = o_pair.astype(o_ref.dtype)
                lse_ref[0, r, r0:r0 + Q_BLOCK, lanes] = lse_pair
            else:
                rows = pl.ds(r + r0 * dilation, Q_BLOCK, stride=dilation)
                o_plane[hp, rows, :] = o_pair
                lse_plane[hp, rows, :] = lse_pair

    blocks = [(r, blk * Q_BLOCK) for r in range(dilation) for blk in range(n_blk)]
    stats = scores(*blocks[0])
    for prev, cur in zip(blocks[:-1], blocks[1:]):
        if stage_in_vmem:
            values(*prev, stats)
            stats = scores(*cur)
        else:
            next_stats = scores(*cur)
            values(*prev, stats)
            stats = next_stats
    values(*blocks[-1], stats)

    if dilation > 1 and token_order_out:
        for hp in range(n_pairs):
            lanes = slice(hp * LANES, (hp + 1) * LANES)
            o_ref[0, :, lanes] = o_plane[hp].astype(o_ref.dtype)
            lse_ref[0, :, lanes] = lse_plane[hp]


def _attn_call(g, dilation, q, k, v, token_order_out, casts=()):
    batch, _, res_len, _ = q.shape
    cast_in_specs, cast_out_specs, cast_out_shapes = [], [], []
    for arr, block, first, out_dims in casts:
        axis = 0 if block[0] != out_dims[0] else 1
        assert out_dims[axis] == batch * block[axis]
        def index(b, axis=axis, shift=0):
            return (b + shift, 0) if axis == 0 else (0, b + shift)
        cast_in_specs.append(pl.BlockSpec(block, functools.partial(index, shift=first)))
        cast_out_specs.append(pl.BlockSpec(block, index))
        cast_out_shapes.append(jax.ShapeDtypeStruct(out_dims, BF16))
    seq = dilation * res_len
    in_spec = pl.BlockSpec((1, dilation, res_len, ATTN_WIDTH), lambda b: (b, 0, 0, 0))
    n_pairs = ATTN_WIDTH // LANES
    scratch = []
    if res_len > Q_BLOCK:
        scratch += [pltpu.VMEM((n_pairs, Q_BLOCK, 4 * Q_BLOCK), BF16),
                    pltpu.VMEM((n_pairs, Q_BLOCK, LANES), F32)]
    if token_order_out or dilation == 1:
        out_spec, out_dims = pl.BlockSpec((1, seq, ATTN_WIDTH), lambda b: (b, 0, 0)), (batch, seq, ATTN_WIDTH)
        if dilation > 1:
            scratch += [pltpu.VMEM((n_pairs, seq, LANES), F32)] * 2
    else:
        out_spec, out_dims = in_spec, q.shape
    return pl.pallas_call(
        functools.partial(_attn_kernel, dilation=dilation, n_blk=res_len // Q_BLOCK,
                          token_order_out=token_order_out, n_cast=len(casts)),
        grid=(batch,),
        in_specs=[in_spec, in_spec, in_spec] + cast_in_specs,
        out_specs=[out_spec, out_spec] + cast_out_specs,
        out_shape=[jax.ShapeDtypeStruct(out_dims, BF16), jax.ShapeDtypeStruct(out_dims, F32)] + cast_out_shapes,
        scratch_shapes=scratch,
        compiler_params=pltpu.CompilerParams(
            dimension_semantics=("arbitrary",), vmem_limit_bytes=VMEM_LIMIT_BYTES),
        name=f"attn_g{g}",
    )(q, k, v, *[arr for arr, _, _, _ in casts])


def _token_order(ref, scr):
    _, dilation, n, width = ref.shape
    cols = []
    for cb in range(width // LANES):
        for r in range(dilation):
            scr[cb, pl.ds(r, n, stride=dilation), :] = ref[0, r, :, cb * LANES:(cb + 1) * LANES].astype(F32)
        cols.append(scr[cb])
    return jnp.concatenate(cols, axis=1)


def _merge_kernel(x_ref, ada_ref, norm_g_ref, w_ref,
                  o0_ref, o1_ref, o2_ref, l0_ref, l1_ref, l2_ref,
                  ln_g_ref, ln_b_ref, ws_ref, bias_ref, wa_ref, wb_ref, wo_ref, out_ref, o2_scr, l2_scr):
    xf = x_ref[0]
    tm = xf.shape[0]
    h = _ada_ln(xf, ada_ref, norm_g_ref)
    z = _dot(h.astype(BF16), w_ref[...])
    off = 0
    gate_a = z[:, off:off + ATTN_WIDTH]; off += ATTN_WIDTH
    u = _gelu_exact(z[:, off:off + GMLP_WIDTH]); off += GMLP_WIDTH
    v = _gelu_exact(z[:, off:off + GMLP_WIDTH]); off += GMLP_WIDTH
    gate_b = z[:, off:off + GMLP_WIDTH]; off += GMLP_WIDTH
    merge_a = z[:, off:off + D_MODEL]; off += D_MODEL
    merge_b = z[:, off:off + D_MODEL]

    l0, l1, l2 = l0_ref[0], l1_ref[0], _token_order(l2_ref, l2_scr)
    lmax = jnp.maximum(jnp.maximum(l0, l1), l2)
    e0, e1, e2 = jnp.exp(l0 - lmax), jnp.exp(l1 - lmax), jnp.exp(l2 - lmax)
    attn = (e0 * o0_ref[0].astype(F32) + e1 * o1_ref[0].astype(F32) + e2 * _token_order(o2_ref, o2_scr)) \
        / (e0 + e1 + e2)
    y_a = attn * _silu(gate_a)

    mu = jnp.mean(v, axis=-1, keepdims=True)
    vc = v - mu
    var = jnp.mean(vc * vc, axis=-1, keepdims=True)
    v_ln = (vc * lax.rsqrt(var + EPS) * ln_g_ref[...] + ln_b_ref[...]).astype(BF16)
    row = lax.broadcasted_iota(jnp.int32, (CHUNK, CHUNK), 0)
    col = lax.broadcasted_iota(jnp.int32, (CHUNK, CHUNK), 1)
    causal = row >= col
    lane_lo = col < GMLP_GROUP_DIM
    n_chunks = tm // CHUNK
    sv_cols = []
    for gp in range(GMLP_WIDTH // LANES):
        lanes = slice(gp * LANES, (gp + 1) * LANES)
        rhs = jnp.concatenate([v_ln[c * CHUNK:(c + 1) * CHUNK, lanes] for c in range(n_chunks)], axis=1)
        res = []
        for half in range(2):
            w_s = jnp.where(causal, ws_ref[2 * gp + half], 0.0).astype(BF16)
            res.append(_dot(w_s, rhs))
        sel = jnp.concatenate(
            [jnp.where(lane_lo, res[0][:, c * LANES:(c + 1) * LANES], res[1][:, c * LANES:(c + 1) * LANES])
             + bias_ref[:, lanes] for c in range(n_chunks)], axis=0)
        sv_cols.append(sel)
    sv = jnp.concatenate(sv_cols, axis=1)
    y_b = u * sv * _silu(gate_b)

    merged = (_sigmoid(merge_a) * _dot(y_a.astype(BF16), wa_ref[...])
              + _sigmoid(merge_b) * _dot(y_b.astype(BF16), wb_ref[...]))
    out = _dot(merged.astype(BF16), wo_ref[...])
    gate = ada_ref[pl.ds(pl.program_id(0), 1), 2 * D_MODEL:3 * D_MODEL]
    out_ref[0] = xf + gate * out


def _merge_call(x, ada, norm_g2, w_rest, os_, lses, ln_g, ln_b, w_spatial, bias_tab, wa, wb, wo):
    batch, seq, _ = x.shape
    const2 = lambda b, i: (0, 0)
    tile = lambda width: pl.BlockSpec((1, MERGE_TILE, width), lambda b, i: (b, i, 0))
    last_dilation = ATTN_GROUPS[-1][1]
    res_tile = pl.BlockSpec((1, last_dilation, MERGE_TILE // last_dilation, ATTN_WIDTH),
                            lambda b, i: (b, 0, i, 0))
    staging = pltpu.VMEM((ATTN_WIDTH // LANES, MERGE_TILE, LANES), F32)
    return pl.pallas_call(
        _merge_kernel,
        grid=(batch, seq // MERGE_TILE),
        in_specs=[
            tile(D_MODEL),
            pl.BlockSpec((batch, 3 * D_MODEL), const2),
            pl.BlockSpec((1, D_MODEL), const2),
            _resident((D_MODEL, REST_COLS)),
            tile(ATTN_WIDTH), tile(ATTN_WIDTH), res_tile,
            tile(ATTN_WIDTH), tile(ATTN_WIDTH), res_tile,
            pl.BlockSpec((1, GMLP_WIDTH), const2),
            pl.BlockSpec((1, GMLP_WIDTH), const2),
            pl.BlockSpec((GMLP_GROUPS, CHUNK, CHUNK), lambda b, i: (0, 0, 0)),
            pl.BlockSpec((CHUNK, GMLP_WIDTH), const2),
            _resident((ATTN_WIDTH, D_MODEL)),
            _resident((GMLP_WIDTH, D_MODEL)),
            _resident((D_MODEL, D_MODEL)),
        ],
        out_specs=tile(D_MODEL),
        out_shape=jax.ShapeDtypeStruct(x.shape, x.dtype),
        scratch_shapes=[staging, staging],
        compiler_params=pltpu.CompilerParams(
            dimension_semantics=("arbitrary", "arbitrary"), vmem_limit_bytes=VMEM_LIMIT_BYTES),
        name="merge",
    )(x, ada, norm_g2, w_rest, *os_, *lses, ln_g, ln_b, w_spatial, bias_tab, wa, wb, wo)


def kernel(x, c, positions, norm_g, w_ada, b_ada, w_in, q_norm_g, k_norm_g, sgu_ln_g, sgu_ln_b,
           w_spatial, b_spatial, w_branch_a, w_branch_b, w_out):
    batch, seq, d_model = x.shape
    assert d_model == D_MODEL and seq % QKV_TILE == 0 and seq % MERGE_TILE == 0
    assert w_in.shape == (D_MODEL, QKV_COLS + REST_COLS)

    ada = _ada_call(c, w_ada, b_ada)
    norm_g2 = norm_g.reshape(1, D_MODEL)

    freq = (ROPE_THETA ** (-np.arange(0, ROPE_DIMS, 2, dtype=np.float32) / ROPE_DIMS)).astype(np.float32)
    freq = jnp.asarray(freq.reshape(ROPE_HALF, 1))
    e_mat = jnp.asarray(_rope_expand_matrix(), dtype=BF16)
    bd_mat = jnp.asarray(_head_mean_matrix(), dtype=BF16)

    gq = jnp.tile(q_norm_g, (1, LANES // HEAD_DIM))
    gk = jnp.tile(k_norm_g, (1, LANES // HEAD_DIM))
    qkv = _qkv_call(x, positions, ada, norm_g2, w_in, gq, gk, freq, e_mat, bd_mat)
    rest_block = REST_COLS // batch
    assert QKV_COLS % rest_block == 0
    casts = [
        [(w_in, (D_MODEL, rest_block), QKV_COLS // rest_block, (D_MODEL, REST_COLS))],
        [(w_branch_a, (ATTN_WIDTH // batch, D_MODEL), 0, w_branch_a.shape),
         (w_branch_b, (GMLP_WIDTH // batch, D_MODEL), 0, w_branch_b.shape),
         (w_out, (D_MODEL // batch, D_MODEL), 0, w_out.shape)],
        [],
    ]
    outs, lses, weights = [], [], []
    for g, (window, dilation) in enumerate(ATTN_GROUPS):
        assert window // dilation == Q_BLOCK
        o, lse, *w_bf16 = _attn_call(g, dilation, *qkv[3 * g:3 * g + 3],
                                     token_order_out=g + 1 < len(ATTN_GROUPS), casts=casts[g])
        outs.append(o)
        lses.append(lse)
        weights += w_bf16
    w_rest_bf16, wa_bf16, wb_bf16, wo_bf16 = weights

    bias_tab = jnp.repeat(b_spatial.T, GMLP_GROUP_DIM, axis=1)
    return _merge_call(x, ada, norm_g2, w_rest_bf16, outs, lses,
                       sgu_ln_g.reshape(1, GMLP_WIDTH), sgu_ln_b.reshape(1, GMLP_WIDTH),
                       w_spatial, bias_tab, wa_bf16, wb_bf16, wo_bf16)
```

```python
import functools
import math

import jax
import jax.numpy as jnp
import numpy as np
from jax import lax
from jax.experimental import pallas as pl
from jax.experimental.pallas import tpu as pltpu

D_MODEL = 1024
HEAD_DIM = 64
HEADS = 8
ATTN_GROUPS = ((128, 1), (512, 4), (2048, 16))
ATTN_WIDTH = HEADS * HEAD_DIM
Q_BLOCK = 128
ROPE_THETA = 500000.0
ROPE_DIMS = HEAD_DIM // 4
ROPE_HALF = ROPE_DIMS // 2
GMLP_WIDTH = 512
GMLP_GROUPS = 8
GMLP_GROUP_DIM = GMLP_WIDTH // GMLP_GROUPS
CHUNK = 128
EPS = 1e-6
GROUP_COLS = 3 * ATTN_WIDTH
QKV_COLS = len(ATTN_GROUPS) * GROUP_COLS
REST_COLS = ATTN_WIDTH + 3 * GMLP_WIDTH + 2 * D_MODEL

Q_SCALE = math.log2(math.e) / math.sqrt(HEAD_DIM)

LANES = 128
QKV_TILE = 1024
MERGE_TILE = 512
ADA_STEPS = 6
VMEM_LIMIT_BYTES = 56 * 1024 * 1024

F32 = jnp.float32
BF16 = jnp.bfloat16


def _dot(a, b):
    return jnp.dot(a, b, preferred_element_type=F32)


def _resident(shape):
    return pl.BlockSpec(shape, lambda *_: (0, 0), pipeline_mode=pl.Buffered(1))


def _silu(v):
    return v * (1.0 / (1.0 + jnp.exp(-v)))


def _sigmoid(v):
    return 1.0 / (1.0 + jnp.exp(-v))


def _gelu_exact(v):
    return 0.5 * v * (1.0 + lax.erf(v * (1.0 / math.sqrt(2.0))))


def _ada_ln(xf, ada_ref, norm_g_ref):
    ms = jnp.mean(xf * xf, axis=-1, keepdims=True)
    row = pl.ds(pl.program_id(0), 1)
    shift = ada_ref[row, 0:D_MODEL]
    scale = ada_ref[row, D_MODEL:2 * D_MODEL]
    return xf * lax.rsqrt(ms + EPS) * norm_g_ref[...] * (1.0 + scale) + shift


def _ada_kernel(c_ref, w_ref, b_ref, w_qkv_ref, o_ref, w_qkv_out):
    o_ref[...] = _dot(_silu(c_ref[...]).astype(BF16), w_ref[...].astype(BF16)) + b_ref[...]
    w_qkv_out[...] = w_qkv_ref[...].astype(BF16)


def _ada_call(c, w_ada, b_ada, w_in):
    batch = c.shape[0]
    ada_cols, qkv_cols = 3 * D_MODEL // ADA_STEPS, QKV_COLS // ADA_STEPS
    assert ada_cols % LANES == 0 and qkv_cols % LANES == 0
    col_block = lambda n: (0, n)
    return pl.pallas_call(
        _ada_kernel,
        grid=(ADA_STEPS,),
        in_specs=[pl.BlockSpec((batch, D_MODEL), lambda n: (0, 0)),
                  pl.BlockSpec((D_MODEL, ada_cols), col_block),
                  pl.BlockSpec((1, ada_cols), col_block),
                  pl.BlockSpec((D_MODEL, qkv_cols), col_block)],
        out_specs=[pl.BlockSpec((batch, ada_cols), col_block),
                   pl.BlockSpec((D_MODEL, qkv_cols), col_block)],
        out_shape=[jax.ShapeDtypeStruct((batch, 3 * D_MODEL), F32),
                   jax.ShapeDtypeStruct((D_MODEL, QKV_COLS), BF16)],
        compiler_params=pltpu.CompilerParams(dimension_semantics=("arbitrary",)),
        name="ada",
    )(c, w_ada, b_ada.reshape(1, 3 * D_MODEL), w_in)


def _rope_expand_matrix():
    e = np.zeros((64, 3 * LANES), np.float32)
    for lane in range(LANES):
        dim = lane % HEAD_DIM
        if dim < ROPE_DIMS:
            j = dim % ROPE_HALF
            for part in range(3):
                e[part * 8 + j, lane] = 1.0
                if dim >= ROPE_HALF:
                    e[24 + part * 8 + j, LANES + lane] = 1.0
                else:
                    e[24 + part * 8 + j, 2 * LANES + lane] = -1.0
        else:
            e[48, lane] = 1.0
    return e


def _head_mean_matrix():
    head = np.arange(2 * LANES) // HEAD_DIM
    return (head[:, None] == head[None, :]).astype(np.float32) / HEAD_DIM


def _split3(a):
    hi = a.astype(BF16).astype(F32)
    r = a - hi
    mid = r.astype(BF16).astype(F32)
    return hi, mid, r - mid


def _store_residue_major(out, val, dilation, perm_scr):
    tm, width = val.shape
    if dilation == 1:
        out[0, 0] = val.astype(out.dtype)
        return
    for cb in range(width // LANES):
        perm_scr[cb] = val[:, cb * LANES:(cb + 1) * LANES]
    for r in range(dilation):
        for cb in range(width // LANES):
            out[0, r, :, cb * LANES:(cb + 1) * LANES] = (
                perm_scr[cb, pl.ds(r, tm // dilation, stride=dilation), :].astype(out.dtype))


def _qkv_kernel(x_ref, pos_ref, ada_ref, norm_g_ref, gq_ref, gk_ref, freq_ref, e_ref, bd_ref, *refs):
    n_chunks = QKV_COLS // ATTN_WIDTH
    w_ref, out_refs, perm_scr = refs[0], refs[1:1 + n_chunks], refs[1 + n_chunks]

    xf = x_ref[0]
    tm = xf.shape[0]
    hb = _ada_ln(xf, ada_ref, norm_g_ref).astype(BF16)

    ang = freq_ref[...] * pos_ref[pl.ds(pl.program_id(0), 1), :].astype(F32)
    parts = _split3(jnp.cos(ang)) + _split3(jnp.sin(ang))
    lhs_t = jnp.concatenate(list(parts) + [jnp.ones((8, tm), F32), jnp.zeros((8, tm), F32)], axis=0)
    tab = lax.dot_general(lhs_t.astype(BF16), e_ref[...], (((0,), (0,)), ((), ())),
                          preferred_element_type=F32)
    cos_t, s1_t, s2_t = tab[:, 0:LANES], tab[:, LANES:2 * LANES], tab[:, 2 * LANES:3 * LANES]

    n_groups = len(ATTN_GROUPS)
    chunks = ([(g, which) for which in (0, 1) for g in range(n_groups)]
              + [(g, 2) for g in reversed(range(n_groups))])
    for g, which in chunks:
        dilation = ATTN_GROUPS[g][1]
        col0 = g * GROUP_COLS + which * ATTN_WIDTH
        zc = _dot(hb, w_ref[:, col0:col0 + ATTN_WIDTH])
        out = out_refs[3 * g + which]
        if which == 2:
            _store_residue_major(out, zc, dilation, perm_scr)
            continue
        gain = gq_ref[g:g + 1, :] * Q_SCALE if which == 0 else gk_ref[g:g + 1, :]
        cols = []
        for half in range(ATTN_WIDTH // (2 * LANES)):
            a2 = zc[:, half * 2 * LANES:(half + 1) * 2 * LANES]
            ms = _dot((a2 * a2).astype(BF16), bd_ref[...])
            a2n = a2 * lax.rsqrt(ms + EPS)
            for sub in range(2):
                an = a2n[:, sub * LANES:(sub + 1) * LANES] * gain
                cols.append(an * cos_t + pltpu.roll(an, ROPE_HALF, 1) * s1_t
                            + pltpu.roll(an, LANES - ROPE_HALF, 1) * s2_t)
        _store_residue_major(out, jnp.concatenate(cols, axis=1), dilation, perm_scr)


def _qkv_call(x, positions, ada, norm_g2, w_qkv, gq, gk, freq, e_mat, bd_mat):
    batch, seq, _ = x.shape
    const = lambda b, j: (0, 0)
    out_specs, out_shapes = [], []
    for _, dilation in ATTN_GROUPS:
        spec = pl.BlockSpec((1, dilation, QKV_TILE // dilation, ATTN_WIDTH), lambda b, j: (b, 0, j, 0))
        sds = jax.ShapeDtypeStruct((batch, dilation, seq // dilation, ATTN_WIDTH), BF16)
        out_specs += [spec] * 3
        out_shapes += [sds] * 3
    n_groups = len(ATTN_GROUPS)
    return pl.pallas_call(
        _qkv_kernel,
        grid=(batch, seq // QKV_TILE),
        in_specs=[
            pl.BlockSpec((1, QKV_TILE, D_MODEL), lambda b, j: (b, j, 0)),
            pl.BlockSpec((batch, QKV_TILE), lambda b, j: (0, j)),
            pl.BlockSpec((batch, 3 * D_MODEL), const),
            pl.BlockSpec((1, D_MODEL), const),
            pl.BlockSpec((n_groups, LANES), const),
            pl.BlockSpec((n_groups, LANES), const),
            pl.BlockSpec((8, 1), const),
            pl.BlockSpec((64, 3 * LANES), const),
            pl.BlockSpec((2 * LANES, 2 * LANES), const),
            _resident((D_MODEL, QKV_COLS)),
        ],
        out_specs=out_specs,
        out_shape=out_shapes,
        scratch_shapes=[pltpu.VMEM((ATTN_WIDTH // LANES, QKV_TILE, LANES), F32)],
        compiler_params=pltpu.CompilerParams(
            dimension_semantics=("arbitrary", "arbitrary"), vmem_limit_bytes=VMEM_LIMIT_BYTES),
        name="qkv",
    )(x, positions, ada, norm_g2, gq, gk, freq, e_mat, bd_mat, w_qkv)


def _attn_kernel(q_ref, k_ref, v_ref, *refs, dilation, n_blk, token_order_out, n_cast):
    cast_in, (o_ref, lse_ref) = refs[:n_cast], refs[n_cast:n_cast + 2]
    cast_out, scratch = refs[n_cast + 2:2 * n_cast + 2], refs[2 * n_cast + 2:]
    for src, dst in zip(cast_in, cast_out):
        dst[...] = src[...].astype(dst.dtype)

    row = lax.broadcasted_iota(jnp.int32, (Q_BLOCK, LANES), 0)
    col = lax.broadcasted_iota(jnp.int32, (Q_BLOCK, LANES), 1)
    cur_ok = col <= row
    prev_ok = col >= row
    lane_lo = col < HEAD_DIM
    ones_lo = jnp.where(lane_lo, 1.0, 0.0).astype(BF16)
    ones_hi = jnp.where(lane_lo, 0.0, 1.0).astype(BF16)
    n_pairs = ATTN_WIDTH // LANES
    stage_in_vmem = n_blk > 1
    scratch = list(scratch)
    if stage_in_vmem:
        p_scr, m_scr = scratch.pop(0), scratch.pop(0)
    if dilation > 1 and token_order_out:
        o_plane, lse_plane = scratch

    def split_heads(a):
        zero = jnp.zeros_like(a)
        lo = lax.broadcasted_iota(jnp.int32, a.shape, 1) < HEAD_DIM
        return jnp.concatenate([jnp.where(lo, a, zero), jnp.where(lo, zero, a)], axis=0)

    def window(ref, r, r0, first, lanes):
        if first:
            return ref[0, r, r0:r0 + Q_BLOCK, lanes]
        return ref[0, r, r0 - Q_BLOCK:r0 + Q_BLOCK, lanes]

    def scores(r, r0):
        first = r0 == 0
        ok_head = cur_ok if first else jnp.concatenate([prev_ok, cur_ok], axis=1)
        ok = jnp.concatenate([ok_head, ok_head], axis=1)
        n_keys = ok_head.shape[1]
        stats = []
        for hp in range(n_pairs):
            lanes = slice(hp * LANES, (hp + 1) * LANES)
            q_pair = q_ref[0, r, r0:r0 + Q_BLOCK, lanes]
            k2 = split_heads(window(k_ref, r, r0, first, lanes))
            s = lax.dot_general(q_pair, k2, (((1,), (1,)), ((), ())), preferred_element_type=F32)
            s = jnp.where(ok, s, -jnp.inf)
            m0 = jnp.max(s[:, :n_keys], axis=-1, keepdims=True)
            m1 = jnp.max(s[:, n_keys:], axis=-1, keepdims=True)
            p = jnp.concatenate([jnp.exp2(s[:, :n_keys] - m0), jnp.exp2(s[:, n_keys:] - m1)], axis=1)
            p, m_ln2 = p.astype(BF16), jnp.where(lane_lo, m0, m1) * math.log(2.0)
            if stage_in_vmem:
                p_scr[hp, :, 0:2 * n_keys] = p
                m_scr[hp] = m_ln2
            stats.append((p, m_ln2))
        return stats

    def values(r, r0, stats):
        first = r0 == 0
        reps = 1 if first else 2
        ind = jnp.concatenate([ones_lo] * reps + [ones_hi] * reps, axis=0)
        for hp, (p, m_ln2) in enumerate(stats):
            if stage_in_vmem:
                p, m_ln2 = p_scr[hp, :, 0:2 * reps * Q_BLOCK], m_scr[hp]
            lanes = slice(hp * LANES, (hp + 1) * LANES)
            v2 = jnp.concatenate([split_heads(window(v_ref, r, r0, first, lanes)), ind], axis=1)
            o2 = _dot(p, v2)
            den = o2[:, LANES:]
            o_pair = o2[:, :LANES] * (1.0 / den)
            lse_pair = m_ln2 + jnp.log(den)
            if dilation == 1:
                o_ref[0, r0:r0 + Q_BLOCK, lanes] = o_pair.astype(o_ref.dtype)
                lse_ref[0, r0:r0 + Q_BLOCK, lanes] = lse_pair
            elif not token_order_out:
                o_ref[0, r, r0:r0 + Q_BLOCK, lanes] = o_pair.astype(o_ref.dtype)
                lse_ref[0, r, r0:r0 + Q_BLOCK, lanes] = lse_pair
            else:
                rows = pl.ds(r + r0 * dilation, Q_BLOCK, stride=dilation)
                o_plane[hp, rows, :] = o_pair
                lse_plane[hp, rows, :] = lse_pair

    blocks = [(r, blk * Q_BLOCK) for r in range(dilation) for blk in range(n_blk)]
    stats = scores(*blocks[0])
    for prev, cur in zip(blocks[:-1], blocks[1:]):
        if stage_in_vmem:
            values(*prev, stats)
            stats = scores(*cur)
        else:
            next_stats = scores(*cur)
            values(*prev, stats)
            stats = next_stats
    values(*blocks[-1], stats)

    if dilation > 1 and token_order_out:
        for hp in range(n_pairs):
            lanes = slice(hp * LANES, (hp + 1) * LANES)
            o_ref[0, :, lanes] = o_plane[hp].astype(o_ref.dtype)
            lse_ref[0, :, lanes] = lse_plane[hp]


def _attn_call(g, dilation, q, k, v, token_order_out, casts=()):
    batch, _, res_len, _ = q.shape
    cast_in_specs, cast_out_specs, cast_out_shapes = [], [], []
    for arr, block, first, out_dims in casts:
        axis = 0 if block[0] != out_dims[0] else 1
        assert out_dims[axis] == batch * block[axis]
        def index(b, axis=axis, shift=0):
            return (b + shift, 0) if axis == 0 else (0, b + shift)
        cast_in_specs.append(pl.BlockSpec(block, functools.partial(index, shift=first)))
        cast_out_specs.append(pl.BlockSpec(block, index))
        cast_out_shapes.append(jax.ShapeDtypeStruct(out_dims, BF16))
    seq = dilation * res_len
    in_spec = pl.BlockSpec((1, dilation, res_len, ATTN_WIDTH), lambda b: (b, 0, 0, 0))
    n_pairs = ATTN_WIDTH // LANES
    scratch = []
    if res_len > Q_BLOCK:
        scratch += [pltpu.VMEM((n_pairs, Q_BLOCK, 4 * Q_BLOCK), BF16),
                    pltpu.VMEM((n_pairs, Q_BLOCK, LANES), F32)]
    if token_order_out or dilation == 1:
        out_spec, out_dims = pl.BlockSpec((1, seq, ATTN_WIDTH), lambda b: (b, 0, 0)), (batch, seq, ATTN_WIDTH)
        if dilation > 1:
            scratch += [pltpu.VMEM((n_pairs, seq, LANES), F32)] * 2
    else:
        out_spec, out_dims = in_spec, q.shape
    return pl.pallas_call(
        functools.partial(_attn_kernel, dilation=dilation, n_blk=res_len // Q_BLOCK,
                          token_order_out=token_order_out, n_cast=len(casts)),
        grid=(batch,),
        in_specs=[in_spec, in_spec, in_spec] + cast_in_specs,
        out_specs=[out_spec, out_spec] + cast_out_specs,
        out_shape=[jax.ShapeDtypeStruct(out_dims, BF16), jax.ShapeDtypeStruct(out_dims, F32)] + cast_out_shapes,
        scratch_shapes=scratch,
        compiler_params=pltpu.CompilerParams(
            dimension_semantics=("arbitrary",), vmem_limit_bytes=VMEM_LIMIT_BYTES),
        name=f"attn_g{g}",
    )(q, k, v, *[arr for arr, _, _, _ in casts])


def _token_order(ref, scr):
    _, dilation, n, width = ref.shape
    cols = []
    for cb in range(width // LANES):
        for r in range(dilation):
            scr[cb, pl.ds(r, n, stride=dilation), :] = ref[0, r, :, cb * LANES:(cb + 1) * LANES].astype(F32)
        cols.append(scr[cb])
    return jnp.concatenate(cols, axis=1)


def _merge_kernel(x_ref, ada_ref, norm_g_ref, w_ref,
                  o0_ref, o1_ref, o2_ref, l0_ref, l1_ref, l2_ref,
                  ln_g_ref, ln_b_ref, ws_ref, bias_ref, wa_ref, wb_ref, wo_ref, out_ref, o2_scr, l2_scr):
    xf = x_ref[0]
    tm = xf.shape[0]
    h = _ada_ln(xf, ada_ref, norm_g_ref)
    z = _dot(h.astype(BF16), w_ref[...])
    off = 0
    gate_a = z[:, off:off + ATTN_WIDTH]; off += ATTN_WIDTH
    u = _gelu_exact(z[:, off:off + GMLP_WIDTH]); off += GMLP_WIDTH
    v = _gelu_exact(z[:, off:off + GMLP_WIDTH]); off += GMLP_WIDTH
    gate_b = z[:, off:off + GMLP_WIDTH]; off += GMLP_WIDTH
    merge_a = z[:, off:off + D_MODEL]; off += D_MODEL
    merge_b = z[:, off:off + D_MODEL]

    l0, l1, l2 = l0_ref[0], l1_ref[0], _token_order(l2_ref, l2_scr)
    lmax = jnp.maximum(jnp.maximum(l0, l1), l2)
    e0, e1, e2 = jnp.exp(l0 - lmax), jnp.exp(l1 - lmax), jnp.exp(l2 - lmax)
    attn = (e0 * o0_ref[0].astype(F32) + e1 * o1_ref[0].astype(F32) + e2 * _token_order(o2_ref, o2_scr)) \
        / (e0 + e1 + e2)
    y_a = attn * _silu(gate_a)

    mu = jnp.mean(v, axis=-1, keepdims=True)
    vc = v - mu
    var = jnp.mean(vc * vc, axis=-1, keepdims=True)
    v_ln = (vc * lax.rsqrt(var + EPS) * ln_g_ref[...] + ln_b_ref[...]).astype(BF16)
    row = lax.broadcasted_iota(jnp.int32, (CHUNK, CHUNK), 0)
    col = lax.broadcasted_iota(jnp.int32, (CHUNK, CHUNK), 1)
    causal = row >= col
    lane_lo = col < GMLP_GROUP_DIM
    n_chunks = tm // CHUNK
    sv_cols = []
    for gp in range(GMLP_WIDTH // LANES):
        lanes = slice(gp * LANES, (gp + 1) * LANES)
        rhs = jnp.concatenate([v_ln[c * CHUNK:(c + 1) * CHUNK, lanes] for c in range(n_chunks)], axis=1)
        res = []
        for half in range(2):
            w_s = jnp.where(causal, ws_ref[2 * gp + half], 0.0).astype(BF16)
            res.append(_dot(w_s, rhs))
        sel = jnp.concatenate(
            [jnp.where(lane_lo, res[0][:, c * LANES:(c + 1) * LANES], res[1][:, c * LANES:(c + 1) * LANES])
             + bias_ref[:, lanes] for c in range(n_chunks)], axis=0)
        sv_cols.append(sel)
    sv = jnp.concatenate(sv_cols, axis=1)
    y_b = u * sv * _silu(gate_b)

    merged = (_sigmoid(merge_a) * _dot(y_a.astype(BF16), wa_ref[...])
              + _sigmoid(merge_b) * _dot(y_b.astype(BF16), wb_ref[...]))
    out = _dot(merged.astype(BF16), wo_ref[...])
    gate = ada_ref[pl.ds(pl.program_id(0), 1), 2 * D_MODEL:3 * D_MODEL]
    out_ref[0] = xf + gate * out


def _merge_call(x, ada, norm_g2, w_rest, os_, lses, ln_g, ln_b, w_spatial, bias_tab, wa, wb, wo):
    batch, seq, _ = x.shape
    const2 = lambda b, i: (0, 0)
    tile = lambda width: pl.BlockSpec((1, MERGE_TILE, width), lambda b, i: (b, i, 0))
    last_dilation = ATTN_GROUPS[-1][1]
    res_tile = pl.BlockSpec((1, last_dilation, MERGE_TILE // last_dilation, ATTN_WIDTH),
                            lambda b, i: (b, 0, i, 0))
    staging = pltpu.VMEM((ATTN_WIDTH // LANES, MERGE_TILE, LANES), F32)
    return pl.pallas_call(
        _merge_kernel,
        grid=(batch, seq // MERGE_TILE),
        in_specs=[
            tile(D_MODEL),
            pl.BlockSpec((batch, 3 * D_MODEL), const2),
            pl.BlockSpec((1, D_MODEL), const2),
            _resident((D_MODEL, REST_COLS)),
            tile(ATTN_WIDTH), tile(ATTN_WIDTH), res_tile,
            tile(ATTN_WIDTH), tile(ATTN_WIDTH), res_tile,
            pl.BlockSpec((1, GMLP_WIDTH), const2),
            pl.BlockSpec((1, GMLP_WIDTH), const2),
            pl.BlockSpec((GMLP_GROUPS, CHUNK, CHUNK), lambda b, i: (0, 0, 0)),
            pl.BlockSpec((CHUNK, GMLP_WIDTH), const2),
            _resident((ATTN_WIDTH, D_MODEL)),
            _resident((GMLP_WIDTH, D_MODEL)),
            _resident((D_MODEL, D_MODEL)),
        ],
        out_specs=tile(D_MODEL),
        out_shape=jax.ShapeDtypeStruct(x.shape, x.dtype),
        scratch_shapes=[staging, staging],
        compiler_params=pltpu.CompilerParams(
            dimension_semantics=("arbitrary", "arbitrary"), vmem_limit_bytes=VMEM_LIMIT_BYTES),
        name="merge",
    )(x, ada, norm_g2, w_rest, *os_, *lses, ln_g, ln_b, w_spatial, bias_tab, wa, wb, wo)


def kernel(x, c, positions, norm_g, w_ada, b_ada, w_in, q_norm_g, k_norm_g, sgu_ln_g, sgu_ln_b,
           w_spatial, b_spatial, w_branch_a, w_branch_b, w_out):
    batch, seq, d_model = x.shape
    assert d_model == D_MODEL and seq % QKV_TILE == 0 and seq % MERGE_TILE == 0
    assert w_in.shape == (D_MODEL, QKV_COLS + REST_COLS)

    ada, w_qkv_bf16 = _ada_call(c, w_ada, b_ada, w_in)
    norm_g2 = norm_g.reshape(1, D_MODEL)

    freq = (ROPE_THETA ** (-np.arange(0, ROPE_DIMS, 2, dtype=np.float32) / ROPE_DIMS)).astype(np.float32)
    freq = jnp.asarray(freq.reshape(ROPE_HALF, 1))
    e_mat = jnp.asarray(_rope_expand_matrix(), dtype=BF16)
    bd_mat = jnp.asarray(_head_mean_matrix(), dtype=BF16)

    gq = jnp.tile(q_norm_g, (1, LANES // HEAD_DIM))
    gk = jnp.tile(k_norm_g, (1, LANES // HEAD_DIM))
    qkv = _qkv_call(x, positions, ada, norm_g2, w_qkv_bf16, gq, gk, freq, e_mat, bd_mat)
    rest_block = REST_COLS // batch
    assert QKV_COLS % rest_block == 0
    casts = [
        [(w_in, (D_MODEL, rest_block), QKV_COLS // rest_block, (D_MODEL, REST_COLS))],
        [(w_branch_a, (ATTN_WIDTH // batch, D_MODEL), 0, w_branch_a.shape),
         (w_branch_b, (GMLP_WIDTH // batch, D_MODEL), 0, w_branch_b.shape),
         (w_out, (D_MODEL // batch, D_MODEL), 0, w_out.shape)],
        [],
    ]
    outs, lses, weights = [], [], []
    for g, (window, dilation) in enumerate(ATTN_GROUPS):
        assert window // dilation == Q_BLOCK
        o, lse, *w_bf16 = _attn_call(g, dilation, *qkv[3 * g:3 * g + 3],
                                     token_order_out=g + 1 < len(ATTN_GROUPS), casts=casts[g])
        outs.append(o)
        lses.append(lse)
        weights += w_bf16
    w_rest_bf16, wa_bf16, wb_bf16, wo_bf16 = weights

    bias_tab = jnp.repeat(b_spatial.T, GMLP_GROUP_DIM, axis=1)
    return _merge_call(x, ada, norm_g2, w_rest_bf16, outs, lses,
                       sgu_ln_g.reshape(1, GMLP_WIDTH), sgu_ln_b.reshape(1, GMLP_WIDTH),
                       w_spatial, bias_tab, wa_bf16, wb_bf16, wo_bf16)
```

```python
import functools
import math

import jax
import jax.numpy as jnp
import numpy as np
from jax import lax
from jax.experimental import pallas as pl
from jax.experimental.pallas import tpu as pltpu

D_MODEL = 1024
HEAD_DIM = 64
HEADS = 8
ATTN_GROUPS = ((128, 1), (512, 4), (2048, 16))
ATTN_WIDTH = HEADS * HEAD_DIM
Q_BLOCK = 128
ROPE_THETA = 500000.0
ROPE_DIMS = HEAD_DIM // 4
ROPE_HALF = ROPE_DIMS // 2
GMLP_WIDTH = 512
GMLP_GROUPS = 8
GMLP_GROUP_DIM = GMLP_WIDTH // GMLP_GROUPS
CHUNK = 128
EPS = 1e-6
GROUP_COLS = 3 * ATTN_WIDTH
QKV_COLS = len(ATTN_GROUPS) * GROUP_COLS
REST_COLS = ATTN_WIDTH + 3 * GMLP_WIDTH + 2 * D_MODEL

Q_SCALE = math.log2(math.e) / math.sqrt(HEAD_DIM)

LANES = 128
QKV_TILE = 1024
MERGE_TILE = 512
ADA_STEPS = 6
VMEM_LIMIT_BYTES = 56 * 1024 * 1024

F32 = jnp.float32
BF16 = jnp.bfloat16


def _dot(a, b):
    return jnp.dot(a, b, preferred_element_type=F32)


def _resident(shape):
    return pl.BlockSpec(shape, lambda *_: (0, 0), pipeline_mode=pl.Buffered(1))


def _silu(v):
    return v * (1.0 / (1.0 + jnp.exp(-v)))


def _sigmoid(v):
    return 1.0 / (1.0 + jnp.exp(-v))


def _gelu_exact(v):
    return 0.5 * v * (1.0 + lax.erf(v * (1.0 / math.sqrt(2.0))))


def _ada_ln(xf, ada_ref, norm_g_ref):
    ms = jnp.mean(xf * xf, axis=-1, keepdims=True)
    row = pl.ds(pl.program_id(0), 1)
    shift = ada_ref[row, 0:D_MODEL]
    scale = ada_ref[row, D_MODEL:2 * D_MODEL]
    return xf * lax.rsqrt(ms + EPS) * norm_g_ref[...] * (1.0 + scale) + shift


def _ada_kernel(c_ref, w_ref, b_ref, w_qkv_ref, o_ref, w_qkv_out):
    o_ref[...] = _dot(_silu(c_ref[...]).astype(BF16), w_ref[...].astype(BF16)) + b_ref[...]
    w_qkv_out[...] = w_qkv_ref[...].astype(BF16)


def _ada_call(c, w_ada, b_ada, w_in):
    batch = c.shape[0]
    ada_cols, qkv_cols = 3 * D_MODEL // ADA_STEPS, QKV_COLS // ADA_STEPS
    assert ada_cols % LANES == 0 and qkv_cols % LANES == 0
    col_block = lambda n: (0, n)
    return pl.pallas_call(
        _ada_kernel,
        grid=(ADA_STEPS,),
        in_specs=[pl.BlockSpec((batch, D_MODEL), lambda n: (0, 0)),
                  pl.BlockSpec((D_MODEL, ada_cols), col_block),
                  pl.BlockSpec((1, ada_cols), col_block),
                  pl.BlockSpec((D_MODEL, qkv_cols), col_block)],
        out_specs=[pl.BlockSpec((batch, ada_cols), col_block),
                   pl.BlockSpec((D_MODEL, qkv_cols), col_block)],
        out_shape=[jax.ShapeDtypeStruct((batch, 3 * D_MODEL), F32),
                   jax.ShapeDtypeStruct((D_MODEL, QKV_COLS), BF16)],
        compiler_params=pltpu.CompilerParams(dimension_semantics=("arbitrary",)),
        name="ada",
    )(c, w_ada, b_ada.reshape(1, 3 * D_MODEL), w_in)


def _rope_expand_matrix():
    e = np.zeros((64, 3 * LANES), np.float32)
    for lane in range(LANES):
        dim = lane % HEAD_DIM
        if dim < ROPE_DIMS:
            j = dim % ROPE_HALF
            for part in range(3):
                e[part * 8 + j, lane] = 1.0
                if dim >= ROPE_HALF:
                    e[24 + part * 8 + j, LANES + lane] = 1.0
                else:
                    e[24 + part * 8 + j, 2 * LANES + lane] = -1.0
        else:
            e[48, lane] = 1.0
    return e


def _head_mean_matrix():
    head = np.arange(2 * LANES) // HEAD_DIM
    return (head[:, None] == head[None, :]).astype(np.float32) / HEAD_DIM


def _split3(a):
    hi = a.astype(BF16).astype(F32)
    r = a - hi
    mid = r.astype(BF16).astype(F32)
    return hi, mid, r - mid


def _store_residue_major(out, val, dilation, perm_scr):
    tm, width = val.shape
    if dilation == 1:
        out[0, 0] = val.astype(out.dtype)
        return
    for cb in range(width // LANES):
        perm_scr[cb] = val[:, cb * LANES:(cb + 1) * LANES]
    for r in range(dilation):
        for cb in range(width // LANES):
            out[0, r, :, cb * LANES:(cb + 1) * LANES] = (
                perm_scr[cb, pl.ds(r, tm // dilation, stride=dilation), :].astype(out.dtype))


def _qkv_kernel(x_ref, pos_ref, ada_ref, norm_g_ref, gq_ref, gk_ref, freq_ref, e_ref, bd_ref, *refs):
    n_chunks = QKV_COLS // ATTN_WIDTH
    w_ref, out_refs, perm_scr = refs[0], refs[1:1 + n_chunks], refs[1 + n_chunks]

    xf = x_ref[0]
    tm = xf.shape[0]
    hb = _ada_ln(xf, ada_ref, norm_g_ref).astype(BF16)

    ang = freq_ref[...] * pos_ref[pl.ds(pl.program_id(0), 1), :].astype(F32)
    parts = _split3(jnp.cos(ang)) + _split3(jnp.sin(ang))
    lhs_t = jnp.concatenate(list(parts) + [jnp.ones((8, tm), F32), jnp.zeros((8, tm), F32)], axis=0)
    tab = lax.dot_general(lhs_t.astype(BF16), e_ref[...], (((0,), (0,)), ((), ())),
                          preferred_element_type=F32)
    cos_t, s1_t, s2_t = tab[:, 0:LANES], tab[:, LANES:2 * LANES], tab[:, 2 * LANES:3 * LANES]

    n_groups = len(ATTN_GROUPS)
    chunks = ([(g, which) for which in (0, 1) for g in range(n_groups)]
              + [(g, 2) for g in reversed(range(n_groups))])
    for g, which in chunks:
        dilation = ATTN_GROUPS[g][1]
        col0 = g * GROUP_COLS + which * ATTN_WIDTH
        zc = _dot(hb, w_ref[:, col0:col0 + ATTN_WIDTH])
        out = out_refs[3 * g + which]
        if which == 2:
            _store_residue_major(out, zc, dilation, perm_scr)
            continue
        gain = gq_ref[g:g + 1, :] * Q_SCALE if which == 0 else gk_ref[g:g + 1, :]
        cols = []
        for half in range(ATTN_WIDTH // (2 * LANES)):
            a2 = zc[:, half * 2 * LANES:(half + 1) * 2 * LANES]
            ms = _dot((a2 * a2).astype(BF16), bd_ref[...])
            a2n = a2 * lax.rsqrt(ms + EPS)
            for sub in range(2):
                an = a2n[:, sub * LANES:(sub + 1) * LANES] * gain
                cols.append(an * cos_t + pltpu.roll(an, ROPE_HALF, 1) * s1_t
                            + pltpu.roll(an, LANES - ROPE_HALF, 1) * s2_t)
        _store_residue_major(out, jnp.concatenate(cols, axis=1), dilation, perm_scr)


def _qkv_call(x, positions, ada, norm_g2, w_qkv, gq, gk, freq, e_mat, bd_mat):
    batch, seq, _ = x.shape
    const = lambda b, j: (0, 0)
    out_specs, out_shapes = [], []
    for _, dilation in ATTN_GROUPS:
        spec = pl.BlockSpec((1, dilation, QKV_TILE // dilation, ATTN_WIDTH), lambda b, j: (b, 0, j, 0))
        sds = jax.ShapeDtypeStruct((batch, dilation, seq // dilation, ATTN_WIDTH), BF16)
        out_specs += [spec] * 3
        out_shapes += [sds] * 3
    n_groups = len(ATTN_GROUPS)
    return pl.pallas_call(
        _qkv_kernel,
        grid=(batch, seq // QKV_TILE),
        in_specs=[
            pl.BlockSpec((1, QKV_TILE, D_MODEL), lambda b, j: (b, j, 0)),
            pl.BlockSpec((batch, QKV_TILE), lambda b, j: (0, j)),
            pl.BlockSpec((batch, 3 * D_MODEL), const),
            pl.BlockSpec((1, D_MODEL), const),
            pl.BlockSpec((n_groups, LANES), const),
            pl.BlockSpec((n_groups, LANES), const),
            pl.BlockSpec((8, 1), const),
            pl.BlockSpec((64, 3 * LANES), const),
            pl.BlockSpec((2 * LANES, 2 * LANES), const),
            _resident((D_MODEL, QKV_COLS)),
        ],
        out_specs=out_specs,
        out_shape=out_shapes,
        scratch_shapes=[pltpu.VMEM((ATTN_WIDTH // LANES, QKV_TILE, LANES), F32)],
        compiler_params=pltpu.CompilerParams(
            dimension_semantics=("arbitrary", "arbitrary"), vmem_limit_bytes=VMEM_LIMIT_BYTES),
        name="qkv",
    )(x, positions, ada, norm_g2, gq, gk, freq, e_mat, bd_mat, w_qkv)


def _attn_kernel(q_ref, k_ref, v_ref, *refs, dilation, n_blk, token_order_out, n_cast):
    cast_in, (o_ref, lse_ref) = refs[:n_cast], refs[n_cast:n_cast + 2]
    cast_out, scratch = refs[n_cast + 2:2 * n_cast + 2], refs[2 * n_cast + 2:]
    for src, dst in zip(cast_in, cast_out):
        dst[...] = src[...].astype(dst.dtype)

    row = lax.broadcasted_iota(jnp.int32, (Q_BLOCK, LANES), 0)
    col = lax.broadcasted_iota(jnp.int32, (Q_BLOCK, LANES), 1)
    cur_ok = col <= row
    prev_ok = col >= row
    lane_lo = col < HEAD_DIM
    ones_lo = jnp.where(lane_lo, 1.0, 0.0).astype(BF16)
    ones_hi = jnp.where(lane_lo, 0.0, 1.0).astype(BF16)
    n_pairs = ATTN_WIDTH // LANES
    stage_in_vmem = n_blk > 1
    scratch = list(scratch)
    if stage_in_vmem:
        p_scr, m_scr = scratch.pop(0), scratch.pop(0)
    if dilation > 1 and token_order_out:
        o_plane, lse_plane = scratch

    def split_heads(a):
        zero = jnp.zeros_like(a)
        lo = lax.broadcasted_iota(jnp.int32, a.shape, 1) < HEAD_DIM
        return jnp.concatenate([jnp.where(lo, a, zero), jnp.where(lo, zero, a)], axis=0)

    def window(ref, r, r0, first, lanes):
        if first:
            return ref[0, r, r0:r0 + Q_BLOCK, lanes]
        return ref[0, r, r0 - Q_BLOCK:r0 + Q_BLOCK, lanes]

    def scores(r, r0):
        first = r0 == 0
        ok_head = cur_ok if first else jnp.concatenate([prev_ok, cur_ok], axis=1)
        ok = jnp.concatenate([ok_head, ok_head], axis=1)
        n_keys = ok_head.shape[1]
        stats = []
        for hp in range(n_pairs):
            lanes = slice(hp * LANES, (hp + 1) * LANES)
            q_pair = q_ref[0, r, r0:r0 + Q_BLOCK, lanes]
            k2 = split_heads(window(k_ref, r, r0, first, lanes))
            s = lax.dot_general(q_pair, k2, (((1,), (1,)), ((), ())), preferred_element_type=F32)
            s = jnp.where(ok, s, -jnp.inf)
            m0 = jnp.max(s[:, :n_keys], axis=-1, keepdims=True)
            m1 = jnp.max(s[:, n_keys:], axis=-1, keepdims=True)
            p = jnp.concatenate([jnp.exp2(s[:, :n_keys] - m0), jnp.exp2(s[:, n_keys:] - m1)], axis=1)
            p, m_ln2 = p.astype(BF16), jnp.where(lane_lo, m0, m1) * math.log(2.0)
            if stage_in_vmem:
                p_scr[hp, :, 0:2 * n_keys] = p
                m_scr[hp] = m_ln2
            stats.append((p, m_ln2))
        return stats

    def values(r, r0, stats):
        first = r0 == 0
        reps = 1 if first else 2
        ind = jnp.concatenate([ones_lo] * reps + [ones_hi] * reps, axis=0)
        for hp, (p, m_ln2) in enumerate(stats):
            if stage_in_vmem:
                p, m_ln2 = p_scr[hp, :, 0:2 * reps * Q_BLOCK], m_scr[hp]
            lanes = slice(hp * LANES, (hp + 1) * LANES)
            v2 = jnp.concatenate([split_heads(window(v_ref, r, r0, first, lanes)), ind], axis=1)
            o2 = _dot(p, v2)
            den = o2[:, LANES:]
            o_pair = o2[:, :LANES] * (1.0 / den)
            lse_pair = m_ln2 + jnp.log(den)
            if dilation == 1:
                o_ref[0, r0:r0 + Q_BLOCK, lanes] = o_pair.astype(o_ref.dtype)
                lse_ref[0, r0:r0 + Q_BLOCK, lanes] = lse_pair
            elif not token_order_out:
                o_ref[0, r, r0:r0 + Q_BLOCK, lanes] = o_pair.astype(o_ref.dtype)
                lse_ref[0, r, r0:r0 + Q_BLOCK, lanes] = lse_pair
            else:
                rows = pl.ds(r + r0 * dilation, Q_BLOCK, stride=dilation)
                o_plane[hp, rows, :] = o_pair
                lse_plane[hp, rows, :] = lse_pair

    blocks = [(r, blk * Q_BLOCK) for r in range(dilation) for blk in range(n_blk)]
    stats = scores(*blocks[0])
    for prev, cur in zip(blocks[:-1], blocks[1:]):
        if stage_in_vmem:
            values(*prev, stats)
            stats = scores(*cur)
        else:
            next_stats = scores(*cur)
            values(*prev, stats)
            stats = next_stats
    values(*blocks[-1], stats)

    if dilation > 1 and token_order_out:
        for hp in range(n_pairs):
            lanes = slice(hp * LANES, (hp + 1) * LANES)
            o_ref[0, :, lanes] = o_plane[hp].astype(o_ref.dtype)
            lse_ref[0, :, lanes] = lse_plane[hp]


def _attn_call(g, dilation, q, k, v, token_order_out, casts=()):
    batch, _, res_len, _ = q.shape
    cast_in_specs, cast_out_specs, cast_out_shapes = [], [], []
    for arr, block, first, out_dims in casts:
        axis = 0 if block[0] != out_dims[0] else 1
        assert out_dims[axis] == batch * block[axis]
        def index(b, axis=axis, shift=0):
            return (b + shift, 0) if axis == 0 else (0, b + shift)
        cast_in_specs.append(pl.BlockSpec(block, functools.partial(index, shift=first)))
        cast_out_specs.append(pl.BlockSpec(block, index))
        cast_out_shapes.append(jax.ShapeDtypeStruct(out_dims, BF16))
    seq = dilation * res_len
    in_spec = pl.BlockSpec((1, dilation, res_len, ATTN_WIDTH), lambda b: (b, 0, 0, 0))
    n_pairs = ATTN_WIDTH // LANES
    scratch = []
    if res_len > Q_BLOCK:
        scratch += [pltpu.VMEM((n_pairs, Q_BLOCK, 4 * Q_BLOCK), BF16),
                    pltpu.VMEM((n_pairs, Q_BLOCK, LANES), F32)]
    if token_order_out or dilation == 1:
        out_spec, out_dims = pl.BlockSpec((1, seq, ATTN_WIDTH), lambda b: (b, 0, 0)), (batch, seq, ATTN_WIDTH)
        if dilation > 1:
            scratch += [pltpu.VMEM((n_pairs, seq, LANES), F32)] * 2
    else:
        out_spec, out_dims = in_spec, q.shape
    return pl.pallas_call(
        functools.partial(_attn_kernel, dilation=dilation, n_blk=res_len // Q_BLOCK,
                          token_order_out=token_order_out, n_cast=len(casts)),
        grid=(batch,),
        in_specs=[in_spec, in_spec, in_spec] + cast_in_specs,
        out_specs=[out_spec, out_spec] + cast_out_specs,
        out_shape=[jax.ShapeDtypeStruct(out_dims, BF16), jax.ShapeDtypeStruct(out_dims, F32)] + cast_out_shapes,
        scratch_shapes=scratch,
        compiler_params=pltpu.CompilerParams(
            dimension_semantics=("arbitrary",), vmem_limit_bytes=VMEM_LIMIT_BYTES),
        name=f"attn_g{g}",
    )(q, k, v, *[arr for arr, _, _, _ in casts])


def _token_order(ref, mid_scr, scr):
    _, dilation, n, width = ref.shape
    sub = 4
    assert dilation == sub * sub
    cols = []
    for cb in range(width // LANES):
        for r in range(dilation):
            rh, rl = divmod(r, sub)
            mid_scr[cb, pl.ds(rl * sub * n + rh, n, stride=sub), :] = (
                ref[0, r, :, cb * LANES:(cb + 1) * LANES].astype(F32))
        for rl in range(sub):
            scr[cb, pl.ds(rl, sub * n, stride=sub), :] = mid_scr[cb, rl * sub * n:(rl + 1) * sub * n, :]
        cols.append(scr[cb])
    return jnp.concatenate(cols, axis=1)


def _merge_kernel(x_ref, ada_ref, norm_g_ref, w_ref,
                  o0_ref, o1_ref, o2_ref, l0_ref, l1_ref, l2_ref,
                  ln_g_ref, ln_b_ref, ws_ref, bias_ref, wa_ref, wb_ref, wo_ref, out_ref,
                  o2_mid, o2_scr, l2_mid, l2_scr):
    xf = x_ref[0]
    tm = xf.shape[0]
    h = _ada_ln(xf, ada_ref, norm_g_ref)
    z = _dot(h.astype(BF16), w_ref[...])
    off = 0
    gate_a = z[:, off:off + ATTN_WIDTH]; off += ATTN_WIDTH
    u = _gelu_exact(z[:, off:off + GMLP_WIDTH]); off += GMLP_WIDTH
    v = _gelu_exact(z[:, off:off + GMLP_WIDTH]); off += GMLP_WIDTH
    gate_b = z[:, off:off + GMLP_WIDTH]; off += GMLP_WIDTH
    merge_a = z[:, off:off + D_MODEL]; off += D_MODEL
    merge_b = z[:, off:off + D_MODEL]

    l0, l1, l2 = l0_ref[0], l1_ref[0], _token_order(l2_ref, l2_mid, l2_scr)
    lmax = jnp.maximum(jnp.maximum(l0, l1), l2)
    e0, e1, e2 = jnp.exp(l0 - lmax), jnp.exp(l1 - lmax), jnp.exp(l2 - lmax)
    attn = (e0 * o0_ref[0].astype(F32) + e1 * o1_ref[0].astype(F32) + e2 * _token_order(o2_ref, o2_mid, o2_scr)) \
        / (e0 + e1 + e2)
    y_a = attn * _silu(gate_a)

    mu = jnp.mean(v, axis=-1, keepdims=True)
    vc = v - mu
    var = jnp.mean(vc * vc, axis=-1, keepdims=True)
    v_ln = (vc * lax.rsqrt(var + EPS) * ln_g_ref[...] + ln_b_ref[...]).astype(BF16)
    row = lax.broadcasted_iota(jnp.int32, (CHUNK, CHUNK), 0)
    col = lax.broadcasted_iota(jnp.int32, (CHUNK, CHUNK), 1)
    causal = row >= col
    lane_lo = col < GMLP_GROUP_DIM
    n_chunks = tm // CHUNK
    sv_cols = []
    for gp in range(GMLP_WIDTH // LANES):
        lanes = slice(gp * LANES, (gp + 1) * LANES)
        rhs = jnp.concatenate([v_ln[c * CHUNK:(c + 1) * CHUNK, lanes] for c in range(n_chunks)], axis=1)
        res = []
        for half in range(2):
            w_s = jnp.where(causal, ws_ref[2 * gp + half], 0.0).astype(BF16)
            res.append(_dot(w_s, rhs))
        sel = jnp.concatenate(
            [jnp.where(lane_lo, res[0][:, c * LANES:(c + 1) * LANES], res[1][:, c * LANES:(c + 1) * LANES])
             + bias_ref[:, lanes] for c in range(n_chunks)], axis=0)
        sv_cols.append(sel)
    sv = jnp.concatenate(sv_cols, axis=1)
    y_b = u * sv * _silu(gate_b)

    merged = (_sigmoid(merge_a) * _dot(y_a.astype(BF16), wa_ref[...])
              + _sigmoid(merge_b) * _dot(y_b.astype(BF16), wb_ref[...]))
    out = _dot(merged.astype(BF16), wo_ref[...])
    gate = ada_ref[pl.ds(pl.program_id(0), 1), 2 * D_MODEL:3 * D_MODEL]
    out_ref[0] = xf + gate * out


def _merge_call(x, ada, norm_g2, w_rest, os_, lses, ln_g, ln_b, w_spatial, bias_tab, wa, wb, wo):
    batch, seq, _ = x.shape
    const2 = lambda b, i: (0, 0)
    tile = lambda width: pl.BlockSpec((1, MERGE_TILE, width), lambda b, i: (b, i, 0))
    last_dilation = ATTN_GROUPS[-1][1]
    res_tile = pl.BlockSpec((1, last_dilation, MERGE_TILE // last_dilation, ATTN_WIDTH),
                            lambda b, i: (b, 0, i, 0))
    staging = pltpu.VMEM((ATTN_WIDTH // LANES, MERGE_TILE, LANES), F32)
    return pl.pallas_call(
        _merge_kernel,
        grid=(batch, seq // MERGE_TILE),
        in_specs=[
            tile(D_MODEL),
            pl.BlockSpec((batch, 3 * D_MODEL), const2),
            pl.BlockSpec((1, D_MODEL), const2),
            _resident((D_MODEL, REST_COLS)),
            tile(ATTN_WIDTH), tile(ATTN_WIDTH), res_tile,
            tile(ATTN_WIDTH), tile(ATTN_WIDTH), res_tile,
            pl.BlockSpec((1, GMLP_WIDTH), const2),
            pl.BlockSpec((1, GMLP_WIDTH), const2),
            pl.BlockSpec((GMLP_GROUPS, CHUNK, CHUNK), lambda b, i: (0, 0, 0)),
            pl.BlockSpec((CHUNK, GMLP_WIDTH), const2),
            _resident((ATTN_WIDTH, D_MODEL)),
            _resident((GMLP_WIDTH, D_MODEL)),
            _resident((D_MODEL, D_MODEL)),
        ],
        out_specs=tile(D_MODEL),
        out_shape=jax.ShapeDtypeStruct(x.shape, x.dtype),
        scratch_shapes=[staging] * 4,
        compiler_params=pltpu.CompilerParams(
            dimension_semantics=("arbitrary", "arbitrary"), vmem_limit_bytes=VMEM_LIMIT_BYTES),
        name="merge",
    )(x, ada, norm_g2, w_rest, *os_, *lses, ln_g, ln_b, w_spatial, bias_tab, wa, wb, wo)


def kernel(x, c, positions, norm_g, w_ada, b_ada, w_in, q_norm_g, k_norm_g, sgu_ln_g, sgu_ln_b,
           w_spatial, b_spatial, w_branch_a, w_branch_b, w_out):
    batch, seq, d_model = x.shape
    assert d_model == D_MODEL and seq % QKV_TILE == 0 and seq % MERGE_TILE == 0
    assert w_in.shape == (D_MODEL, QKV_COLS + REST_COLS)

    ada, w_qkv_bf16 = _ada_call(c, w_ada, b_ada, w_in)
    norm_g2 = norm_g.reshape(1, D_MODEL)

    freq = (ROPE_THETA ** (-np.arange(0, ROPE_DIMS, 2, dtype=np.float32) / ROPE_DIMS)).astype(np.float32)
    freq = jnp.asarray(freq.reshape(ROPE_HALF, 1))
    e_mat = jnp.asarray(_rope_expand_matrix(), dtype=BF16)
    bd_mat = jnp.asarray(_head_mean_matrix(), dtype=BF16)

    gq = jnp.tile(q_norm_g, (1, LANES // HEAD_DIM))
    gk = jnp.tile(k_norm_g, (1, LANES // HEAD_DIM))
    qkv = _qkv_call(x, positions, ada, norm_g2, w_qkv_bf16, gq, gk, freq, e_mat, bd_mat)
    rest_block = REST_COLS // batch
    assert QKV_COLS % rest_block == 0
    casts = [
        [(w_in, (D_MODEL, rest_block), QKV_COLS // rest_block, (D_MODEL, REST_COLS))],
        [(w_branch_a, (ATTN_WIDTH // batch, D_MODEL), 0, w_branch_a.shape),
         (w_branch_b, (GMLP_WIDTH // batch, D_MODEL), 0, w_branch_b.shape),
         (w_out, (D_MODEL // batch, D_MODEL), 0, w_out.shape)],
        [],
    ]
    outs, lses, weights = [], [], []
    for g, (window, dilation) in enumerate(ATTN_GROUPS):
        assert window // dilation == Q_BLOCK
        o, lse, *w_bf16 = _attn_call(g, dilation, *qkv[3 * g:3 * g + 3],
                                     token_order_out=g + 1 < len(ATTN_GROUPS), casts=casts[g])
        outs.append(o)
        lses.append(lse)
        weights += w_bf16
    w_rest_bf16, wa_bf16, wb_bf16, wo_bf16 = weights

    bias_tab = jnp.repeat(b_spatial.T, GMLP_GROUP_DIM, axis=1)
    return _merge_call(x, ada, norm_g2, w_rest_bf16, outs, lses,
                       sgu_ln_g.reshape(1, GMLP_WIDTH), sgu_ln_b.reshape(1, GMLP_WIDTH),
                       w_spatial, bias_tab, wa_bf16, wb_bf16, wo_bf16)
```

```python
import functools
import math

import jax
import jax.numpy as jnp
import numpy as np
from jax import lax
from jax.experimental import pallas as pl
from jax.experimental.pallas import tpu as pltpu

D_MODEL = 1024
HEAD_DIM = 64
HEADS = 8
ATTN_GROUPS = ((128, 1), (512, 4), (2048, 16))
ATTN_WIDTH = HEADS * HEAD_DIM
Q_BLOCK = 128
ROPE_THETA = 500000.0
ROPE_DIMS = HEAD_DIM // 4
ROPE_HALF = ROPE_DIMS // 2
GMLP_WIDTH = 512
GMLP_GROUPS = 8
GMLP_GROUP_DIM = GMLP_WIDTH // GMLP_GROUPS
CHUNK = 128
EPS = 1e-6
GROUP_COLS = 3 * ATTN_WIDTH
QKV_COLS = len(ATTN_GROUPS) * GROUP_COLS
REST_COLS = ATTN_WIDTH + 3 * GMLP_WIDTH + 2 * D_MODEL

Q_SCALE = math.log2(math.e) / math.sqrt(HEAD_DIM)

LANES = 128
SUBLANE_STRIDE = 4
QKV_TILE = 1024
MERGE_TILE = 512
ADA_STEPS = 6
VMEM_LIMIT_BYTES = 56 * 1024 * 1024

F32 = jnp.float32
BF16 = jnp.bfloat16


def _dot(a, b):
    return jnp.dot(a, b, preferred_element_type=F32)


def _resident(shape):
    return pl.BlockSpec(shape, lambda *_: (0, 0), pipeline_mode=pl.Buffered(1))


def _silu(v):
    return v * (1.0 / (1.0 + jnp.exp(-v)))


def _sigmoid(v):
    return 1.0 / (1.0 + jnp.exp(-v))


def _gelu_exact(v):
    return 0.5 * v * (1.0 + lax.erf(v * (1.0 / math.sqrt(2.0))))


def _ada_ln(xf, ada_ref, norm_g_ref):
    ms = jnp.mean(xf * xf, axis=-1, keepdims=True)
    row = pl.ds(pl.program_id(0), 1)
    shift = ada_ref[row, 0:D_MODEL]
    scale = ada_ref[row, D_MODEL:2 * D_MODEL]
    return xf * lax.rsqrt(ms + EPS) * norm_g_ref[...] * (1.0 + scale) + shift


def _ada_kernel(c_ref, w_ref, b_ref, w_qkv_ref, o_ref, w_qkv_out):
    o_ref[...] = _dot(_silu(c_ref[...]).astype(BF16), w_ref[...].astype(BF16)) + b_ref[...]
    w_qkv_out[...] = w_qkv_ref[...].astype(BF16)


def _ada_call(c, w_ada, b_ada, w_in):
    batch = c.shape[0]
    ada_cols, qkv_cols = 3 * D_MODEL // ADA_STEPS, QKV_COLS // ADA_STEPS
    assert ada_cols % LANES == 0 and qkv_cols % LANES == 0
    col_block = lambda n: (0, n)
    return pl.pallas_call(
        _ada_kernel,
        grid=(ADA_STEPS,),
        in_specs=[pl.BlockSpec((batch, D_MODEL), lambda n: (0, 0)),
                  pl.BlockSpec((D_MODEL, ada_cols), col_block),
                  pl.BlockSpec((1, ada_cols), col_block),
                  pl.BlockSpec((D_MODEL, qkv_cols), col_block)],
        out_specs=[pl.BlockSpec((batch, ada_cols), col_block),
                   pl.BlockSpec((D_MODEL, qkv_cols), col_block)],
        out_shape=[jax.ShapeDtypeStruct((batch, 3 * D_MODEL), F32),
                   jax.ShapeDtypeStruct((D_MODEL, QKV_COLS), BF16)],
        compiler_params=pltpu.CompilerParams(dimension_semantics=("arbitrary",)),
        name="ada",
    )(c, w_ada, b_ada.reshape(1, 3 * D_MODEL), w_in)


def _rope_expand_matrix():
    e = np.zeros((64, 3 * LANES), np.float32)
    for lane in range(LANES):
        dim = lane % HEAD_DIM
        if dim < ROPE_DIMS:
            j = dim % ROPE_HALF
            for part in range(3):
                e[part * 8 + j, lane] = 1.0
                if dim >= ROPE_HALF:
                    e[24 + part * 8 + j, LANES + lane] = 1.0
                else:
                    e[24 + part * 8 + j, 2 * LANES + lane] = -1.0
        else:
            e[48, lane] = 1.0
    return e


def _head_mean_matrix():
    head = np.arange(2 * LANES) // HEAD_DIM
    return (head[:, None] == head[None, :]).astype(np.float32) / HEAD_DIM


def _split3(a):
    hi = a.astype(BF16).astype(F32)
    r = a - hi
    mid = r.astype(BF16).astype(F32)
    return hi, mid, r - mid


def _store_residue_major(out, val, dilation, perm_scr, mid_scr):
    tm, width = val.shape
    if dilation == 1:
        out[0, 0] = val.astype(out.dtype)
        return
    for cb in range(width // LANES):
        perm_scr[cb] = val[:, cb * LANES:(cb + 1) * LANES]
    two_pass = dilation > SUBLANE_STRIDE
    if two_pass:
        assert dilation == SUBLANE_STRIDE * SUBLANE_STRIDE
        n_u = tm // SUBLANE_STRIDE
        for cb in range(width // LANES):
            for rl in range(SUBLANE_STRIDE):
                mid_scr[cb, rl * n_u:(rl + 1) * n_u, :] = perm_scr[cb, pl.ds(rl, n_u, stride=SUBLANE_STRIDE), :]
    for r in range(dilation):
        for cb in range(width // LANES):
            if two_pass:
                rh, rl = divmod(r, SUBLANE_STRIDE)
                piece = mid_scr[cb, pl.ds(rl * n_u + rh, tm // dilation, stride=SUBLANE_STRIDE), :]
            else:
                piece = perm_scr[cb, pl.ds(r, tm // dilation, stride=dilation), :]
            out[0, r, :, cb * LANES:(cb + 1) * LANES] = piece.astype(out.dtype)


def _qkv_kernel(x_ref, pos_ref, ada_ref, norm_g_ref, gq_ref, gk_ref, freq_ref, e_ref, bd_ref, *refs):
    n_chunks = QKV_COLS // ATTN_WIDTH
    w_ref, out_refs, (perm_scr, mid_scr) = refs[0], refs[1:1 + n_chunks], refs[1 + n_chunks:]

    xf = x_ref[0]
    tm = xf.shape[0]
    hb = _ada_ln(xf, ada_ref, norm_g_ref).astype(BF16)

    ang = freq_ref[...] * pos_ref[pl.ds(pl.program_id(0), 1), :].astype(F32)
    parts = _split3(jnp.cos(ang)) + _split3(jnp.sin(ang))
    lhs_t = jnp.concatenate(list(parts) + [jnp.ones((8, tm), F32), jnp.zeros((8, tm), F32)], axis=0)
    tab = lax.dot_general(lhs_t.astype(BF16), e_ref[...], (((0,), (0,)), ((), ())),
                          preferred_element_type=F32)
    cos_t, s1_t, s2_t = tab[:, 0:LANES], tab[:, LANES:2 * LANES], tab[:, 2 * LANES:3 * LANES]

    n_groups = len(ATTN_GROUPS)
    chunks = ([(g, which) for which in (0, 1) for g in range(n_groups)]
              + [(g, 2) for g in reversed(range(n_groups))])
    for g, which in chunks:
        dilation = ATTN_GROUPS[g][1]
        col0 = g * GROUP_COLS + which * ATTN_WIDTH
        zc = _dot(hb, w_ref[:, col0:col0 + ATTN_WIDTH])
        out = out_refs[3 * g + which]
        if which == 2:
            _store_residue_major(out, zc, dilation, perm_scr, mid_scr)
            continue
        gain = gq_ref[g:g + 1, :] * Q_SCALE if which == 0 else gk_ref[g:g + 1, :]
        cols = []
        for half in range(ATTN_WIDTH // (2 * LANES)):
            a2 = zc[:, half * 2 * LANES:(half + 1) * 2 * LANES]
            ms = _dot((a2 * a2).astype(BF16), bd_ref[...])
            a2n = a2 * lax.rsqrt(ms + EPS)
            for sub in range(2):
                an = a2n[:, sub * LANES:(sub + 1) * LANES] * gain
                cols.append(an * cos_t + pltpu.roll(an, ROPE_HALF, 1) * s1_t
                            + pltpu.roll(an, LANES - ROPE_HALF, 1) * s2_t)
        _store_residue_major(out, jnp.concatenate(cols, axis=1), dilation, perm_scr, mid_scr)


def _qkv_call(x, positions, ada, norm_g2, w_qkv, gq, gk, freq, e_mat, bd_mat):
    batch, seq, _ = x.shape
    const = lambda b, j: (0, 0)
    out_specs, out_shapes = [], []
    for _, dilation in ATTN_GROUPS:
        spec = pl.BlockSpec((1, dilation, QKV_TILE // dilation, ATTN_WIDTH), lambda b, j: (b, 0, j, 0))
        sds = jax.ShapeDtypeStruct((batch, dilation, seq // dilation, ATTN_WIDTH), BF16)
        out_specs += [spec] * 3
        out_shapes += [sds] * 3
    n_groups = len(ATTN_GROUPS)
    return pl.pallas_call(
        _qkv_kernel,
        grid=(batch, seq // QKV_TILE),
        in_specs=[
            pl.BlockSpec((1, QKV_TILE, D_MODEL), lambda b, j: (b, j, 0)),
            pl.BlockSpec((batch, QKV_TILE), lambda b, j: (0, j)),
            pl.BlockSpec((batch, 3 * D_MODEL), const),
            pl.BlockSpec((1, D_MODEL), const),
            pl.BlockSpec((n_groups, LANES), const),
            pl.BlockSpec((n_groups, LANES), const),
            pl.BlockSpec((8, 1), const),
            pl.BlockSpec((64, 3 * LANES), const),
            pl.BlockSpec((2 * LANES, 2 * LANES), const),
            _resident((D_MODEL, QKV_COLS)),
        ],
        out_specs=out_specs,
        out_shape=out_shapes,
        scratch_shapes=[pltpu.VMEM((ATTN_WIDTH // LANES, QKV_TILE, LANES), F32)] * 2,
        compiler_params=pltpu.CompilerParams(
            dimension_semantics=("arbitrary", "arbitrary"), vmem_limit_bytes=VMEM_LIMIT_BYTES),
        name="qkv",
    )(x, positions, ada, norm_g2, gq, gk, freq, e_mat, bd_mat, w_qkv)


def _attn_kernel(q_ref, k_ref, v_ref, *refs, dilation, n_blk, token_order_out, n_cast):
    cast_in, (o_ref, lse_ref) = refs[:n_cast], refs[n_cast:n_cast + 2]
    cast_out, scratch = refs[n_cast + 2:2 * n_cast + 2], refs[2 * n_cast + 2:]
    for src, dst in zip(cast_in, cast_out):
        dst[...] = src[...].astype(dst.dtype)

    row = lax.broadcasted_iota(jnp.int32, (Q_BLOCK, LANES), 0)
    col = lax.broadcasted_iota(jnp.int32, (Q_BLOCK, LANES), 1)
    cur_ok = col <= row
    prev_ok = col >= row
    lane_lo = col < HEAD_DIM
    ones_lo = jnp.where(lane_lo, 1.0, 0.0).astype(BF16)
    ones_hi = jnp.where(lane_lo, 0.0, 1.0).astype(BF16)
    n_pairs = ATTN_WIDTH // LANES
    stage_in_vmem = n_blk > 1
    scratch = list(scratch)
    if stage_in_vmem:
        p_scr, m_scr = scratch.pop(0), scratch.pop(0)
    if dilation > 1 and token_order_out:
        o_plane, lse_plane = scratch

    def split_heads(a):
        zero = jnp.zeros_like(a)
        lo = lax.broadcasted_iota(jnp.int32, a.shape, 1) < HEAD_DIM
        return jnp.concatenate([jnp.where(lo, a, zero), jnp.where(lo, zero, a)], axis=0)

    def window(ref, r, r0, first, lanes):
        if first:
            return ref[0, r, r0:r0 + Q_BLOCK, lanes]
        return ref[0, r, r0 - Q_BLOCK:r0 + Q_BLOCK, lanes]

    def scores(r, r0):
        first = r0 == 0
        ok_head = cur_ok if first else jnp.concatenate([prev_ok, cur_ok], axis=1)
        ok = jnp.concatenate([ok_head, ok_head], axis=1)
        n_keys = ok_head.shape[1]
        stats = []
        for hp in range(n_pairs):
            lanes = slice(hp * LANES, (hp + 1) * LANES)
            q_pair = q_ref[0, r, r0:r0 + Q_BLOCK, lanes]
            k2 = split_heads(window(k_ref, r, r0, first, lanes))
            s = lax.dot_general(q_pair, k2, (((1,), (1,)), ((), ())), preferred_element_type=F32)
            s = jnp.where(ok, s, -jnp.inf)
            m0 = jnp.max(s[:, :n_keys], axis=-1, keepdims=True)
            m1 = jnp.max(s[:, n_keys:], axis=-1, keepdims=True)
            p = jnp.concatenate([jnp.exp2(s[:, :n_keys] - m0), jnp.exp2(s[:, n_keys:] - m1)], axis=1)
            p, m_ln2 = p.astype(BF16), jnp.where(lane_lo, m0, m1) * math.log(2.0)
            if stage_in_vmem:
                p_scr[hp, :, 0:2 * n_keys] = p
                m_scr[hp] = m_ln2
            stats.append((p, m_ln2))
        return stats

    def values(r, r0, stats):
        first = r0 == 0
        reps = 1 if first else 2
        ind = jnp.concatenate([ones_lo] * reps + [ones_hi] * reps, axis=0)
        for hp, (p, m_ln2) in enumerate(stats):
            if stage_in_vmem:
                p, m_ln2 = p_scr[hp, :, 0:2 * reps * Q_BLOCK], m_scr[hp]
            lanes = slice(hp * LANES, (hp + 1) * LANES)
            v2 = jnp.concatenate([split_heads(window(v_ref, r, r0, first, lanes)), ind], axis=1)
            o2 = _dot(p, v2)
            den = o2[:, LANES:]
            o_pair = o2[:, :LANES] * (1.0 / den)
            lse_pair = m_ln2 + jnp.log(den)
            if dilation == 1:
                o_ref[0, r0:r0 + Q_BLOCK, lanes] = o_pair.astype(o_ref.dtype)
                lse_ref[0, r0:r0 + Q_BLOCK, lanes] = lse_pair
            elif not token_order_out:
                o_ref[0, r, r0:r0 + Q_BLOCK, lanes] = o_pair.astype(o_ref.dtype)
                lse_ref[0, r, r0:r0 + Q_BLOCK, lanes] = lse_pair
            else:
                rows = pl.ds(r + r0 * dilation, Q_BLOCK, stride=dilation)
                o_plane[hp, rows, :] = o_pair
                lse_plane[hp, rows, :] = lse_pair

    blocks = [(r, blk * Q_BLOCK) for r in range(dilation) for blk in range(n_blk)]
    stats = scores(*blocks[0])
    for prev, cur in zip(blocks[:-1], blocks[1:]):
        if stage_in_vmem:
            values(*prev, stats)
            stats = scores(*cur)
        else:
            next_stats = scores(*cur)
            values(*prev, stats)
            stats = next_stats
    values(*blocks[-1], stats)

    if dilation > 1 and token_order_out:
        for hp in range(n_pairs):
            lanes = slice(hp * LANES, (hp + 1) * LANES)
            o_ref[0, :, lanes] = o_plane[hp].astype(o_ref.dtype)
            lse_ref[0, :, lanes] = lse_plane[hp]


def _attn_call(g, dilation, q, k, v, token_order_out, casts=()):
    batch, _, res_len, _ = q.shape
    cast_in_specs, cast_out_specs, cast_out_shapes = [], [], []
    for arr, block, first, out_dims in casts:
        axis = 0 if block[0] != out_dims[0] else 1
        assert out_dims[axis] == batch * block[axis]
        def index(b, axis=axis, shift=0):
            return (b + shift, 0) if axis == 0 else (0, b + shift)
        cast_in_specs.append(pl.BlockSpec(block, functools.partial(index, shift=first)))
        cast_out_specs.append(pl.BlockSpec(block, index))
        cast_out_shapes.append(jax.ShapeDtypeStruct(out_dims, BF16))
    seq = dilation * res_len
    in_spec = pl.BlockSpec((1, dilation, res_len, ATTN_WIDTH), lambda b: (b, 0, 0, 0))
    n_pairs = ATTN_WIDTH // LANES
    scratch = []
    if res_len > Q_BLOCK:
        scratch += [pltpu.VMEM((n_pairs, Q_BLOCK, 4 * Q_BLOCK), BF16),
                    pltpu.VMEM((n_pairs, Q_BLOCK, LANES), F32)]
    if token_order_out or dilation == 1:
        out_spec, out_dims = pl.BlockSpec((1, seq, ATTN_WIDTH), lambda b: (b, 0, 0)), (batch, seq, ATTN_WIDTH)
        if dilation > 1:
            scratch += [pltpu.VMEM((n_pairs, seq, LANES), F32)] * 2
    else:
        out_spec, out_dims = in_spec, q.shape
    return pl.pallas_call(
        functools.partial(_attn_kernel, dilation=dilation, n_blk=res_len // Q_BLOCK,
                          token_order_out=token_order_out, n_cast=len(casts)),
        grid=(batch,),
        in_specs=[in_spec, in_spec, in_spec] + cast_in_specs,
        out_specs=[out_spec, out_spec] + cast_out_specs,
        out_shape=[jax.ShapeDtypeStruct(out_dims, BF16), jax.ShapeDtypeStruct(out_dims, F32)] + cast_out_shapes,
        scratch_shapes=scratch,
        compiler_params=pltpu.CompilerParams(
            dimension_semantics=("arbitrary",), vmem_limit_bytes=VMEM_LIMIT_BYTES),
        name=f"attn_g{g}",
    )(q, k, v, *[arr for arr, _, _, _ in casts])


def _token_order(ref, mid_scr, scr):
    _, dilation, n, width = ref.shape
    sub = SUBLANE_STRIDE
    assert dilation == sub * sub
    cols = []
    for cb in range(width // LANES):
        for r in range(dilation):
            rh, rl = divmod(r, sub)
            mid_scr[cb, pl.ds(rl * sub * n + rh, n, stride=sub), :] = (
                ref[0, r, :, cb * LANES:(cb + 1) * LANES].astype(F32))
        for rl in range(sub):
            scr[cb, pl.ds(rl, sub * n, stride=sub), :] = mid_scr[cb, rl * sub * n:(rl + 1) * sub * n, :]
        cols.append(scr[cb])
    return jnp.concatenate(cols, axis=1)


def _merge_kernel(x_ref, ada_ref, norm_g_ref, w_ref,
                  o0_ref, o1_ref, o2_ref, l0_ref, l1_ref, l2_ref,
                  ln_g_ref, ln_b_ref, ws_ref, bias_ref, wa_ref, wb_ref, wo_ref, out_ref,
                  o2_mid, o2_scr, l2_mid, l2_scr):
    xf = x_ref[0]
    tm = xf.shape[0]
    h = _ada_ln(xf, ada_ref, norm_g_ref)
    z = _dot(h.astype(BF16), w_ref[...])
    off = 0
    gate_a = z[:, off:off + ATTN_WIDTH]; off += ATTN_WIDTH
    u = _gelu_exact(z[:, off:off + GMLP_WIDTH]); off += GMLP_WIDTH
    v = _gelu_exact(z[:, off:off + GMLP_WIDTH]); off += GMLP_WIDTH
    gate_b = z[:, off:off + GMLP_WIDTH]; off += GMLP_WIDTH
    merge_a = z[:, off:off + D_MODEL]; off += D_MODEL
    merge_b = z[:, off:off + D_MODEL]

    l0, l1, l2 = l0_ref[0], l1_ref[0], _token_order(l2_ref, l2_mid, l2_scr)
    lmax = jnp.maximum(jnp.maximum(l0, l1), l2)
    e0, e1, e2 = jnp.exp(l0 - lmax), jnp.exp(l1 - lmax), jnp.exp(l2 - lmax)
    attn = (e0 * o0_ref[0].astype(F32) + e1 * o1_ref[0].astype(F32) + e2 * _token_order(o2_ref, o2_mid, o2_scr)) \
        / (e0 + e1 + e2)
    y_a = attn * _silu(gate_a)

    mu = jnp.mean(v, axis=-1, keepdims=True)
    vc = v - mu
    var = jnp.mean(vc * vc, axis=-1, keepdims=True)
    v_ln = (vc * lax.rsqrt(var + EPS) * ln_g_ref[...] + ln_b_ref[...]).astype(BF16)
    row = lax.broadcasted_iota(jnp.int32, (CHUNK, CHUNK), 0)
    col = lax.broadcasted_iota(jnp.int32, (CHUNK, CHUNK), 1)
    causal = row >= col
    lane_lo = col < GMLP_GROUP_DIM
    n_chunks = tm // CHUNK
    sv_cols = []
    for gp in range(GMLP_WIDTH // LANES):
        lanes = slice(gp * LANES, (gp + 1) * LANES)
        rhs = jnp.concatenate([v_ln[c * CHUNK:(c + 1) * CHUNK, lanes] for c in range(n_chunks)], axis=1)
        res = []
        for half in range(2):
            w_s = jnp.where(causal, ws_ref[2 * gp + half], 0.0).astype(BF16)
            res.append(_dot(w_s, rhs))
        sel = jnp.concatenate(
            [jnp.where(lane_lo, res[0][:, c * LANES:(c + 1) * LANES], res[1][:, c * LANES:(c + 1) * LANES])
             + bias_ref[:, lanes] for c in range(n_chunks)], axis=0)
        sv_cols.append(sel)
    sv = jnp.concatenate(sv_cols, axis=1)
    y_b = u * sv * _silu(gate_b)

    merged = (_sigmoid(merge_a) * _dot(y_a.astype(BF16), wa_ref[...])
              + _sigmoid(merge_b) * _dot(y_b.astype(BF16), wb_ref[...]))
    out = _dot(merged.astype(BF16), wo_ref[...])
    gate = ada_ref[pl.ds(pl.program_id(0), 1), 2 * D_MODEL:3 * D_MODEL]
    out_ref[0] = xf + gate * out


def _merge_call(x, ada, norm_g2, w_rest, os_, lses, ln_g, ln_b, w_spatial, bias_tab, wa, wb, wo):
    batch, seq, _ = x.shape
    const2 = lambda b, i: (0, 0)
    tile = lambda width: pl.BlockSpec((1, MERGE_TILE, width), lambda b, i: (b, i, 0))
    last_dilation = ATTN_GROUPS[-1][1]
    res_tile = pl.BlockSpec((1, last_dilation, MERGE_TILE // last_dilation, ATTN_WIDTH),
                            lambda b, i: (b, 0, i, 0))
    staging = pltpu.VMEM((ATTN_WIDTH // LANES, MERGE_TILE, LANES), F32)
    return pl.pallas_call(
        _merge_kernel,
        grid=(batch, seq // MERGE_TILE),
        in_specs=[
            tile(D_MODEL),
            pl.BlockSpec((batch, 3 * D_MODEL), const2),
            pl.BlockSpec((1, D_MODEL), const2),
            _resident((D_MODEL, REST_COLS)),
            tile(ATTN_WIDTH), tile(ATTN_WIDTH), res_tile,
            tile(ATTN_WIDTH), tile(ATTN_WIDTH), res_tile,
            pl.BlockSpec((1, GMLP_WIDTH), const2),
            pl.BlockSpec((1, GMLP_WIDTH), const2),
            pl.BlockSpec((GMLP_GROUPS, CHUNK, CHUNK), lambda b, i: (0, 0, 0)),
            pl.BlockSpec((CHUNK, GMLP_WIDTH), const2),
            _resident((ATTN_WIDTH, D_MODEL)),
            _resident((GMLP_WIDTH, D_MODEL)),
            _resident((D_MODEL, D_MODEL)),
        ],
        out_specs=tile(D_MODEL),
        out_shape=jax.ShapeDtypeStruct(x.shape, x.dtype),
        scratch_shapes=[staging] * 4,
        compiler_params=pltpu.CompilerParams(
            dimension_semantics=("arbitrary", "arbitrary"), vmem_limit_bytes=VMEM_LIMIT_BYTES),
        name="merge",
    )(x, ada, norm_g2, w_rest, *os_, *lses, ln_g, ln_b, w_spatial, bias_tab, wa, wb, wo)


def kernel(x, c, positions, norm_g, w_ada, b_ada, w_in, q_norm_g, k_norm_g, sgu_ln_g, sgu_ln_b,
           w_spatial, b_spatial, w_branch_a, w_branch_b, w_out):
    batch, seq, d_model = x.shape
    assert d_model == D_MODEL and seq % QKV_TILE == 0 and seq % MERGE_TILE == 0
    assert w_in.shape == (D_MODEL, QKV_COLS + REST_COLS)

    ada, w_qkv_bf16 = _ada_call(c, w_ada, b_ada, w_in)
    norm_g2 = norm_g.reshape(1, D_MODEL)

    freq = (ROPE_THETA ** (-np.arange(0, ROPE_DIMS, 2, dtype=np.float32) / ROPE_DIMS)).astype(np.float32)
    freq = jnp.asarray(freq.reshape(ROPE_HALF, 1))
    e_mat = jnp.asarray(_rope_expand_matrix(), dtype=BF16)
    bd_mat = jnp.asarray(_head_mean_matrix(), dtype=BF16)

    gq = jnp.tile(q_norm_g, (1, LANES // HEAD_DIM))
    gk = jnp.tile(k_norm_g, (1, LANES // HEAD_DIM))
    qkv = _qkv_call(x, positions, ada, norm_g2, w_qkv_bf16, gq, gk, freq, e_mat, bd_mat)
    rest_block = REST_COLS // batch
    assert QKV_COLS % rest_block == 0
    casts = [
        [(w_in, (D_MODEL, rest_block), QKV_COLS // rest_block, (D_MODEL, REST_COLS))],
        [(w_branch_a, (ATTN_WIDTH // batch, D_MODEL), 0, w_branch_a.shape),
         (w_branch_b, (GMLP_WIDTH // batch, D_MODEL), 0, w_branch_b.shape),
         (w_out, (D_MODEL // batch, D_MODEL), 0, w_out.shape)],
        [],
    ]
    outs, lses, weights = [], [], []
    for g, (window, dilation) in enumerate(ATTN_GROUPS):
        assert window // dilation == Q_BLOCK
        o, lse, *w_bf16 = _attn_call(g, dilation, *qkv[3 * g:3 * g + 3],
                                     token_order_out=g + 1 < len(ATTN_GROUPS), casts=casts[g])
        outs.append(o)
        lses.append(lse)
        weights += w_bf16
    w_rest_bf16, wa_bf16, wb_bf16, wo_bf16 = weights

    bias_tab = jnp.repeat(b_spatial.T, GMLP_GROUP_DIM, axis=1)
    return _merge_call(x, ada, norm_g2, w_rest_bf16, outs, lses,
                       sgu_ln_g.reshape(1, GMLP_WIDTH), sgu_ln_b.reshape(1, GMLP_WIDTH),
                       w_spatial, bias_tab, wa_bf16, wb_bf16, wo_bf16)
```

```python
import functools
import math

import jax
import jax.numpy as jnp
import numpy as np
from jax import lax
from jax.experimental import pallas as pl
from jax.experimental.pallas import tpu as pltpu

D_MODEL = 1024
HEAD_DIM = 64
HEADS = 8
ATTN_GROUPS = ((128, 1), (512, 4), (2048, 16))
ATTN_WIDTH = HEADS * HEAD_DIM
Q_BLOCK = 128
ROPE_THETA = 500000.0
ROPE_DIMS = HEAD_DIM // 4
ROPE_HALF = ROPE_DIMS // 2
GMLP_WIDTH = 512
GMLP_GROUPS = 8
GMLP_GROUP_DIM = GMLP_WIDTH // GMLP_GROUPS
CHUNK = 128
EPS = 1e-6
GROUP_COLS = 3 * ATTN_WIDTH
QKV_COLS = len(ATTN_GROUPS) * GROUP_COLS
REST_COLS = ATTN_WIDTH + 3 * GMLP_WIDTH + 2 * D_MODEL

Q_SCALE = math.log2(math.e) / math.sqrt(HEAD_DIM)

LANES = 128
SUBLANE_STRIDE = 4
QKV_TILE = 1024
MERGE_TILE = 512
ADA_STEPS = 6
VMEM_LIMIT_BYTES = 56 * 1024 * 1024

F32 = jnp.float32
BF16 = jnp.bfloat16


def _dot(a, b):
    return jnp.dot(a, b, preferred_element_type=F32)


def _resident(shape):
    return pl.BlockSpec(shape, lambda *_: (0, 0), pipeline_mode=pl.Buffered(1))


def _silu(v):
    return v * (1.0 / (1.0 + jnp.exp(-v)))


def _sigmoid(v):
    return 1.0 / (1.0 + jnp.exp(-v))


def _gelu_exact(v):
    return 0.5 * v * (1.0 + lax.erf(v * (1.0 / math.sqrt(2.0))))


def _ada_ln(xf, ada_ref, norm_g_ref):
    ms = jnp.mean(xf * xf, axis=-1, keepdims=True)
    row = pl.ds(pl.program_id(0), 1)
    shift = ada_ref[row, 0:D_MODEL]
    scale = ada_ref[row, D_MODEL:2 * D_MODEL]
    return xf * lax.rsqrt(ms + EPS) * norm_g_ref[...] * (1.0 + scale) + shift


def _ada_kernel(c_ref, w_ref, b_ref, w_qkv_ref, o_ref, w_qkv_out):
    o_ref[...] = _dot(_silu(c_ref[...]).astype(BF16), w_ref[...].astype(BF16)) + b_ref[...]
    w_qkv_out[...] = w_qkv_ref[...].astype(BF16)


def _ada_call(c, w_ada, b_ada, w_in):
    batch = c.shape[0]
    ada_cols, qkv_cols = 3 * D_MODEL // ADA_STEPS, QKV_COLS // ADA_STEPS
    assert ada_cols % LANES == 0 and qkv_cols % LANES == 0
    col_block = lambda n: (0, n)
    return pl.pallas_call(
        _ada_kernel,
        grid=(ADA_STEPS,),
        in_specs=[pl.BlockSpec((batch, D_MODEL), lambda n: (0, 0)),
                  pl.BlockSpec((D_MODEL, ada_cols), col_block),
                  pl.BlockSpec((1, ada_cols), col_block),
                  pl.BlockSpec((D_MODEL, qkv_cols), col_block)],
        out_specs=[pl.BlockSpec((batch, ada_cols), col_block),
                   pl.BlockSpec((D_MODEL, qkv_cols), col_block)],
        out_shape=[jax.ShapeDtypeStruct((batch, 3 * D_MODEL), F32),
                   jax.ShapeDtypeStruct((D_MODEL, QKV_COLS), BF16)],
        compiler_params=pltpu.CompilerParams(dimension_semantics=("arbitrary",)),
        name="ada",
    )(c, w_ada, b_ada.reshape(1, 3 * D_MODEL), w_in)


def _rope_expand_matrix():
    e = np.zeros((64, 3 * LANES), np.float32)
    for lane in range(LANES):
        dim = lane % HEAD_DIM
        if dim < ROPE_DIMS:
            j = dim % ROPE_HALF
            for part in range(3):
                e[part * 8 + j, lane] = 1.0
                if dim >= ROPE_HALF:
                    e[24 + part * 8 + j, LANES + lane] = 1.0
                else:
                    e[24 + part * 8 + j, 2 * LANES + lane] = -1.0
        else:
            e[48, lane] = 1.0
    return e


def _head_mean_matrix():
    head = np.arange(2 * LANES) // HEAD_DIM
    return (head[:, None] == head[None, :]).astype(np.float32) / HEAD_DIM


def _split3(a):
    hi = a.astype(BF16).astype(F32)
    r = a - hi
    mid = r.astype(BF16).astype(F32)
    return hi, mid, r - mid


def _store_residue_major(out, val, dilation, perm_scr, mid_scr):
    tm, width = val.shape
    if dilation == 1:
        out[0, 0] = val.astype(out.dtype)
        return
    for cb in range(width // LANES):
        perm_scr[cb] = val[:, cb * LANES:(cb + 1) * LANES]
    two_pass = dilation > SUBLANE_STRIDE
    if two_pass:
        assert dilation == SUBLANE_STRIDE * SUBLANE_STRIDE
        n_u = tm // SUBLANE_STRIDE
        for cb in range(width // LANES):
            for rl in range(SUBLANE_STRIDE):
                mid_scr[cb, rl * n_u:(rl + 1) * n_u, :] = perm_scr[cb, pl.ds(rl, n_u, stride=SUBLANE_STRIDE), :]
    for r in range(dilation):
        for cb in range(width // LANES):
            if two_pass:
                rh, rl = divmod(r, SUBLANE_STRIDE)
                piece = mid_scr[cb, pl.ds(rl * n_u + rh, tm // dilation, stride=SUBLANE_STRIDE), :]
            else:
                piece = perm_scr[cb, pl.ds(r, tm // dilation, stride=dilation), :]
            out[0, r, :, cb * LANES:(cb + 1) * LANES] = piece.astype(out.dtype)


def _qkv_kernel(x_ref, pos_ref, ada_ref, norm_g_ref, gq_ref, gk_ref, freq_ref, e_ref, bd_ref, *refs):
    n_chunks = QKV_COLS // ATTN_WIDTH
    w_ref, out_refs, h_out = refs[0], refs[1:1 + n_chunks], refs[1 + n_chunks]
    perm_scr, mid_scr = refs[2 + n_chunks:]

    xf = x_ref[0]
    tm = xf.shape[0]
    hb = _ada_ln(xf, ada_ref, norm_g_ref).astype(BF16)
    h_out[0] = hb

    ang = freq_ref[...] * pos_ref[pl.ds(pl.program_id(0), 1), :].astype(F32)
    parts = _split3(jnp.cos(ang)) + _split3(jnp.sin(ang))
    lhs_t = jnp.concatenate(list(parts) + [jnp.ones((8, tm), F32), jnp.zeros((8, tm), F32)], axis=0)
    tab = lax.dot_general(lhs_t.astype(BF16), e_ref[...], (((0,), (0,)), ((), ())),
                          preferred_element_type=F32)
    cos_t, s1_t, s2_t = tab[:, 0:LANES], tab[:, LANES:2 * LANES], tab[:, 2 * LANES:3 * LANES]

    n_groups = len(ATTN_GROUPS)
    chunks = ([(g, which) for which in (0, 1) for g in range(n_groups)]
              + [(g, 2) for g in reversed(range(n_groups))])
    for g, which in chunks:
        dilation = ATTN_GROUPS[g][1]
        col0 = g * GROUP_COLS + which * ATTN_WIDTH
        zc = _dot(hb, w_ref[:, col0:col0 + ATTN_WIDTH])
        out = out_refs[3 * g + which]
        if which == 2:
            _store_residue_major(out, zc, dilation, perm_scr, mid_scr)
            continue
        gain = gq_ref[g:g + 1, :] * Q_SCALE if which == 0 else gk_ref[g:g + 1, :]
        cols = []
        for half in range(ATTN_WIDTH // (2 * LANES)):
            a2 = zc[:, half * 2 * LANES:(half + 1) * 2 * LANES]
            ms = _dot((a2 * a2).astype(BF16), bd_ref[...])
            a2n = a2 * lax.rsqrt(ms + EPS)
            for sub in range(2):
                an = a2n[:, sub * LANES:(sub + 1) * LANES] * gain
                cols.append(an * cos_t + pltpu.roll(an, ROPE_HALF, 1) * s1_t
                            + pltpu.roll(an, LANES - ROPE_HALF, 1) * s2_t)
        _store_residue_major(out, jnp.concatenate(cols, axis=1), dilation, perm_scr, mid_scr)


def _qkv_call(x, positions, ada, norm_g2, w_qkv, gq, gk, freq, e_mat, bd_mat):
    batch, seq, _ = x.shape
    const = lambda b, j: (0, 0)
    out_specs, out_shapes = [], []
    for _, dilation in ATTN_GROUPS:
        spec = pl.BlockSpec((1, dilation, QKV_TILE // dilation, ATTN_WIDTH), lambda b, j: (b, 0, j, 0))
        sds = jax.ShapeDtypeStruct((batch, dilation, seq // dilation, ATTN_WIDTH), BF16)
        out_specs += [spec] * 3
        out_shapes += [sds] * 3
    out_specs.append(pl.BlockSpec((1, QKV_TILE, D_MODEL), lambda b, j: (b, j, 0)))
    out_shapes.append(jax.ShapeDtypeStruct((batch, seq, D_MODEL), BF16))
    n_groups = len(ATTN_GROUPS)
    return pl.pallas_call(
        _qkv_kernel,
        grid=(batch, seq // QKV_TILE),
        in_specs=[
            pl.BlockSpec((1, QKV_TILE, D_MODEL), lambda b, j: (b, j, 0)),
            pl.BlockSpec((batch, QKV_TILE), lambda b, j: (0, j)),
            pl.BlockSpec((batch, 3 * D_MODEL), const),
            pl.BlockSpec((1, D_MODEL), const),
            pl.BlockSpec((n_groups, LANES), const),
            pl.BlockSpec((n_groups, LANES), const),
            pl.BlockSpec((8, 1), const),
            pl.BlockSpec((64, 3 * LANES), const),
            pl.BlockSpec((2 * LANES, 2 * LANES), const),
            _resident((D_MODEL, QKV_COLS)),
        ],
        out_specs=out_specs,
        out_shape=out_shapes,
        scratch_shapes=[pltpu.VMEM((ATTN_WIDTH // LANES, QKV_TILE, LANES), F32)] * 2,
        compiler_params=pltpu.CompilerParams(
            dimension_semantics=("arbitrary", "arbitrary"), vmem_limit_bytes=VMEM_LIMIT_BYTES),
        name="qkv",
    )(x, positions, ada, norm_g2, gq, gk, freq, e_mat, bd_mat, w_qkv)


def _attn_kernel(q_ref, k_ref, v_ref, *refs, dilation, n_blk, token_order_out, n_cast):
    cast_in, (o_ref, lse_ref) = refs[:n_cast], refs[n_cast:n_cast + 2]
    cast_out, scratch = refs[n_cast + 2:2 * n_cast + 2], refs[2 * n_cast + 2:]
    for src, dst in zip(cast_in, cast_out):
        dst[...] = src[...].astype(dst.dtype)

    row = lax.broadcasted_iota(jnp.int32, (Q_BLOCK, LANES), 0)
    col = lax.broadcasted_iota(jnp.int32, (Q_BLOCK, LANES), 1)
    cur_ok = col <= row
    prev_ok = col >= row
    lane_lo = col < HEAD_DIM
    ones_lo = jnp.where(lane_lo, 1.0, 0.0).astype(BF16)
    ones_hi = jnp.where(lane_lo, 0.0, 1.0).astype(BF16)
    n_pairs = ATTN_WIDTH // LANES
    stage_in_vmem = n_blk > 1
    scratch = list(scratch)
    if stage_in_vmem:
        p_scr, m_scr = scratch.pop(0), scratch.pop(0)
    if dilation > 1 and token_order_out:
        o_plane, lse_plane = scratch

    def split_heads(a):
        zero = jnp.zeros_like(a)
        lo = lax.broadcasted_iota(jnp.int32, a.shape, 1) < HEAD_DIM
        return jnp.concatenate([jnp.where(lo, a, zero), jnp.where(lo, zero, a)], axis=0)

    def window(ref, r, r0, first, lanes):
        if first:
            return ref[0, r, r0:r0 + Q_BLOCK, lanes]
        return ref[0, r, r0 - Q_BLOCK:r0 + Q_BLOCK, lanes]

    def scores(r, r0):
        first = r0 == 0
        ok_head = cur_ok if first else jnp.concatenate([prev_ok, cur_ok], axis=1)
        ok = jnp.concatenate([ok_head, ok_head], axis=1)
        n_keys = ok_head.shape[1]
        stats = []
        for hp in range(n_pairs):
            lanes = slice(hp * LANES, (hp + 1) * LANES)
            q_pair = q_ref[0, r, r0:r0 + Q_BLOCK, lanes]
            k2 = split_heads(window(k_ref, r, r0, first, lanes))
            s = lax.dot_general(q_pair, k2, (((1,), (1,)), ((), ())), preferred_element_type=F32)
            s = jnp.where(ok, s, -jnp.inf)
            m0 = jnp.max(s[:, :n_keys], axis=-1, keepdims=True)
            m1 = jnp.max(s[:, n_keys:], axis=-1, keepdims=True)
            p = jnp.concatenate([jnp.exp2(s[:, :n_keys] - m0), jnp.exp2(s[:, n_keys:] - m1)], axis=1)
            p, m_ln2 = p.astype(BF16), jnp.where(lane_lo, m0, m1) * math.log(2.0)
            if stage_in_vmem:
                p_scr[hp, :, 0:2 * n_keys] = p
                m_scr[hp] = m_ln2
            stats.append((p, m_ln2))
        return stats

    def values(r, r0, stats):
        first = r0 == 0
        reps = 1 if first else 2
        ind = jnp.concatenate([ones_lo] * reps + [ones_hi] * reps, axis=0)
        for hp, (p, m_ln2) in enumerate(stats):
            if stage_in_vmem:
                p, m_ln2 = p_scr[hp, :, 0:2 * reps * Q_BLOCK], m_scr[hp]
            lanes = slice(hp * LANES, (hp + 1) * LANES)
            v2 = jnp.concatenate([split_heads(window(v_ref, r, r0, first, lanes)), ind], axis=1)
            o2 = _dot(p, v2)
            den = o2[:, LANES:]
            o_pair = o2[:, :LANES] * (1.0 / den)
            lse_pair = m_ln2 + jnp.log(den)
            if dilation == 1:
                o_ref[0, r0:r0 + Q_BLOCK, lanes] = o_pair.astype(o_ref.dtype)
                lse_ref[0, r0:r0 + Q_BLOCK, lanes] = lse_pair
            elif not token_order_out:
                o_ref[0, r, r0:r0 + Q_BLOCK, lanes] = o_pair.astype(o_ref.dtype)
                lse_ref[0, r, r0:r0 + Q_BLOCK, lanes] = lse_pair
            else:
                rows = pl.ds(r + r0 * dilation, Q_BLOCK, stride=dilation)
                o_plane[hp, rows, :] = o_pair
                lse_plane[hp, rows, :] = lse_pair

    blocks = [(r, blk * Q_BLOCK) for r in range(dilation) for blk in range(n_blk)]
    stats = scores(*blocks[0])
    for prev, cur in zip(blocks[:-1], blocks[1:]):
        if stage_in_vmem:
            values(*prev, stats)
            stats = scores(*cur)
        else:
            next_stats = scores(*cur)
            values(*prev, stats)
            stats = next_stats
    values(*blocks[-1], stats)

    if dilation > 1 and token_order_out:
        for hp in range(n_pairs):
            lanes = slice(hp * LANES, (hp + 1) * LANES)
            o_ref[0, :, lanes] = o_plane[hp].astype(o_ref.dtype)
            lse_ref[0, :, lanes] = lse_plane[hp]


def _attn_call(g, dilation, q, k, v, token_order_out, casts=()):
    batch, _, res_len, _ = q.shape
    cast_in_specs, cast_out_specs, cast_out_shapes = [], [], []
    for arr, block, first, out_dims in casts:
        axis = 0 if block[0] != out_dims[0] else 1
        assert out_dims[axis] == batch * block[axis]
        def index(b, axis=axis, shift=0):
            return (b + shift, 0) if axis == 0 else (0, b + shift)
        cast_in_specs.append(pl.BlockSpec(block, functools.partial(index, shift=first)))
        cast_out_specs.append(pl.BlockSpec(block, index))
        cast_out_shapes.append(jax.ShapeDtypeStruct(out_dims, BF16))
    seq = dilation * res_len
    in_spec = pl.BlockSpec((1, dilation, res_len, ATTN_WIDTH), lambda b: (b, 0, 0, 0))
    n_pairs = ATTN_WIDTH // LANES
    scratch = []
    if res_len > Q_BLOCK:
        scratch += [pltpu.VMEM((n_pairs, Q_BLOCK, 4 * Q_BLOCK), BF16),
                    pltpu.VMEM((n_pairs, Q_BLOCK, LANES), F32)]
    if token_order_out or dilation == 1:
        out_spec, out_dims = pl.BlockSpec((1, seq, ATTN_WIDTH), lambda b: (b, 0, 0)), (batch, seq, ATTN_WIDTH)
        if dilation > 1:
            scratch += [pltpu.VMEM((n_pairs, seq, LANES), F32)] * 2
    else:
        out_spec, out_dims = in_spec, q.shape
    return pl.pallas_call(
        functools.partial(_attn_kernel, dilation=dilation, n_blk=res_len // Q_BLOCK,
                          token_order_out=token_order_out, n_cast=len(casts)),
        grid=(batch,),
        in_specs=[in_spec, in_spec, in_spec] + cast_in_specs,
        out_specs=[out_spec, out_spec] + cast_out_specs,
        out_shape=[jax.ShapeDtypeStruct(out_dims, BF16), jax.ShapeDtypeStruct(out_dims, F32)] + cast_out_shapes,
        scratch_shapes=scratch,
        compiler_params=pltpu.CompilerParams(
            dimension_semantics=("arbitrary",), vmem_limit_bytes=VMEM_LIMIT_BYTES),
        name=f"attn_g{g}",
    )(q, k, v, *[arr for arr, _, _, _ in casts])


def _token_order(ref, mid_scr, scr):
    _, dilation, n, width = ref.shape
    sub = SUBLANE_STRIDE
    assert dilation == sub * sub
    cols = []
    for cb in range(width // LANES):
        for r in range(dilation):
            rh, rl = divmod(r, sub)
            mid_scr[cb, pl.ds(rl * sub * n + rh, n, stride=sub), :] = (
                ref[0, r, :, cb * LANES:(cb + 1) * LANES].astype(F32))
        for rl in range(sub):
            scr[cb, pl.ds(rl, sub * n, stride=sub), :] = mid_scr[cb, rl * sub * n:(rl + 1) * sub * n, :]
        cols.append(scr[cb])
    return jnp.concatenate(cols, axis=1)


def _merge_kernel(x_ref, h_ref, ada_ref, w_ref,
                  o0_ref, o1_ref, o2_ref, l0_ref, l1_ref, l2_ref,
                  ln_g_ref, ln_b_ref, ws_ref, bias_ref, wa_ref, wb_ref, wo_ref, out_ref,
                  o2_mid, o2_scr, l2_mid, l2_scr):
    xf = x_ref[0]
    tm = xf.shape[0]
    z = _dot(h_ref[0], w_ref[...])
    off = 0
    gate_a = z[:, off:off + ATTN_WIDTH]; off += ATTN_WIDTH
    u = _gelu_exact(z[:, off:off + GMLP_WIDTH]); off += GMLP_WIDTH
    v = _gelu_exact(z[:, off:off + GMLP_WIDTH]); off += GMLP_WIDTH
    gate_b = z[:, off:off + GMLP_WIDTH]; off += GMLP_WIDTH
    merge_a = z[:, off:off + D_MODEL]; off += D_MODEL
    merge_b = z[:, off:off + D_MODEL]

    l0, l1, l2 = l0_ref[0], l1_ref[0], _token_order(l2_ref, l2_mid, l2_scr)
    lmax = jnp.maximum(jnp.maximum(l0, l1), l2)
    e0, e1, e2 = jnp.exp(l0 - lmax), jnp.exp(l1 - lmax), jnp.exp(l2 - lmax)
    attn = (e0 * o0_ref[0].astype(F32) + e1 * o1_ref[0].astype(F32) + e2 * _token_order(o2_ref, o2_mid, o2_scr)) \
        / (e0 + e1 + e2)
    y_a = attn * _silu(gate_a)

    mu = jnp.mean(v, axis=-1, keepdims=True)
    vc = v - mu
    var = jnp.mean(vc * vc, axis=-1, keepdims=True)
    v_ln = (vc * lax.rsqrt(var + EPS) * ln_g_ref[...] + ln_b_ref[...]).astype(BF16)
    row = lax.broadcasted_iota(jnp.int32, (CHUNK, CHUNK), 0)
    col = lax.broadcasted_iota(jnp.int32, (CHUNK, CHUNK), 1)
    causal = row >= col
    lane_lo = col < GMLP_GROUP_DIM
    n_chunks = tm // CHUNK
    sv_cols = []
    for gp in range(GMLP_WIDTH // LANES):
        lanes = slice(gp * LANES, (gp + 1) * LANES)
        rhs = jnp.concatenate([v_ln[c * CHUNK:(c + 1) * CHUNK, lanes] for c in range(n_chunks)], axis=1)
        res = []
        for half in range(2):
            w_s = jnp.where(causal, ws_ref[2 * gp + half], 0.0).astype(BF16)
            res.append(_dot(w_s, rhs))
        sel = jnp.concatenate(
            [jnp.where(lane_lo, res[0][:, c * LANES:(c + 1) * LANES], res[1][:, c * LANES:(c + 1) * LANES])
             + bias_ref[:, lanes] for c in range(n_chunks)], axis=0)
        sv_cols.append(sel)
    sv = jnp.concatenate(sv_cols, axis=1)
    y_b = u * sv * _silu(gate_b)

    merged = (_sigmoid(merge_a) * _dot(y_a.astype(BF16), wa_ref[...])
              + _sigmoid(merge_b) * _dot(y_b.astype(BF16), wb_ref[...]))
    out = _dot(merged.astype(BF16), wo_ref[...])
    gate = ada_ref[pl.ds(pl.program_id(0), 1), 2 * D_MODEL:3 * D_MODEL]
    out_ref[0] = xf + gate * out


def _merge_call(x, h, ada, w_rest, os_, lses, ln_g, ln_b, w_spatial, bias_tab, wa, wb, wo):
    batch, seq, _ = x.shape
    const2 = lambda b, i: (0, 0)
    tile = lambda width: pl.BlockSpec((1, MERGE_TILE, width), lambda b, i: (b, i, 0))
    last_dilation = ATTN_GROUPS[-1][1]
    res_tile = pl.BlockSpec((1, last_dilation, MERGE_TILE // last_dilation, ATTN_WIDTH),
                            lambda b, i: (b, 0, i, 0))
    staging = pltpu.VMEM((ATTN_WIDTH // LANES, MERGE_TILE, LANES), F32)
    return pl.pallas_call(
        _merge_kernel,
        grid=(batch, seq // MERGE_TILE),
        in_specs=[
            tile(D_MODEL),
            tile(D_MODEL),
            pl.BlockSpec((batch, 3 * D_MODEL), const2),
            _resident((D_MODEL, REST_COLS)),
            tile(ATTN_WIDTH), tile(ATTN_WIDTH), res_tile,
            tile(ATTN_WIDTH), tile(ATTN_WIDTH), res_tile,
            pl.BlockSpec((1, GMLP_WIDTH), const2),
            pl.BlockSpec((1, GMLP_WIDTH), const2),
            pl.BlockSpec((GMLP_GROUPS, CHUNK, CHUNK), lambda b, i: (0, 0, 0)),
            pl.BlockSpec((CHUNK, GMLP_WIDTH), const2),
            _resident((ATTN_WIDTH, D_MODEL)),
            _resident((GMLP_WIDTH, D_MODEL)),
            _resident((D_MODEL, D_MODEL)),
        ],
        out_specs=tile(D_MODEL),
        out_shape=jax.ShapeDtypeStruct(x.shape, x.dtype),
        scratch_shapes=[staging] * 4,
        compiler_params=pltpu.CompilerParams(
            dimension_semantics=("arbitrary", "arbitrary"), vmem_limit_bytes=VMEM_LIMIT_BYTES),
        name="merge",
    )(x, h, ada, w_rest, *os_, *lses, ln_g, ln_b, w_spatial, bias_tab, wa, wb, wo)


def kernel(x, c, positions, norm_g, w_ada, b_ada, w_in, q_norm_g, k_norm_g, sgu_ln_g, sgu_ln_b,
           w_spatial, b_spatial, w_branch_a, w_branch_b, w_out):
    batch, seq, d_model = x.shape
    assert d_model == D_MODEL and seq % QKV_TILE == 0 and seq % MERGE_TILE == 0
    assert w_in.shape == (D_MODEL, QKV_COLS + REST_COLS)

    ada, w_qkv_bf16 = _ada_call(c, w_ada, b_ada, w_in)
    norm_g2 = norm_g.reshape(1, D_MODEL)

    freq = (ROPE_THETA ** (-np.arange(0, ROPE_DIMS, 2, dtype=np.float32) / ROPE_DIMS)).astype(np.float32)
    freq = jnp.asarray(freq.reshape(ROPE_HALF, 1))
    e_mat = jnp.asarray(_rope_expand_matrix(), dtype=BF16)
    bd_mat = jnp.asarray(_head_mean_matrix(), dtype=BF16)

    gq = jnp.tile(q_norm_g, (1, LANES // HEAD_DIM))
    gk = jnp.tile(k_norm_g, (1, LANES // HEAD_DIM))
    *qkv, h_bf16 = _qkv_call(x, positions, ada, norm_g2, w_qkv_bf16, gq, gk, freq, e_mat, bd_mat)
    rest_block = REST_COLS // batch
    assert QKV_COLS % rest_block == 0
    casts = [
        [(w_in, (D_MODEL, rest_block), QKV_COLS // rest_block, (D_MODEL, REST_COLS))],
        [(w_branch_a, (ATTN_WIDTH // batch, D_MODEL), 0, w_branch_a.shape),
         (w_branch_b, (GMLP_WIDTH // batch, D_MODEL), 0, w_branch_b.shape),
         (w_out, (D_MODEL // batch, D_MODEL), 0, w_out.shape)],
        [],
    ]
    outs, lses, weights = [], [], []
    for g, (window, dilation) in enumerate(ATTN_GROUPS):
        assert window // dilation == Q_BLOCK
        o, lse, *w_bf16 = _attn_call(g, dilation, *qkv[3 * g:3 * g + 3],
                                     token_order_out=g + 1 < len(ATTN_GROUPS), casts=casts[g])
        outs.append(o)
        lses.append(lse)
        weights += w_bf16
    w_rest_bf16, wa_bf16, wb_bf16, wo_bf16 = weights

    bias_tab = jnp.repeat(b_spatial.T, GMLP_GROUP_DIM, axis=1)
    return _merge_call(x, h_bf16, ada, w_rest_bf16, outs, lses,
                       sgu_ln_g.reshape(1, GMLP_WIDTH), sgu_ln_b.reshape(1, GMLP_WIDTH),
                       w_spatial, bias_tab, wa_bf16, wb_bf16, wo_bf16)
```
